```python
import math
import numpy as np
import jax
import jax.numpy as jnp
from jax import lax

D_MODEL = 1024
BATCH = 4
SEQ = 4096
DEPTH = 1
DEC_BATCH = 128
DEC_SEQ = 1
PAST_LEN = 2048
PAGE_SIZE = 128

SSM_EXPAND = 2
D_INNER = SSM_EXPAND * D_MODEL
SSM_HEAD_DIM = 64
N_SSM_HEADS = D_INNER // SSM_HEAD_DIM
SSM_GROUPS = 4
SSM_STATE = 128
SSM_CONV = 4
CONV_DIM = D_INNER + 2 * SSM_GROUPS * SSM_STATE
SSD_CHUNK = 128
N_ATT_HEADS = 16
ATT_HEAD_DIM = 64
N_KV_HEADS = 4
Q_PER_KV = N_ATT_HEADS // N_KV_HEADS
ATT_WIDTH = N_ATT_HEADS * ATT_HEAD_DIM
ATT_SCALE = ATT_HEAD_DIM ** -0.5
ROPE_DIM = ATT_HEAD_DIM // 4
ROPE_THETA = 500000.0
CMP_LEN = 32
CMP_STRIDE = 16
CMP_HIDDEN = 2 * ATT_HEAD_DIM
SEL_LEN = 64
SEL_TOPK = 16
WINDOW = 512
Q_BLOCK = 128
N_KV_STREAMS = 4
N_KV_PROJ = 6
D_FF = 2816
FFN_CONV = 3
EPS = 1e-6
NEG = -1e30
BIG = 1e30
TINY = 1e-30
IN_SIZES = (D_INNER, CONV_DIM, N_SSM_HEADS, ATT_WIDTH, N_KV_PROJ * N_KV_HEADS * ATT_HEAD_DIM, 3 * N_ATT_HEADS, 2 * D_MODEL)
D_IN_PROJ = sum(IN_SIZES)

kernel_name = "hybrid_ssd_nsa_convffn_adaln_step"


def _rmsnorm(x, w):
    x32 = x.astype(jnp.float32)
    y = x32 * lax.rsqrt(jnp.mean(x32 * x32, axis=-1, keepdims=True) + EPS)
    return y.astype(x.dtype) * w


def _rope(x, pos):
    half = ROPE_DIM // 2
    inv_freq = ROPE_THETA ** (-jnp.arange(half, dtype=jnp.float32) / half)
    ang = pos.astype(jnp.float32)[:, None] * inv_freq[None, :]
    shape = (pos.shape[0],) + (1,) * (x.ndim - 3) + (half,)
    cos = jnp.cos(ang).reshape(shape)
    sin = jnp.sin(ang).reshape(shape)
    x1 = x[..., :half].astype(jnp.float32)
    x2 = x[..., half:ROPE_DIM].astype(jnp.float32)
    rot = jnp.concatenate([x1 * cos - x2 * sin, x2 * cos + x1 * sin], axis=-1).astype(x.dtype)
    return jnp.concatenate([rot, x[..., ROPE_DIM:]], axis=-1)


def _causal_dwconv(xp, w, b):
    k = w.shape[0]
    t = xp.shape[1] - k + 1
    out = xp[:, 0:t] * w[0]
    for i in range(1, k):
        out = out + xp[:, i:i + t] * w[i]
    return out + b


def _masked_softmax(s, mask):
    s = jnp.where(mask, s.astype(jnp.float32), NEG)
    e = jnp.where(mask, jnp.exp(s - jnp.max(s, axis=-1, keepdims=True)), 0.0)
    return e / jnp.maximum(jnp.sum(e, axis=-1, keepdims=True), TINY)


def _ssd(x, dt, a, bm, cm, h0):
    b, t, nh, hp = x.shape
    g, n = bm.shape[2], bm.shape[3]
    r = nh // g
    l = SSD_CHUNK if t % SSD_CHUNK == 0 else t
    nc = t // l
    xdt = (x.astype(jnp.float32) * dt[..., None]).reshape(b, nc, l, g, r, hp)
    cs = jnp.cumsum((dt * a).reshape(b, nc, l, g, r), axis=2)
    bc = bm.astype(jnp.float32).reshape(b, nc, l, g, n)
    cc = cm.astype(jnp.float32).reshape(b, nc, l, g, n)
    causal = jnp.tril(jnp.ones((l, l), dtype=bool))[None, None, :, :, None, None]
    seg = cs[:, :, :, None] - cs[:, :, None, :]
    lmat = jnp.exp(jnp.where(causal, seg, -jnp.inf))
    cb = jnp.einsum("bclgn,bcsgn->bclsg", cc, bc)
    y_diag = jnp.einsum("bclsgr,bcsgrp->bclgrp", cb[..., None] * lmat, xdt)
    decay = jnp.exp(cs[:, :, -1:] - cs)
    states = jnp.einsum("bclgn,bclgrp->bcgrpn", bc, xdt * decay[..., None])
    chunk_decay = jnp.exp(cs[:, :, -1])

    def step(h, inp):
        st, dec = inp
        return h * dec[..., None, None] + st, h

    h_last, h_prev = lax.scan(step, h0.astype(jnp.float32).reshape(b, g, r, hp, n),
                              (jnp.moveaxis(states, 1, 0), jnp.moveaxis(chunk_decay, 1, 0)))
    h_prev = jnp.moveaxis(h_prev, 0, 1)
    y_off = jnp.einsum("bclgn,bcgrpn->bclgrp", cc, h_prev) * jnp.exp(cs)[..., None]
    y = (y_diag + y_off).reshape(b, t, nh, hp).astype(x.dtype)
    return y, h_last.reshape(b, nh, hp, n).astype(x.dtype)


def _mamba(z, xbc, dt_raw, conv_buf, h0, p):
    gn = SSM_GROUPS * SSM_STATE
    xp = jnp.concatenate([conv_buf.astype(xbc.dtype), xbc], axis=1)
    new_buf = xp[:, xp.shape[1] - (SSM_CONV - 1):]
    xc = jax.nn.silu(_causal_dwconv(xp, p["ssm_conv_w"], p["ssm_conv_b"]))
    b, t = xc.shape[0], xc.shape[1]
    xs = xc[..., :D_INNER].reshape(b, t, N_SSM_HEADS, SSM_HEAD_DIM)
    bm = xc[..., D_INNER:D_INNER + gn].reshape(b, t, SSM_GROUPS, SSM_STATE)
    cm = xc[..., D_INNER + gn:].reshape(b, t, SSM_GROUPS, SSM_STATE)
    dt = jax.nn.softplus((dt_raw + p["ssm_dt_bias"]).astype(jnp.float32))
    a = -jnp.exp(p["ssm_A_log"].astype(jnp.float32))
    y, h = _ssd(xs, dt, a, bm, cm, h0)
    y = (y + xs * p["ssm_D"][:, None]).reshape(b, t, D_INNER)
    y = _rmsnorm(y * jax.nn.silu(z), p["ssm_norm_w"])
    return y, new_buf, h


def _compress(k, pe, w1, w2):
    b, length = k.shape[0], k.shape[1]
    nc = (length - CMP_LEN) // CMP_STRIDE + 1
    nsub = CMP_LEN // CMP_STRIDE
    nj = nc + nsub - 1
    sub = k[:, :nj * CMP_STRIDE].reshape(b, nj, CMP_STRIDE, N_KV_HEADS, ATT_HEAD_DIM)
    hid = jnp.einsum("ld,lde->e", pe, w1)
    for r in range(nsub):
        part = jnp.einsum("bjshd,sde->bjhe", sub, w1[r * CMP_STRIDE:(r + 1) * CMP_STRIDE])
        hid = hid + part[:, r:r + nc]
    return jnp.einsum("bnhe,ed->bnhd", jax.nn.silu(hid), w2)


def _overlap_matrix(nc, ns):
    cst = np.arange(nc)[:, None] * CMP_STRIDE
    sst = np.arange(ns)[None, :] * SEL_LEN
    ov = np.clip(np.minimum(cst + CMP_LEN, sst + SEL_LEN) - np.maximum(cst, sst), 0, None)
    return jnp.asarray(ov / CMP_STRIDE, dtype=jnp.float32)


def _nsa_global(q, t, kc, vc, ks, vs):
    b, nq = q.shape[0], q.shape[1]
    nc = kc.shape[1]
    cmp_end = jnp.arange(nc) * CMP_STRIDE + (CMP_LEN - 1)
    mask_c = (cmp_end[None, :] <= t[:, None])[None, :, None, None, :]
    p_c = _masked_softmax(jnp.einsum("bqhgd,bnhd->bqhgn", q, kc) * ATT_SCALE, mask_c)
    o_c = jnp.einsum("bqhgn,bnhd->bqhgd", p_c.astype(vc.dtype), vc)
    length = ks.shape[1]
    ns = -(-length // SEL_LEN)
    pad = ((0, 0), (0, ns * SEL_LEN - length), (0, 0), (0, 0))
    ks = jnp.pad(ks, pad)
    vs = jnp.pad(vs, pad)
    imp = jnp.einsum("bqhn,nj->bqhj", jnp.sum(p_c, axis=3), _overlap_matrix(nc, ns))
    blk = jnp.arange(ns)[None, :]
    cur = (t // SEL_LEN)[:, None]
    valid = blk * SEL_LEN <= t[:, None]
    forced = (blk == 0) | (blk == cur) | (blk == cur - 1)
    imp = jnp.where(valid[None, :, None, :], jnp.where(forced[None, :, None, :], BIG, imp), -BIG)
    n_top = min(SEL_TOPK, ns)
    _, idx = lax.top_k(imp, n_top)
    kb = ks.reshape(b, ns, SEL_LEN, N_KV_HEADS, ATT_HEAD_DIM).transpose(0, 3, 1, 2, 4)
    vb = vs.reshape(b, ns, SEL_LEN, N_KV_HEADS, ATT_HEAD_DIM).transpose(0, 3, 1, 2, 4)
    bi = jnp.arange(b)[:, None, None, None]
    hi = jnp.arange(N_KV_HEADS)[None, None, :, None]
    m = n_top * SEL_LEN
    kg = kb[bi, hi, idx].reshape(b, nq, N_KV_HEADS, m, ATT_HEAD_DIM)
    vg = vb[bi, hi, idx].reshape(b, nq, N_KV_HEADS, m, ATT_HEAD_DIM)
    kpos = (idx[..., None] * SEL_LEN + jnp.arange(SEL_LEN)).reshape(b, nq, N_KV_HEADS, m)
    mask_s = (kpos <= t[None, :, None, None])[:, :, :, None, :]
    p_s = _masked_softmax(jnp.einsum("bqhgd,bqhmd->bqhgm", q, kg) * ATT_SCALE, mask_s)
    o_s = jnp.einsum("bqhgm,bqhmd->bqhgd", p_s.astype(vg.dtype), vg)
    return o_c, o_s


def _nsa_window(q, t, kw, vw, kpos):
    d = t[:, None] - kpos[None, :]
    mask = ((d >= 0) & (d < WINDOW) & (kpos >= 0)[None, :])[None, :, None, None, :]
    p = _masked_softmax(jnp.einsum("bqhgd,bkhd->bqhgk", q, kw) * ATT_SCALE, mask)
    return jnp.einsum("bqhgk,bkhd->bqhgd", p.astype(vw.dtype), vw)


def _nsa_merge(o_c, o_s, o_w, gates):
    b, t = gates.shape[0], gates.shape[1]
    g = jax.nn.sigmoid(gates).reshape(b, t, N_KV_HEADS, Q_PER_KV, 3)
    o = g[..., 0:1] * o_c + g[..., 1:2] * o_s + g[..., 2:3] * o_w
    return o.reshape(b, t, ATT_WIDTH)


def _nsa_prompt(q, kv, gates, p):
    b, t = q.shape[0], q.shape[1]
    kc = _compress(kv[:, :, 0], p["cmp_pe_k"], p["cmp_w1_k"], p["cmp_w2_k"])
    vc = _compress(kv[:, :, 1], p["cmp_pe_v"], p["cmp_w1_v"], p["cmp_w2_v"])
    ks = kv[:, :, 2]
    vs = kv[:, :, 3]
    pad_w = ((0, 0), (WINDOW, 0), (0, 0), (0, 0))
    kw = jnp.pad(kv[:, :, 4], pad_w)
    vw = jnp.pad(kv[:, :, 5], pad_w)

    def one_block(i):
        q0 = i * Q_BLOCK
        qb = lax.dynamic_slice_in_dim(q, q0, Q_BLOCK, axis=1)
        gb = lax.dynamic_slice_in_dim(gates, q0, Q_BLOCK, axis=1)
        tq = q0 + jnp.arange(Q_BLOCK, dtype=jnp.int32)
        o_c, o_s = _nsa_global(qb, tq, kc, vc, ks, vs)
        kpos = q0 - WINDOW + jnp.arange(WINDOW + Q_BLOCK, dtype=jnp.int32)
        kwb = lax.dynamic_slice_in_dim(kw, q0, WINDOW + Q_BLOCK, axis=1)
        vwb = lax.dynamic_slice_in_dim(vw, q0, WINDOW + Q_BLOCK, axis=1)
        o_w = _nsa_window(qb, tq, kwb, vwb, kpos)
        return _nsa_merge(o_c, o_s, o_w, gb)

    out = lax.map(one_block, jnp.arange(t // Q_BLOCK, dtype=jnp.int32))
    out = jnp.moveaxis(out, 0, 1).reshape(b, t, ATT_WIDTH)
    win_buf = min(WINDOW, PAST_LEN)
    win_state = jnp.stack([kw[:, kw.shape[1] - win_buf:], vw[:, vw.shape[1] - win_buf:]], axis=2)
    return out, kv[:, :, :N_KV_STREAMS], win_state


def _nsa_sample(q, kv, gates, cache_nsa_kv, page_table, cache_win_kv, p):
    b, t = q.shape[0], q.shape[1]
    n_pages = page_table.shape[1]
    past_len = n_pages * PAGE_SIZE
    past = cache_nsa_kv[page_table].reshape(b, past_len, N_KV_STREAMS, N_KV_HEADS, ATT_HEAD_DIM)
    kv_rows = kv[:, :, :N_KV_STREAMS]
    full = jnp.concatenate([past.astype(kv.dtype), kv_rows], axis=1)
    kc = _compress(full[:, :, 0], p["cmp_pe_k"], p["cmp_w1_k"], p["cmp_w2_k"])
    vc = _compress(full[:, :, 1], p["cmp_pe_v"], p["cmp_w1_v"], p["cmp_w2_v"])
    tq = past_len + jnp.arange(t, dtype=jnp.int32)
    o_c, o_s = _nsa_global(q, tq, kc, vc, full[:, :, 2], full[:, :, 3])
    win_buf = cache_win_kv.shape[1]
    win = jnp.concatenate([cache_win_kv.astype(kv.dtype), kv[:, :, 4:]], axis=1)
    kpos = past_len - win_buf + jnp.arange(win_buf + t, dtype=jnp.int32)
    o_w = _nsa_window(q, tq, win[:, :, 0], win[:, :, 1], kpos)
    return _nsa_merge(o_c, o_s, o_w, gates), kv_rows, win[:, t:]


def _layer(x, c, pos, conv_buf, h0, ffn_buf, nsa_fn, p):
    mod = jnp.einsum("bd,de->be", jax.nn.silu(c), p["ada_w"]) + p["ada_b"]
    sh1, sc1, g1, sh2, sc2, g2 = jnp.split(mod[:, None, :], 6, axis=-1)
    h = _rmsnorm(x, p["norm1_w"]) * (1.0 + sc1) + sh1
    proj = h @ p["w_in"]
    z, xbc, dt_raw, q, kv, att_g, merge_g = jnp.split(proj, np.cumsum(IN_SIZES)[:-1].tolist(), axis=-1)
    b, t = x.shape[0], x.shape[1]
    y_ssm, conv_new, h_new = _mamba(z, xbc, dt_raw, conv_buf, h0, p)
    q = _rope(q.reshape(b, t, N_ATT_HEADS, ATT_HEAD_DIM), pos).reshape(b, t, N_KV_HEADS, Q_PER_KV, ATT_HEAD_DIM)
    kv = kv.reshape(b, t, N_KV_PROJ, N_KV_HEADS, ATT_HEAD_DIM)
    kv = kv.at[:, :, 0::2].set(_rope(kv[:, :, 0::2], pos))
    y_att, kv_rows, win_state = nsa_fn(q, kv, att_g)
    gm = jax.nn.sigmoid(merge_g)
    merged = gm[..., :D_MODEL] * (y_ssm @ p["w_ssm_out"]) + gm[..., D_MODEL:] * (y_att @ p["w_att_out"])
    x = x + g1 * (merged @ p["w_out"])
    h2 = _rmsnorm(x, p["norm2_w"]) * (1.0 + sc2) + sh2
    u = h2 @ p["ffn_w_up"]
    up = jnp.concatenate([ffn_buf.astype(u.dtype), u], axis=1)
    ffn_new = up[:, up.shape[1] - (FFN_CONV - 1):]
    u = _causal_dwconv(up, p["ffn_conv_w"], p["ffn_conv_b"])
    f = (jax.nn.silu(u[..., :D_FF]) * u[..., D_FF:]) @ p["ffn_w_down"]
    x = x + g2 * f
    return x, kv_rows, win_state, conv_new, h_new, ffn_new


def setup_inputs(seed: int = 0) -> dict:
    key = jax.random.key(seed)
    k = jax.random.split(key, 40)
    f32 = jnp.float32

    def nrm(kk, shape, scale):
        return jax.random.normal(kk, shape, f32) * scale

    n_pages = PAST_LEN // PAGE_SIZE
    n_used = DEC_BATCH * n_pages
    n_pool = (n_used * 5 + 3) // 4
    win_buf = min(WINDOW, PAST_LEN)
    page_table = jax.random.permutation(k[5], n_pool)[:n_used].reshape(DEC_BATCH, n_pages).astype(jnp.int32)
    dt_col = D_INNER + CONV_DIM
    w_in = nrm(k[10], (D_MODEL, D_IN_PROJ), D_MODEL ** -0.5)
    w_in = w_in.at[:, dt_col:dt_col + N_SSM_HEADS].multiply(0.1)
    dt0 = jnp.exp(jax.random.uniform(k[12], (N_SSM_HEADS,), f32, math.log(1e-3), math.log(1e-1)))
    return {
        "x_prompt": nrm(k[0], (BATCH, SEQ, D_MODEL), 1.0),
        "x_sample": nrm(k[1], (DEC_BATCH, DEC_SEQ, D_MODEL), 1.0),
        "c_prompt": nrm(k[2], (BATCH, D_MODEL), 1.0),
        "c_sample": nrm(k[3], (DEC_BATCH, D_MODEL), 1.0),
        "cache_nsa_kv": nrm(k[4], (n_pool, PAGE_SIZE, N_KV_STREAMS, N_KV_HEADS, ATT_HEAD_DIM), 1.0),
        "page_table": page_table,
        "cache_win_kv": nrm(k[6], (DEC_BATCH, win_buf, 2, N_KV_HEADS, ATT_HEAD_DIM), 1.0),
        "state_ssm": nrm(k[7], (DEC_BATCH, N_SSM_HEADS, SSM_HEAD_DIM, SSM_STATE), 0.3),
        "state_ssm_conv": nrm(k[8], (DEC_BATCH, SSM_CONV - 1, CONV_DIM), 1.0),
        "state_ffn_conv": nrm(k[9], (DEC_BATCH, FFN_CONV - 1, 2 * D_FF), 1.0),
        "ada_w": nrm(k[11], (D_MODEL, 6 * D_MODEL), 0.5 * D_MODEL ** -0.5),
        "ada_b": nrm(k[13], (6 * D_MODEL,), 0.01),
        "norm1_w": 1.0 + nrm(k[14], (D_MODEL,), 0.1),
        "norm2_w": 1.0 + nrm(k[15], (D_MODEL,), 0.1),
        "final_norm_w": 1.0 + nrm(k[16], (D_MODEL,), 0.1),
        "w_in": w_in,
        "ssm_conv_w": nrm(k[17], (SSM_CONV, CONV_DIM), 0.5),
        "ssm_conv_b": nrm(k[18], (CONV_DIM,), 0.01),
        "ssm_dt_bias": dt0 + jnp.log(-jnp.expm1(-dt0)),
        "ssm_A_log": jnp.log(jax.random.uniform(k[19], (N_SSM_HEADS,), f32, 1.0, 16.0)),
        "ssm_D": 1.0 + nrm(k[20], (N_SSM_HEADS,), 0.1),
        "ssm_norm_w": 1.0 + nrm(k[21], (D_INNER,), 0.1),
        "cmp_pe_k": nrm(k[22], (CMP_LEN, ATT_HEAD_DIM), 0.5),
        "cmp_w1_k": nrm(k[23], (CMP_LEN, ATT_HEAD_DIM, CMP_HIDDEN), (CMP_LEN * ATT_HEAD_DIM) ** -0.5),
        "cmp_w2_k": nrm(k[24], (CMP_HIDDEN, ATT_HEAD_DIM), 2.0 * CMP_HIDDEN ** -0.5),
        "cmp_pe_v": nrm(k[25], (CMP_LEN, ATT_HEAD_DIM), 0.5),
        "cmp_w1_v": nrm(k[26], (CMP_LEN, ATT_HEAD_DIM, CMP_HIDDEN), (CMP_LEN * ATT_HEAD_DIM) ** -0.5),
        "cmp_w2_v": nrm(k[27], (CMP_HIDDEN, ATT_HEAD_DIM), 2.0 * CMP_HIDDEN ** -0.5),
        "w_ssm_out": nrm(k[28], (D_INNER, D_MODEL), D_INNER ** -0.5),
        "w_att_out": nrm(k[29], (ATT_WIDTH, D_MODEL), ATT_WIDTH ** -0.5),
        "w_out": nrm(k[30], (D_MODEL, D_MODEL), D_MODEL ** -0.5),
        "ffn_w_up": nrm(k[31], (D_MODEL, 2 * D_FF), D_MODEL ** -0.5),
        "ffn_conv_w": nrm(k[32], (FFN_CONV, 2 * D_FF), 0.6),
        "ffn_conv_b": nrm(k[33], (2 * D_FF,), 0.01),
        "ffn_w_down": nrm(k[34], (D_FF, D_MODEL), D_FF ** -0.5),
    }


def reference(x_prompt, x_sample, c_prompt, c_sample, cache_nsa_kv, page_table, cache_win_kv, state_ssm,
              state_ssm_conv, state_ffn_conv, ada_w, ada_b, norm1_w, norm2_w, final_norm_w, w_in,
              ssm_conv_w, ssm_conv_b, ssm_dt_bias, ssm_A_log, ssm_D, ssm_norm_w,
              cmp_pe_k, cmp_w1_k, cmp_w2_k, cmp_pe_v, cmp_w1_v, cmp_w2_v,
              w_ssm_out, w_att_out, w_out, ffn_w_up, ffn_conv_w, ffn_conv_b, ffn_w_down):
    p = {
        "ada_w": ada_w, "ada_b": ada_b, "norm1_w": norm1_w, "norm2_w": norm2_w, "w_in": w_in,
        "ssm_conv_w": ssm_conv_w, "ssm_conv_b": ssm_conv_b, "ssm_dt_bias": ssm_dt_bias,
        "ssm_A_log": ssm_A_log, "ssm_D": ssm_D, "ssm_norm_w": ssm_norm_w,
        "cmp_pe_k": cmp_pe_k, "cmp_w1_k": cmp_w1_k, "cmp_w2_k": cmp_w2_k,
        "cmp_pe_v": cmp_pe_v, "cmp_w1_v": cmp_w1_v, "cmp_w2_v": cmp_w2_v,
        "w_ssm_out": w_ssm_out, "w_att_out": w_att_out, "w_out": w_out,
        "ffn_w_up": ffn_w_up, "ffn_conv_w": ffn_conv_w, "ffn_conv_b": ffn_conv_b, "ffn_w_down": ffn_w_down,
    }
    b, t = x_prompt.shape[0], x_prompt.shape[1]
    ds = x_sample.shape[1]
    past_len = page_table.shape[1] * PAGE_SIZE
    pos_p = jnp.arange(t, dtype=jnp.int32)
    pos_s = past_len + jnp.arange(ds, dtype=jnp.int32)
    dtype = x_prompt.dtype
    hp = x_prompt
    hs = x_sample
    for _ in range(DEPTH):
        hp, kv_p, win_p, conv_p, ssm_p, ffn_p = _layer(
            hp, c_prompt, pos_p,
            jnp.zeros((b, SSM_CONV - 1, CONV_DIM), dtype),
            jnp.zeros((b, N_SSM_HEADS, SSM_HEAD_DIM, SSM_STATE), dtype),
            jnp.zeros((b, FFN_CONV - 1, 2 * D_FF), dtype),
            lambda qq, kk, gg: _nsa_prompt(qq, kk, gg, p), p)
        hs, kv_s, win_s, conv_s, ssm_s, ffn_s = _layer(
            hs, c_sample, pos_s, state_ssm_conv, state_ssm, state_ffn_conv,
            lambda qq, kk, gg: _nsa_sample(qq, kk, gg, cache_nsa_kv, page_table, cache_win_kv, p), p)
    y_prompt = _rmsnorm(hp, final_norm_w)
    y_sample = _rmsnorm(hs, final_norm_w)
    return (y_prompt, y_sample, kv_p, kv_s, win_p, win_s, ssm_p, ssm_s, conv_p, conv_s, ffn_p, ffn_s)
```

```python
import functools

import numpy as np
import jax
import jax.numpy as jnp
from jax import lax
from jax.experimental import pallas as pl
from jax.experimental.pallas import tpu as pltpu

F32 = jnp.float32
BF16 = jnp.bfloat16
HIGHEST = lax.Precision.HIGHEST

D_MODEL = 1024
D_INNER = 2048
N_SSM_HEADS = 32
SSM_HEAD_DIM = 64
SSM_STATE = 128
SSM_GROUPS = 4
SSM_CONV = 4
CONV_DIM = 3072
SSD_CHUNK = 128
N_ATT_HEADS = 16
ATT_HEAD_DIM = 64
N_KV_HEADS = 4
Q_PER_KV = 4
ATT_WIDTH = 1024
ATT_SCALE = ATT_HEAD_DIM ** -0.5
ROPE_DIM = 16
ROPE_THETA = 500000.0
CMP_LEN = 32
CMP_STRIDE = 16
CMP_HIDDEN = 128
SEL_LEN = 64
SEL_TOPK = 16
WINDOW = 512
Q_BLOCK = 128
PAGE_SIZE = 128
D_FF = 2816
FFN_CONV = 3
EPS = 1e-6
NEG = -1e30
BIG = 1e30
TINY = 1e-30

COL_Z = 0
COL_XS = 2048
COL_Q = 4096
COL_GM = 5120
COL_BM = 7168
COL_CM = 7680
COL_KV = 8192
COL_TAIL = 9728
N_PROJ = 9856
PROJ_TN = 896
VMEM_LIMIT = 56 * 1024 * 1024


def _sig(x):
    return 1.0 / (1.0 + jnp.exp(-x))


def _silu(x):
    return x * _sig(x)


def _softplus(x):
    return jnp.maximum(x, 0.0) + jnp.log1p(jnp.exp(-jnp.abs(x)))


def _dot(a, b):
    return jnp.dot(a, b, preferred_element_type=F32)


def _dot_hi(a, b):
    return jnp.dot(a, b, preferred_element_type=F32, precision=HIGHEST)


def _dot_nt(a, b):
    return lax.dot_general(a, b, (((1,), (1,)), ((), ())), preferred_element_type=F32)


def _dot_nt_hi(a, b):
    return lax.dot_general(a, b, (((1,), (1,)), ((), ())), preferred_element_type=F32, precision=HIGHEST)


def _params(sem):
    return pltpu.CompilerParams(dimension_semantics=sem, vmem_limit_bytes=VMEM_LIMIT)


def _ada_kernel(c_ref, w_ref, b_ref, o_ref):
    c = c_ref[...]
    o_ref[...] = _dot(_silu(c).astype(BF16), w_ref[...]) + b_ref[...]


def _ada_mod(c_all, w_bf, b_row):
    m = c_all.shape[0]
    tn = 512
    return pl.pallas_call(
        _ada_kernel,
        grid=(w_bf.shape[1] // tn,),
        in_specs=[pl.BlockSpec((m, D_MODEL), lambda j: (0, 0)),
                  pl.BlockSpec((D_MODEL, tn), lambda j: (0, j)),
                  pl.BlockSpec((1, tn), lambda j: (0, j))],
        out_specs=pl.BlockSpec((m, tn), lambda j: (0, j)),
        out_shape=jax.ShapeDtypeStruct((m, w_bf.shape[1]), F32),
        compiler_params=_params(("arbitrary",)),
        name="ada_mod",
    )(c_all, w_bf, b_row)


def _inproj_kernel(x_ref, sh_ref, sc_ref, nw_ref, w_ref, o_ref, h_scr):
    @pl.when(pl.program_id(2) == 0)
    def _():
        x = x_ref[...]
        y = x * lax.rsqrt(jnp.mean(x * x, axis=-1, keepdims=True) + EPS)
        h = y * nw_ref[...] * (1.0 + sc_ref[...]) + sh_ref[...]
        h_scr[...] = h.astype(BF16)

    o_ref[...] = _dot(h_scr[...], w_ref[...])


def _inproj(x3, mod3, nw_row, w_bf, tm):
    bx, tx, _ = x3.shape
    r = mod3.shape[1]
    mod_b = (lambda b: b) if mod3.shape[0] == bx else (lambda b: 0)
    return pl.pallas_call(
        _inproj_kernel,
        grid=(bx, tx // tm, N_PROJ // PROJ_TN),
        in_specs=[pl.BlockSpec((None, tm, D_MODEL), lambda b, i, j: (b, i, 0)),
                  pl.BlockSpec((None, r, D_MODEL), lambda b, i, j: (mod_b(b), 0, 0)),
                  pl.BlockSpec((None, r, D_MODEL), lambda b, i, j: (mod_b(b), 0, 1)),
                  pl.BlockSpec((1, D_MODEL), lambda b, i, j: (0, 0)),
                  pl.BlockSpec((D_MODEL, PROJ_TN), lambda b, i, j: (0, j))],
        out_specs=pl.BlockSpec((None, tm, PROJ_TN), lambda b, i, j: (b, i, j)),
        out_shape=jax.ShapeDtypeStruct((bx, tx, N_PROJ), F32),
        scratch_shapes=[pltpu.VMEM((tm, D_MODEL), BF16)],
        compiler_params=_params(("arbitrary", "arbitrary", "arbitrary")),
        name="inproj",
    )(x3, mod3, mod3, nw_row, w_bf)


def _ssd_kernel(z_ref, xs_ref, bm_ref, cm_ref, dt_ref, cw_ref, cb_ref, dtb_ref, alog_ref, dvec_ref, nw_ref,
                y_ref, conv_ref, hout_ref, xpad, h_t, *, nchunks):
    c = pl.program_id(1)
    ln = SSD_CHUNK

    @pl.when(c == 0)
    def _():
        xpad[0:8, :] = jnp.zeros((8, CONV_DIM), F32)
        h_t[...] = jnp.zeros_like(h_t)

    xpad[8:8 + ln, 0:2048] = xs_ref[...]
    xpad[8:8 + ln, 2048:2560] = bm_ref[...]
    xpad[8:8 + ln, 2560:3072] = cm_ref[...]
    cw = cw_ref[...]
    conv = (xpad[5:5 + ln, :] * cw[0:1, :] + xpad[6:6 + ln, :] * cw[1:2, :] + xpad[7:7 + ln, :] * cw[2:3, :]
            + xpad[8:8 + ln, :] * cw[3:4, :] + cb_ref[...])
    xc = _silu(conv)
    xs_c = xc[:, 0:2048]
    bm_c = xc[:, 2048:2560]
    cm_c = xc[:, 2560:3072]

    dt = _softplus(dt_ref[...] + dtb_ref[...])
    a = -jnp.exp(alog_ref[...])
    row = lax.broadcasted_iota(jnp.int32, (ln, ln), 0)
    col = lax.broadcasted_iota(jnp.int32, (ln, ln), 1)
    causal = row >= col
    cs = _dot_hi(causal.astype(F32), dt * a)
    cs_t = cs.T
    dt_t = dt.T

    x_bf = xs_c.astype(BF16)
    bm_bf = bm_c.astype(BF16)
    cm_bf = cm_c.astype(BF16)
    lo = lax.broadcasted_iota(jnp.int32, (1, 128), 1) < 64
    ys = []
    for g in range(SSM_GROUPS):
        bg = bm_bf[:, g * 128:(g + 1) * 128]
        cg = cm_bf[:, g * 128:(g + 1) * 128]
        cb = _dot_nt(cg, bg)
        b_t = bm_c[:, g * 128:(g + 1) * 128].T
        for r2 in range(4):
            pair = g * 4 + r2
            xp = x_bf[:, pair * 128:(pair + 1) * 128]
            yd, st = [], []
            for h in (2 * pair, 2 * pair + 1):
                cs_col = cs[:, h:h + 1]
                cs_row = cs_t[h:h + 1, :]
                dt_row = dt_t[h:h + 1, :]
                lmat = jnp.where(causal, jnp.exp(cs_col - cs_row), 0.0)
                yd.append(_dot((cb * lmat * dt_row).astype(BF16), xp))
                w_row = dt_row * jnp.exp(cs_t[h:h + 1, ln - 1:ln] - cs_row)
                st.append(_dot((b_t * w_row).astype(BF16), xp))
            ha, hb = 2 * pair, 2 * pair + 1
            ecol = jnp.where(lo, jnp.exp(cs[:, ha:ha + 1]), jnp.exp(cs[:, hb:hb + 1]))
            hprev = h_t[pair]
            yoff = _dot(cg, hprev.astype(BF16)) * ecol
            ys.append(jnp.where(lo, yd[0], yd[1]) + yoff)
            edec = jnp.where(lo, jnp.exp(cs_t[ha:ha + 1, ln - 1:ln]), jnp.exp(cs_t[hb:hb + 1, ln - 1:ln]))
            h_t[pair] = hprev * edec + jnp.where(lo, st[0], st[1])
    y = jnp.concatenate(ys, axis=1) + xs_c * dvec_ref[...]
    yz = y * _silu(z_ref[...])
    ms = jnp.mean(yz * yz, axis=-1, keepdims=True)
    y_ref[...] = (yz * lax.rsqrt(ms + EPS) * nw_ref[...]).astype(BF16)

    xpad[0:8, :] = xpad[ln:ln + 8, :]

    @pl.when(c == nchunks - 1)
    def _():
        conv_ref[...] = xpad[ln + 5:ln + 8, :]
        for pair in range(16):
            hout_ref[pair * 128:(pair + 1) * 128, :] = h_t[pair].T


def _ssd_prompt(proj3, cw, cb_row, dtb_row, alog_row, dvec_row, nw_row):
    bx, tx, _ = proj3.shape
    nchunks = tx // SSD_CHUNK
    ln = SSD_CHUNK
    const = lambda shape: pl.BlockSpec(shape, lambda b, c: (0, 0))
    return pl.pallas_call(
        functools.partial(_ssd_kernel, nchunks=nchunks),
        grid=(bx, nchunks),
        in_specs=[pl.BlockSpec((None, ln, 2048), lambda b, c: (b, c, COL_Z // 2048)),
                  pl.BlockSpec((None, ln, 2048), lambda b, c: (b, c, COL_XS // 2048)),
                  pl.BlockSpec((None, ln, 512), lambda b, c: (b, c, COL_BM // 512)),
                  pl.BlockSpec((None, ln, 512), lambda b, c: (b, c, COL_CM // 512)),
                  pl.BlockSpec((None, ln, 128), lambda b, c: (b, c, COL_TAIL // 128)),
                  const((SSM_CONV, CONV_DIM)), const((1, CONV_DIM)), const((1, 128)), const((1, 128)),
                  const((1, D_INNER)), const((1, D_INNER))],
        out_specs=[pl.BlockSpec((None, ln, D_INNER), lambda b, c: (b, c, 0)),
                   pl.BlockSpec((None, SSM_CONV - 1, CONV_DIM), lambda b, c: (b, 0, 0)),
                   pl.BlockSpec((None, N_SSM_HEADS * SSM_HEAD_DIM, SSM_STATE), lambda b, c: (b, 0, 0))],
        out_shape=[jax.ShapeDtypeStruct((bx, tx, D_INNER), BF16),
                   jax.ShapeDtypeStruct((bx, SSM_CONV - 1, CONV_DIM), F32),
                   jax.ShapeDtypeStruct((bx, N_SSM_HEADS * SSM_HEAD_DIM, SSM_STATE), F32)],
        scratch_shapes=[pltpu.VMEM((ln + 8, CONV_DIM), F32), pltpu.VMEM((16, 128, 128), F32)],
        compiler_params=_params(("arbitrary", "arbitrary")),
        name="ssd_prompt",
    )(proj3, proj3, proj3, proj3, proj3, cw, cb_row, dtb_row, alog_row, dvec_row, nw_row)


def _ssm_pre_kernel(xs_ref, bm_ref, cm_ref, dt_ref, s0_ref, s1_ref, s2_ref, cw_ref, cb_ref, dtb_ref, alog_ref,
                    dvec_ref, eh_ref, xdt_ref, dec_ref, yd_ref, bmc_ref, cmc_ref, xdt_t_ref, dec_t_ref):
    cw = cw_ref[...]
    xbc = jnp.concatenate([xs_ref[...], bm_ref[...], cm_ref[...]], axis=1)
    conv = s0_ref[...] * cw[0:1, :] + s1_ref[...] * cw[1:2, :] + s2_ref[...] * cw[2:3, :] + xbc * cw[3:4, :] + cb_ref[...]
    xc = _silu(conv)
    xs_c = xc[:, 0:2048]
    bm_c = xc[:, 2048:2560]
    cm_c = xc[:, 2560:3072]
    dt = _softplus(dt_ref[...] + dtb_ref[...])
    a = -jnp.exp(alog_ref[...])
    dec = jnp.exp(dt * a)
    eh = eh_ref[...]
    dt_e = _dot_hi(dt, eh)
    dec_e = _dot_hi(dec, eh)
    xdt = xs_c * dt_e
    cbs = []
    for g in range(SSM_GROUPS):
        cbg = jnp.sum(cm_c[:, g * 128:(g + 1) * 128] * bm_c[:, g * 128:(g + 1) * 128], axis=-1, keepdims=True)
        cbs.append(jnp.broadcast_to(cbg, (cbg.shape[0], 512)))
    cb_e = jnp.concatenate(cbs, axis=1)
    xdt_ref[...] = xdt
    dec_ref[...] = dec_e
    yd_ref[...] = cb_e * xdt + xs_c * dvec_ref[...]
    bmc_ref[...] = bm_c
    cmc_ref[...] = cm_c
    for k in range(16):
        xdt_t_ref[k * 128:(k + 1) * 128, :] = xdt[:, k * 128:(k + 1) * 128].T
        dec_t_ref[k * 128:(k + 1) * 128, :] = dec_e[:, k * 128:(k + 1) * 128].T


def _ssm_pre(proj3, s0, s1, s2, cw, cb_row, dtb_row, alog_row, dvec_row, eh):
    nb = proj3.shape[1]
    const = lambda shape: pl.BlockSpec(shape, lambda i: (0,) * len(shape))
    return pl.pallas_call(
        _ssm_pre_kernel,
        grid=(1,),
        in_specs=[pl.BlockSpec((None, nb, 2048), lambda i: (0, 0, COL_XS // 2048)),
                  pl.BlockSpec((None, nb, 512), lambda i: (0, 0, COL_BM // 512)),
                  pl.BlockSpec((None, nb, 512), lambda i: (0, 0, COL_CM // 512)),
                  pl.BlockSpec((None, nb, 128), lambda i: (0, 0, COL_TAIL // 128)),
                  const((nb, CONV_DIM)), const((nb, CONV_DIM)), const((nb, CONV_DIM)),
                  const((SSM_CONV, CONV_DIM)), const((1, CONV_DIM)), const((1, 128)), const((1, 128)),
                  const((1, D_INNER)), const((128, D_INNER))],
        out_specs=[const((nb, D_INNER)), const((nb, D_INNER)), const((nb, D_INNER)), const((nb, 512)),
                   const((nb, 512)), const((D_INNER, nb)), const((D_INNER, nb))],
        out_shape=[jax.ShapeDtypeStruct((nb, D_INNER), F32)] * 3 + [jax.ShapeDtypeStruct((nb, 512), F32)] * 2
        + [jax.ShapeDtypeStruct((D_INNER, nb), F32)] * 2,
        compiler_params=_params(("arbitrary",)),
        name="ssm_pre",
    )(proj3, proj3, proj3, proj3, s0, s1, s2, cw, cb_row, dtb_row, alog_row, dvec_row, eh)


def _ssm_state_kernel(h0_ref, xdt_t_ref, dec_t_ref, bm_ref, cm_ref, hn_ref, yoff_ref):
    b = pl.program_id(0)
    h0 = h0_ref[...]
    onehot = (lax.broadcasted_iota(jnp.int32, (128, 128), 0) == b).astype(F32)
    dec_b = _dot_hi(dec_t_ref[...], onehot)
    outs, yoffs = [], []
    for g in range(SSM_GROUPS):
        bc_row = bm_ref[:, g * 128:(g + 1) * 128]
        outs.append(_dot_hi(xdt_t_ref[g * 512:(g + 1) * 512, :], onehot * bc_row))
        cm_row = cm_ref[:, g * 128:(g + 1) * 128]
        cm8 = jnp.broadcast_to(cm_row, (8, 128)).astype(BF16)
        yoffs.append(_dot_nt(cm8, h0[g * 512:(g + 1) * 512, :].astype(BF16)))
    hn_ref[...] = h0 * dec_b + jnp.concatenate(outs, axis=0)
    yoff_ref[...] = jnp.concatenate(yoffs, axis=1)


def _ssm_state(h0, xdt_t, dec_t, bm_c, cm_c):
    nb = h0.shape[0]
    const = lambda shape: pl.BlockSpec(shape, lambda b: (0, 0))
    return pl.pallas_call(
        _ssm_state_kernel,
        grid=(nb,),
        in_specs=[pl.BlockSpec((None, D_INNER, SSM_STATE), lambda b: (b, 0, 0)),
                  const((D_INNER, nb)), const((D_INNER, nb)),
                  pl.BlockSpec((None, 1, 512), lambda b: (b, 0, 0)), pl.BlockSpec((None, 1, 512), lambda b: (b, 0, 0))],
        out_specs=[pl.BlockSpec((None, D_INNER, SSM_STATE), lambda b: (b, 0, 0)),
                   pl.BlockSpec((None, 8, D_INNER), lambda b: (b, 0, 0))],
        out_shape=[jax.ShapeDtypeStruct((nb, D_INNER, SSM_STATE), F32),
                   jax.ShapeDtypeStruct((nb, 8, D_INNER), F32)],
        compiler_params=_params(("arbitrary",)),
        name="ssm_state",
    )(h0, xdt_t, dec_t, bm_c.reshape(nb, 1, 512), cm_c.reshape(nb, 1, 512))


def _ssm_post_kernel(yd_ref, yoff_ref, dec_ref, z_ref, nw_ref, y_ref):
    y = yd_ref[...] + yoff_ref[...] * dec_ref[...]
    yz = y * _silu(z_ref[...])
    ms = jnp.mean(yz * yz, axis=-1, keepdims=True)
    y_ref[...] = (yz * lax.rsqrt(ms + EPS) * nw_ref[...]).astype(BF16)


def _ssm_post(yd, yoff, dec_e, proj3, nw_row):
    nb = yd.shape[0]
    const = lambda shape: pl.BlockSpec(shape, lambda i: (0, 0))
    return pl.pallas_call(
        _ssm_post_kernel,
        grid=(1,),
        in_specs=[const((nb, D_INNER)), const((nb, D_INNER)), const((nb, D_INNER)),
                  pl.BlockSpec((None, nb, 2048), lambda i: (0, 0, COL_Z // 2048)), const((1, D_INNER))],
        out_specs=const((nb, D_INNER)),
        out_shape=jax.ShapeDtypeStruct((nb, D_INNER), BF16),
        compiler_params=_params(("arbitrary",)),
        name="ssm_post",
    )(yd, yoff, dec_e, proj3, nw_row)


def _rope128(x, c, s1, s2):
    return x * c + pltpu.roll(x, 120, axis=1) * s1 + pltpu.roll(x, 8, axis=1) * s2


def _rope_kernel(q_ref, k01_ref, k23_ref, k45_ref, tail_ref, c_ref, s1_ref, s2_ref,
                 qo_ref, kvrows_ref, khm_ref, gates_ref, win_ref, *, ntiles):
    c = c_ref[...]
    s1 = s1_ref[...]
    s2 = s2_ref[...]
    q = q_ref[...]
    qo_ref[...] = jnp.concatenate(
        [_rope128(q[:, k * 128:(k + 1) * 128], c, s1, s2) for k in range(8)], axis=1
    ).astype(BF16) * jnp.asarray(ATT_SCALE, BF16)
    streams = []
    for pref in (k01_ref, k23_ref, k45_ref):
        blk = pref[...]
        kk = jnp.concatenate([_rope128(blk[:, k * 128:(k + 1) * 128], c, s1, s2) for k in range(2)], axis=1)
        streams.append(kk)
        streams.append(blk[:, 256:512])
    kvrows_ref[...] = jnp.concatenate(streams[0:4], axis=1)
    for s in range(6):
        sb = streams[s].astype(BF16)
        for h in range(N_KV_HEADS):
            khm_ref[s * 4 + h] = sb[:, h * 64:(h + 1) * 64]
    g = _sig(tail_ref[...])
    for hk in range(N_KV_HEADS):
        gates_ref[hk] = g[:, 32 + hk * 12:32 + (hk + 1) * 12]

    @pl.when(pl.program_id(1) == ntiles - 1)
    def _():
        win_ref[...] = jnp.concatenate(streams[4:6], axis=1)


def _rope_prep(proj3, ctab, s1tab, s2tab, tr):
    bx, tx, _ = proj3.shape
    ntiles = tx // tr
    tab = pl.BlockSpec((tr, 128), lambda b, i: (i, 0))
    return pl.pallas_call(
        functools.partial(_rope_kernel, ntiles=ntiles),
        grid=(bx, ntiles),
        in_specs=[pl.BlockSpec((None, tr, 1024), lambda b, i: (b, i, COL_Q // 1024)),
                  pl.BlockSpec((None, tr, 512), lambda b, i: (b, i, COL_KV // 512)),
                  pl.BlockSpec((None, tr, 512), lambda b, i: (b, i, COL_KV // 512 + 1)),
                  pl.BlockSpec((None, tr, 512), lambda b, i: (b, i, COL_KV // 512 + 2)),
                  pl.BlockSpec((None, tr, 128), lambda b, i: (b, i, COL_TAIL // 128)),
                  tab, tab, tab],
        out_specs=[pl.BlockSpec((None, tr, 1024), lambda b, i: (b, i, 0)),
                   pl.BlockSpec((None, tr, 1024), lambda b, i: (b, i, 0)),
                   pl.BlockSpec((None, 24, tr, 64), lambda b, i: (b, 0, i, 0)),
                   pl.BlockSpec((None, 4, tr, 12), lambda b, i: (b, 0, i, 0)),
                   pl.BlockSpec((None, tr, 512), lambda b, i: (b, 0, 0))],
        out_shape=[jax.ShapeDtypeStruct((bx, tx, 1024), BF16),
                   jax.ShapeDtypeStruct((bx, tx, 1024), F32),
                   jax.ShapeDtypeStruct((bx, 24, tx, 64), BF16),
                   jax.ShapeDtypeStruct((bx, 4, tx, 12), F32),
                   jax.ShapeDtypeStruct((bx, tr, 512), F32)],
        compiler_params=_params(("arbitrary", "arbitrary")),
        name="rope_prep",
    )(proj3, proj3, proj3, proj3, proj3, ctab, s1tab, s2tab)


def _compress_kernel(*refs, nrefs, npages, head_major):
    if nrefs > 1:
        refs = refs[1:]
    page_refs = refs[:nrefs]
    wbd_ref, w1f_ref, pe_ref, w2bd_ref, kc_ref, vc_ref, shift = refs[nrefs:]
    nsub = PAGE_SIZE // CMP_STRIDE
    nj = npages * nsub
    shift[nj:nj + 8, :] = jnp.zeros((8, 512), F32)
    ri = lax.broadcasted_iota(jnp.int32, (PAGE_SIZE, PAGE_SIZE), 0)
    ci = lax.broadcasted_iota(jnp.int32, (PAGE_SIZE, PAGE_SIZE), 1)
    perm = (ci == (ri & (nsub - 1)) * CMP_STRIDE + lax.shift_right_logical(ri, 3)).astype(BF16)
    if nrefs == 1:
        pages = [page_refs[0][p * PAGE_SIZE:(p + 1) * PAGE_SIZE, :] for p in range(npages)]
    else:
        pages = [pr[...] for pr in page_refs]
    grouped = [_dot(perm, pg.astype(BF16)) for pg in pages]
    for st, o_ref in ((0, kc_ref), (1, vc_ref)):
        acc = None
        for s in range(CMP_STRIDE):
            parts = [y[s * nsub:(s + 1) * nsub, st * 256:(st + 1) * 256] for y in grouped]
            xs = parts[0] if npages == 1 else jnp.concatenate(parts, axis=0)
            t = _dot(xs.astype(BF16), wbd_ref[st, s])
            acc = t if acc is None else acc + t
        shift[0:nj, :] = acc[:, 512:1024]
        pe8 = jnp.broadcast_to(pe_ref[st], (8, CMP_LEN * ATT_HEAD_DIM)).astype(BF16)
        pe_t = _dot(pe8, w1f_ref[st])[0:1, :]
        hid = acc[:, 0:512] + shift[pl.ds(1, nj), :] + jnp.concatenate([pe_t] * 4, axis=1)
        out = _dot(_silu(hid).astype(BF16), w2bd_ref[st]).astype(BF16)
        if head_major:
            for h in range(N_KV_HEADS):
                o_ref[h] = out[:, h * 64:(h + 1) * 64]
        else:
            o_ref[...] = out


def _compress_prompt(kv_rows, wbd, w1f, pe, w2bd):
    bx, tx, _ = kv_rows.shape
    nj = tx // CMP_STRIDE
    c4 = lambda shape: pl.BlockSpec(shape, lambda b: (0,) * len(shape))
    return pl.pallas_call(
        functools.partial(_compress_kernel, nrefs=1, npages=tx // PAGE_SIZE, head_major=True),
        grid=(bx,),
        in_specs=[pl.BlockSpec((None, tx, 512), lambda b: (b, 0, 0)),
                  c4(wbd.shape), c4(w1f.shape), c4(pe.shape), c4(w2bd.shape)],
        out_specs=[pl.BlockSpec((None, 4, nj, 64), lambda b: (b, 0, 0, 0))] * 2,
        out_shape=[jax.ShapeDtypeStruct((bx, 4, nj, 64), BF16)] * 2,
        scratch_shapes=[pltpu.VMEM((nj + 8, 512), F32)],
        compiler_params=_params(("arbitrary",)),
        name="compress_prompt",
    )(kv_rows, wbd, w1f, pe, w2bd)


def _compress_sample(page_table, cache3, wbd, w1f, pe, w2bd):
    nb, npages = page_table.shape
    nsub = PAGE_SIZE // CMP_STRIDE
    nj = npages * nsub
    c4 = lambda shape: pl.BlockSpec(shape, lambda b, pt: (0,) * len(shape))
    page_specs = [pl.BlockSpec((None, PAGE_SIZE, 512), functools.partial(lambda b, pt, p: (pt[b, p], 0, 0), p=p))
                  for p in range(npages)]
    grid_spec = pltpu.PrefetchScalarGridSpec(
        num_scalar_prefetch=1,
        grid=(nb,),
        in_specs=page_specs + [c4(wbd.shape), c4(w1f.shape), c4(pe.shape), c4(w2bd.shape)],
        out_specs=[pl.BlockSpec((None, nj, 256), lambda b, pt: (b, 0, 0))] * 2,
        scratch_shapes=[pltpu.VMEM((nj + 8, 512), F32)],
    )
    return pl.pallas_call(
        functools.partial(_compress_kernel, nrefs=npages, npages=npages, head_major=False),
        grid_spec=grid_spec,
        out_shape=[jax.ShapeDtypeStruct((nb, nj, 256), BF16)] * 2,
        compiler_params=_params(("arbitrary",)),
        name="compress_sample",
    )(page_table, *([cache3] * npages), wbd, w1f, pe, w2bd)


def _two_pass_attention(q, k_ref, v_ref, lo, hi, bias_fn, s_scr):
    def pass_a(kt, mx):
        k = k_ref[pl.ds(pl.multiple_of(kt * 128, 128), 128), :]
        s = _dot_nt(q, k) + bias_fn(kt)
        s_scr[kt] = s
        return jnp.maximum(mx, s)

    mx = lax.fori_loop(lo, hi, pass_a, jnp.full((128, 128), -jnp.inf, F32))
    m = jnp.broadcast_to(jnp.max(mx, axis=-1, keepdims=True), (128, 128))

    def pass_b(kt, carry):
        lsum, acc = carry
        v = v_ref[pl.ds(pl.multiple_of(kt * 128, 128), 128), :]
        p = jnp.exp(s_scr[kt] - m)
        return lsum + p, acc + _dot(p.astype(BF16), v)

    lsum, acc = lax.fori_loop(lo, hi, pass_b, (jnp.zeros((128, 128), F32), jnp.zeros((128, 64), F32)))
    return acc / jnp.maximum(jnp.sum(lsum, axis=-1, keepdims=True), TINY)


def _nsa_prompt_kernel(q_ref, g_ref, kc_ref, vc_ref, ks_ref, vs_ref, kw_ref, vw_ref, ovt_ref, eall_ref,
                       o_ref, s_scr, b_scr, *, ntiles):
    qb = pl.program_id(2)
    q0 = qb * Q_BLOCK
    q4 = q_ref[...]
    qs = [q4[:, r * 64:(r + 1) * 64] for r in range(Q_PER_KV)]
    ncmp = kc_ref.shape[0]

    trow = q0 + lax.broadcasted_iota(jnp.int32, (128, ncmp), 0)
    jl = lax.broadcasted_iota(jnp.int32, (128, ncmp), 1)
    mask_c = (jl * CMP_STRIDE + (CMP_LEN - 1)) <= trow
    kc = kc_ref[...]
    vc = vc_ref[...]
    psum = jnp.zeros((128, ncmp), F32)
    o_c = []
    for r in range(Q_PER_KV):
        s = jnp.where(mask_c, _dot_nt(qs[r], kc), NEG)
        e = jnp.where(mask_c, jnp.exp(s - jnp.max(s, axis=-1, keepdims=True)), 0.0)
        p = e / jnp.maximum(jnp.sum(e, axis=-1, keepdims=True), TINY)
        psum = psum + p
        o_c.append(_dot(p.astype(BF16), vc))
    imp_t = _dot_nt_hi(ovt_ref[...], psum)
    nblk = imp_t.shape[0]
    tq = q0 + lax.broadcasted_iota(jnp.int32, (nblk, 128), 1)
    blk = lax.broadcasted_iota(jnp.int32, (nblk, 128), 0)
    cur = lax.shift_right_logical(tq, 6)
    valid = blk * SEL_LEN <= tq
    forced = (blk == 0) | (blk == cur) | (blk == cur - 1)
    imp = jnp.where(valid, jnp.where(forced, BIG, imp_t), -BIG)
    rank = jnp.zeros((nblk, 128), F32)
    for j in range(nblk):
        rj = imp[j:j + 1, :]
        beats = (rj > imp) | ((rj == imp) & (blk > j))
        rank = rank + jnp.where(beats, 1.0, 0.0)
    sel_t = jnp.where(rank < float(SEL_TOPK), 1.0, 0.0)
    sel_pad = jnp.concatenate([sel_t, jnp.zeros((128 - nblk, 128), F32)], axis=0) if nblk < 128 else sel_t
    sel = sel_pad.T[:, 0:nblk].astype(BF16)
    bias_all = (_dot(sel, eall_ref[...]) - 1.0) * BIG
    for t in range(ntiles):
        b_scr[t] = bias_all[:, t * 128:(t + 1) * 128]

    row = lax.broadcasted_iota(jnp.int32, (128, 128), 0)
    col = lax.broadcasted_iota(jnp.int32, (128, 128), 1)
    diff = row - col

    def sel_bias(kt):
        return b_scr[kt] + jnp.where((kt < qb) | (diff >= 0), 0.0, NEG)

    def win_bias(kt):
        d = (qb - kt) * 128 + diff
        return jnp.where((d >= 0) & (d < WINDOW), 0.0, NEG)

    wlo = jnp.maximum(qb - WINDOW // 128, 0)
    g = g_ref[...]
    outs = []
    for r in range(Q_PER_KV):
        o_s = _two_pass_attention(qs[r], ks_ref, vs_ref, 0, qb + 1, sel_bias, s_scr)
        o_w = _two_pass_attention(qs[r], kw_ref, vw_ref, wlo, qb + 1, win_bias, s_scr)
        outs.append(g[:, 3 * r:3 * r + 1] * o_c[r] + g[:, 3 * r + 1:3 * r + 2] * o_s + g[:, 3 * r + 2:3 * r + 3] * o_w)
    o_ref[...] = jnp.concatenate(outs, axis=1).astype(BF16)


def _nsa_prompt(q_r, gates, kc, vc, khm, ovt, eall):
    bx, tx, _ = q_r.shape
    ntiles = tx // 128
    ncmp = kc.shape[2]
    kv_spec = lambda s: pl.BlockSpec((None, None, tx, 64), lambda b, h, i: (b, s * 4 + h, 0, 0))
    return pl.pallas_call(
        functools.partial(_nsa_prompt_kernel, ntiles=ntiles),
        grid=(bx, N_KV_HEADS, ntiles),
        in_specs=[pl.BlockSpec((None, 128, 256), lambda b, h, i: (b, i, h)),
                  pl.BlockSpec((None, None, 128, 12), lambda b, h, i: (b, h, i, 0)),
                  pl.BlockSpec((None, None, ncmp, 64), lambda b, h, i: (b, h, 0, 0)),
                  pl.BlockSpec((None, None, ncmp, 64), lambda b, h, i: (b, h, 0, 0)),
                  kv_spec(2), kv_spec(3), kv_spec(4), kv_spec(5),
                  pl.BlockSpec(ovt.shape, lambda b, h, i: (0, 0)),
                  pl.BlockSpec(eall.shape, lambda b, h, i: (0, 0))],
        out_specs=pl.BlockSpec((None, 128, 256), lambda b, h, i: (b, i, h)),
        out_shape=jax.ShapeDtypeStruct((bx, tx, ATT_WIDTH), BF16),
        scratch_shapes=[pltpu.VMEM((ntiles, 128, 128), F32), pltpu.VMEM((ntiles, 128, 128), F32)],
        compiler_params=_params(("arbitrary", "arbitrary", "arbitrary")),
        name="nsa_prompt",
    )(q_r, gates, kc, vc, khm, khm, khm, khm, ovt, eall)


def _fold_heads(o256, hmask):
    o = o256 * hmask
    return o[:, 0:64] + o[:, 64:128] + o[:, 128:192] + o[:, 192:256]


def _nsa_sample_kernel(*refs, npages):
    refs = refs[1:]
    pages = refs[:npages]
    (q_ref, g_ref, knew_ref, wnew_ref, kc_ref, vc_ref, win_ref, ov_ref, e_ref,
     o_ref, wout_ref, wshift) = refs[npages:]
    past = npages * PAGE_SIZE
    ncmp = kc_ref.shape[0]
    ntok = (past - CMP_LEN) // CMP_STRIDE + 1
    nsel = past // SEL_LEN + 1

    q16 = q_ref[...].astype(F32)
    rowh = lax.broadcasted_iota(jnp.int32, (16, 256), 0)
    laneh = lax.broadcasted_iota(jnp.int32, (16, 256), 1)
    hmask = (lax.shift_right_logical(rowh, 2) == lax.shift_right_logical(laneh, 6)).astype(F32)
    qbd_f = jnp.concatenate([q16] * 4, axis=1) * hmask
    qbd = qbd_f.astype(BF16)

    jl = lax.broadcasted_iota(jnp.int32, (16, ncmp), 1)
    mask_c = jl < ntok
    s = jnp.where(mask_c, _dot_nt(qbd, kc_ref[...]), NEG)
    e = jnp.where(mask_c, jnp.exp(s - jnp.max(s, axis=-1, keepdims=True)), 0.0)
    p_c = e / jnp.maximum(jnp.sum(e, axis=-1, keepdims=True), TINY)
    o_c = _fold_heads(_dot(p_c.astype(BF16), vc_ref[...]), hmask)

    gm = (lax.shift_right_logical(lax.broadcasted_iota(jnp.int32, (8, 16), 1), 2)
          == lax.broadcasted_iota(jnp.int32, (8, 16), 0)).astype(F32)
    imp = _dot_hi(_dot_hi(gm, p_c), ov_ref[...])
    blk = lax.broadcasted_iota(jnp.int32, (8, 128), 1)
    exists = blk < nsel
    forced = (blk == 0) | (blk == nsel - 1) | (blk == nsel - 2)
    imp = jnp.where(exists, jnp.where(forced, BIG, imp), -BIG)
    rank = jnp.zeros((8, 128), F32)
    for j in range(nsel):
        cj = imp[:, j:j + 1]
        beats = (cj > imp) | ((cj == imp) & (blk > j))
        rank = rank + jnp.where(beats, 1.0, 0.0)
    sel8 = jnp.where((rank < float(min(SEL_TOPK, nsel))) & exists, 1.0, 0.0)
    gm_t = (lax.shift_right_logical(lax.broadcasted_iota(jnp.int32, (16, 8), 0), 2)
            == lax.broadcasted_iota(jnp.int32, (16, 8), 1)).astype(F32)
    sel16 = _dot(gm_t, sel8)
    bias_past = (_dot(sel16.astype(BF16), e_ref[...]) - 1.0) * BIG

    knew = knew_ref[...]
    s_new = jnp.sum(qbd_f * knew[:, 512:768], axis=-1, keepdims=True)
    s_past = jnp.concatenate(
        [_dot_nt(qbd, pg[:, 0:256].astype(BF16)) for pg in pages], axis=1) + bias_past
    m = jnp.maximum(jnp.max(s_past, axis=-1, keepdims=True), s_new)
    e_past = jnp.exp(s_past - m)
    e_new = jnp.exp(s_new - m)
    acc = e_new * knew[:, 768:1024]
    for p, pg in enumerate(pages):
        acc = acc + _dot(e_past[:, p * 128:(p + 1) * 128].astype(BF16), pg[:, 256:512].astype(BF16))
    lsum = jnp.sum(e_past, axis=-1, keepdims=True) + e_new
    o_s = _fold_heads(acc / jnp.maximum(lsum, TINY), hmask)

    win = win_ref[...]
    wnew = wnew_ref[...]
    wl = lax.broadcasted_iota(jnp.int32, (16, WINDOW), 1)
    s_w = jnp.where(wl >= 1, _dot_nt(qbd, win[:, 0:256].astype(BF16)), NEG)
    s_wn = jnp.sum(qbd_f * wnew[:, 0:256], axis=-1, keepdims=True)
    mw = jnp.maximum(jnp.max(s_w, axis=-1, keepdims=True), s_wn)
    e_w = jnp.where(wl >= 1, jnp.exp(s_w - mw), 0.0)
    e_wn = jnp.exp(s_wn - mw)
    acc_w = _dot(e_w.astype(BF16), win[:, 256:512].astype(BF16)) + e_wn * wnew[:, 256:512]
    o_w = _fold_heads(acc_w / jnp.maximum(jnp.sum(e_w, axis=-1, keepdims=True) + e_wn, TINY), hmask)

    g = _sig(g_ref[...])
    o_ref[...] = (g[:, 0:1] * o_c + g[:, 1:2] * o_s + g[:, 2:3] * o_w).astype(BF16)

    wshift[0:WINDOW, :] = win
    wshift[WINDOW:WINDOW + 8, :] = jnp.broadcast_to(wnew, (8, 512))
    wout_ref[...] = wshift[pl.ds(1, WINDOW), :]


def _nsa_sample(page_table, cache3, q16, g16, knew, wnew, kc, vc, win, ov, emat):
    nb, npages = page_table.shape
    ncmp = kc.shape[1]
    c2 = lambda shape: pl.BlockSpec(shape, lambda b, pt: (0,) * len(shape))
    page_specs = [pl.BlockSpec((None, PAGE_SIZE, 512), functools.partial(lambda b, pt, p: (pt[b, p], 0, 1), p=p))
                  for p in range(npages)]
    grid_spec = pltpu.PrefetchScalarGridSpec(
        num_scalar_prefetch=1,
        grid=(nb,),
        in_specs=page_specs + [
            pl.BlockSpec((None, 16, 64), lambda b, pt: (b, 0, 0)),
            pl.BlockSpec((None, 16, 3), lambda b, pt: (b, 0, 0)),
            pl.BlockSpec((None, 1, 1024), lambda b, pt: (b, 0, 0)),
            pl.BlockSpec((None, 1, 512), lambda b, pt: (b, 0, 0)),
            pl.BlockSpec((None, ncmp, 256), lambda b, pt: (b, 0, 0)),
            pl.BlockSpec((None, ncmp, 256), lambda b, pt: (b, 0, 0)),
            pl.BlockSpec((None, WINDOW, 512), lambda b, pt: (b, 0, 0)),
            c2(ov.shape), c2(emat.shape)],
        out_specs=[pl.BlockSpec((None, 16, 64), lambda b, pt: (b, 0, 0)),
                   pl.BlockSpec((None, WINDOW, 512), lambda b, pt: (b, 0, 0))],
        scratch_shapes=[pltpu.VMEM((WINDOW + 8, 512), F32)],
    )
    return pl.pallas_call(
        functools.partial(_nsa_sample_kernel, npages=npages),
        grid_spec=grid_spec,
        out_shape=[jax.ShapeDtypeStruct((nb, 16, 64), BF16),
                   jax.ShapeDtypeStruct((nb, WINDOW, 512), F32)],
        compiler_params=_params(("arbitrary",)),
        name="nsa_sample",
    )(page_table, *([cache3] * npages), q16, g16, knew, wnew, kc, vc, win, ov, emat)


def _merge_kernel(x_ref, yssm_ref, yatt_ref, gs_ref, ga_ref, g1_ref, sh2_ref, sc2_ref, nw_ref,
                  wss_ref, wat_ref, wo_ref, x1_ref, h2_ref):
    ms = _dot(yssm_ref[...], wss_ref[...])
    ma = _dot(yatt_ref[...], wat_ref[...])
    merged = _sig(gs_ref[...]) * ms + _sig(ga_ref[...]) * ma
    x1 = x_ref[...] + g1_ref[...] * _dot(merged.astype(BF16), wo_ref[...])
    x1_ref[...] = x1
    y = x1 * lax.rsqrt(jnp.mean(x1 * x1, axis=-1, keepdims=True) + EPS)
    h2_ref[...] = (y * nw_ref[...] * (1.0 + sc2_ref[...]) + sh2_ref[...]).astype(BF16)


def _merge_out(x3, yssm, yatt, proj3, mod3, nw_row, wss, wat, wo, tm):
    bx, tx, _ = x3.shape
    r = mod3.shape[1]
    mod_b = (lambda b: b) if mod3.shape[0] == bx else (lambda b: 0)
    row = lambda w, cb: pl.BlockSpec((None, tm, w), lambda b, i: (b, i, cb))
    modc = lambda cb: pl.BlockSpec((None, r, D_MODEL), lambda b, i: (mod_b(b), 0, cb))
    const = lambda shape: pl.BlockSpec(shape, lambda b, i: (0, 0))
    return pl.pallas_call(
        _merge_kernel,
        grid=(bx, tx // tm),
        in_specs=[row(D_MODEL, 0), row(D_INNER, 0), row(ATT_WIDTH, 0),
                  row(1024, COL_GM // 1024), row(1024, COL_GM // 1024 + 1),
                  modc(2), modc(3), modc(4), const((1, D_MODEL)),
                  const(wss.shape), const(wat.shape), const(wo.shape)],
        out_specs=[row(D_MODEL, 0), row(D_MODEL, 0)],
        out_shape=[jax.ShapeDtypeStruct((bx, tx, D_MODEL), F32), jax.ShapeDtypeStruct((bx, tx, D_MODEL), BF16)],
        compiler_params=_params(("arbitrary", "arbitrary")),
        name="merge_out",
    )(x3, yssm, yatt, proj3, proj3, mod3, mod3, mod3, nw_row, wss, wat, wo)


FFN_TN = 256
FFN_NT = D_FF // FFN_TN


def _ffn_up_prompt_kernel(h2_ref, wa_ref, wb_ref, cwa_ref, cwb_ref, cba_ref, cbb_ref,
                          act_ref, fa_ref, fb_ref, work, tails, *, tm, ntiles):
    i = pl.program_id(1)
    j = pl.program_id(2)
    h2 = h2_ref[...]
    u = jnp.concatenate([_dot(h2, wa_ref[...]), _dot(h2, wb_ref[...])], axis=1)
    work[8:8 + tm, :] = u

    @pl.when(i == 0)
    def _():
        work[0:8, :] = jnp.zeros((8, 2 * FFN_TN), F32)

    @pl.when(i > 0)
    def _():
        work[0:8, :] = tails[j]

    cw = jnp.concatenate([cwa_ref[...], cwb_ref[...]], axis=1)
    cb = jnp.concatenate([cba_ref[...], cbb_ref[...]], axis=1)
    conv = work[6:6 + tm, :] * cw[0:1, :] + work[7:7 + tm, :] * cw[1:2, :] + work[8:8 + tm, :] * cw[2:3, :] + cb
    act_ref[...] = (_silu(conv[:, 0:FFN_TN]) * conv[:, FFN_TN:2 * FFN_TN]).astype(BF16)
    tails[j] = work[tm:tm + 8, :]
    fa_ref[...] = work[tm + 6:tm + 8, 0:FFN_TN]
    fb_ref[...] = work[tm + 6:tm + 8, FFN_TN:2 * FFN_TN]


def _ffn_up_prompt(h2, wup, cw, cb_row, tm):
    bx, tx, _ = h2.shape
    ntiles = tx // tm
    return pl.pallas_call(
        functools.partial(_ffn_up_prompt_kernel, tm=tm, ntiles=ntiles),
        grid=(bx, ntiles, FFN_NT),
        in_specs=[pl.BlockSpec((None, tm, D_MODEL), lambda b, i, j: (b, i, 0)),
                  pl.BlockSpec((D_MODEL, FFN_TN), lambda b, i, j: (0, j)),
                  pl.BlockSpec((D_MODEL, FFN_TN), lambda b, i, j: (0, j + FFN_NT)),
                  pl.BlockSpec((FFN_CONV, FFN_TN), lambda b, i, j: (0, j)),
                  pl.BlockSpec((FFN_CONV, FFN_TN), lambda b, i, j: (0, j + FFN_NT)),
                  pl.BlockSpec((1, FFN_TN), lambda b, i, j: (0, j)),
                  pl.BlockSpec((1, FFN_TN), lambda b, i, j: (0, j + FFN_NT))],
        out_specs=[pl.BlockSpec((None, tm, FFN_TN), lambda b, i, j: (b, i, j)),
                   pl.BlockSpec((None, None, FFN_CONV - 1, FFN_TN), lambda b, i, j: (b, i, 0, j)),
                   pl.BlockSpec((None, None, FFN_CONV - 1, FFN_TN), lambda b, i, j: (b, i, 0, j))],
        out_shape=[jax.ShapeDtypeStruct((bx, tx, D_FF), BF16),
                   jax.ShapeDtypeStruct((bx, ntiles, FFN_CONV - 1, D_FF), F32),
                   jax.ShapeDtypeStruct((bx, ntiles, FFN_CONV - 1, D_FF), F32)],
        scratch_shapes=[pltpu.VMEM((tm + 8, 2 * FFN_TN), F32), pltpu.VMEM((FFN_NT, 8, 2 * FFN_TN), F32)],
        compiler_params=_params(("arbitrary", "arbitrary", "arbitrary")),
        name="ffn_up_prompt",
    )(h2, wup, wup, cw, cw, cb_row, cb_row)


def _ffn_up_sample_kernel(h2_ref, wa_ref, wb_ref, h0a_ref, h0b_ref, h1a_ref, h1b_ref, cwa_ref, cwb_ref,
                          cba_ref, cbb_ref, act_ref, ua_ref, ub_ref):
    h2 = h2_ref[...]
    ua = _dot(h2, wa_ref[...])
    ub = _dot(h2, wb_ref[...])
    cwa = cwa_ref[...]
    cwb = cwb_ref[...]
    ca = h0a_ref[...] * cwa[0:1, :] + h1a_ref[...] * cwa[1:2, :] + ua * cwa[2:3, :] + cba_ref[...]
    cb = h0b_ref[...] * cwb[0:1, :] + h1b_ref[...] * cwb[1:2, :] + ub * cwb[2:3, :] + cbb_ref[...]
    act_ref[...] = (_silu(ca) * cb).astype(BF16)
    ua_ref[...] = ua
    ub_ref[...] = ub


def _ffn_up_sample(h2, wup, hist0, hist1, cw, cb_row):
    nb = h2.shape[0]
    col = lambda rows, off: pl.BlockSpec((rows, FFN_TN), lambda j: (0, j + off))
    return pl.pallas_call(
        _ffn_up_sample_kernel,
        grid=(FFN_NT,),
        in_specs=[pl.BlockSpec((nb, D_MODEL), lambda j: (0, 0)),
                  col(D_MODEL, 0), col(D_MODEL, FFN_NT), col(nb, 0), col(nb, FFN_NT), col(nb, 0), col(nb, FFN_NT),
                  col(FFN_CONV, 0), col(FFN_CONV, FFN_NT), col(1, 0), col(1, FFN_NT)],
        out_specs=[col(nb, 0), col(nb, 0), col(nb, 0)],
        out_shape=[jax.ShapeDtypeStruct((nb, D_FF), BF16), jax.ShapeDtypeStruct((nb, D_FF), F32),
                   jax.ShapeDtypeStruct((nb, D_FF), F32)],
        compiler_params=_params(("arbitrary",)),
        name="ffn_up_sample",
    )(h2, wup, wup, hist0, hist0, hist1, hist1, cw, cw, cb_row, cb_row)


def _ffn_down_kernel(act_ref, x1_ref, g2_ref, nw_ref, w_ref, y_ref):
    x2 = x1_ref[...] + g2_ref[...] * _dot(act_ref[...], w_ref[...])
    y_ref[...] = x2 * lax.rsqrt(jnp.mean(x2 * x2, axis=-1, keepdims=True) + EPS) * nw_ref[...]


def _ffn_down(act, x1, mod3, nw_row, wdown, tm):
    bx, tx, _ = x1.shape
    r = mod3.shape[1]
    mod_b = (lambda b: b) if mod3.shape[0] == bx else (lambda b: 0)
    return pl.pallas_call(
        _ffn_down_kernel,
        grid=(bx, tx // tm),
        in_specs=[pl.BlockSpec((None, tm, D_FF), lambda b, i: (b, i, 0)),
                  pl.BlockSpec((None, tm, D_MODEL), lambda b, i: (b, i, 0)),
                  pl.BlockSpec((None, r, D_MODEL), lambda b, i: (mod_b(b), 0, 5)),
                  pl.BlockSpec((1, D_MODEL), lambda b, i: (0, 0)),
                  pl.BlockSpec((D_FF, D_MODEL), lambda b, i: (0, 0))],
        out_specs=pl.BlockSpec((None, tm, D_MODEL), lambda b, i: (b, i, 0)),
        out_shape=jax.ShapeDtypeStruct((bx, tx, D_MODEL), F32),
        compiler_params=_params(("arbitrary", "arbitrary")),
        name="ffn_down",
    )(act, x1, mod3, nw_row, wdown)


def _rope_tables(pos):
    half = ROPE_DIM // 2
    inv_freq = ROPE_THETA ** (-jnp.arange(half, dtype=F32) / half)
    ang = pos.astype(F32)[:, None] * inv_freq[None, :]
    cos, sin = jnp.cos(ang), jnp.sin(ang)
    n = pos.shape[0]
    ones = jnp.ones((n, ATT_HEAD_DIM - ROPE_DIM), F32)
    zeros8 = jnp.zeros((n, half), F32)
    zeros48 = jnp.zeros((n, ATT_HEAD_DIM - ROPE_DIM), F32)
    c = jnp.concatenate([cos, cos, ones], axis=1)
    s1 = jnp.concatenate([-sin, zeros8, zeros48], axis=1)
    s2 = jnp.concatenate([zeros8, sin, zeros48], axis=1)
    return tuple(jnp.concatenate([t, t], axis=1) for t in (c, s1, s2))


def _overlap(nc, ns):
    cst = np.arange(nc)[:, None] * CMP_STRIDE
    sst = np.arange(ns)[None, :] * SEL_LEN
    ov = np.clip(np.minimum(cst + CMP_LEN, sst + SEL_LEN) - np.maximum(cst, sst), 0, None)
    return (ov / CMP_STRIDE).astype(np.float32)


def _cmp_weights(pe, w1, w2):
    eye = jnp.eye(N_KV_HEADS, dtype=F32)
    wab = jnp.stack([w1[:CMP_STRIDE], w1[CMP_STRIDE:]], axis=1)
    wbd = jnp.einsum("hk,sade->shdake", eye, wab).reshape(CMP_STRIDE, 256, 1024)
    w2bd = jnp.einsum("hk,ed->hekd", eye, w2).reshape(4 * CMP_HIDDEN, 256)
    return wbd.astype(BF16), w1.reshape(CMP_LEN * ATT_HEAD_DIM, CMP_HIDDEN).astype(BF16), pe.reshape(1, -1), w2bd.astype(BF16)


def kernel(x_prompt, x_sample, c_prompt, c_sample, cache_nsa_kv, page_table, cache_win_kv, state_ssm, state_ssm_conv, state_ffn_conv, ada_w, ada_b, norm1_w, norm2_w, final_norm_w, w_in, ssm_conv_w, ssm_conv_b, ssm_dt_bias, ssm_A_log, ssm_D, ssm_norm_w, cmp_pe_k, cmp_w1_k, cmp_w2_k, cmp_pe_v, cmp_w1_v, cmp_w2_v, w_ssm_out, w_att_out, w_out, ffn_w_up, ffn_conv_w, ffn_conv_b, ffn_w_down):
    bp, tp, _ = x_prompt.shape
    nb = x_sample.shape[0]
    npages = page_table.shape[1]
    past = npages * PAGE_SIZE

    o_z, o_xbc, o_dt = 0, D_INNER, D_INNER + CONV_DIM
    o_q = o_dt + N_SSM_HEADS
    o_kv = o_q + ATT_WIDTH
    o_ag = o_kv + 6 * 256
    o_mg = o_ag + 3 * N_ATT_HEADS
    w_r = jnp.concatenate([
        w_in[:, o_z:o_z + 2048], w_in[:, o_xbc:o_xbc + 2048], w_in[:, o_q:o_q + 1024], w_in[:, o_mg:o_mg + 2048],
        w_in[:, o_xbc + 2048:o_xbc + 3072], w_in[:, o_kv:o_kv + 1536], w_in[:, o_dt:o_dt + 32],
        w_in[:, o_ag:o_ag + 48], jnp.zeros((D_MODEL, 48), F32)], axis=1).astype(BF16)
    ada_w_bf = ada_w.astype(BF16)
    wss = w_ssm_out.astype(BF16)
    wat = w_att_out.astype(BF16)
    wo = w_out.astype(BF16)
    wup = ffn_w_up.astype(BF16)
    wdown = ffn_w_down.astype(BF16)
    row = lambda v: v.reshape(1, -1)
    pad128 = lambda v: jnp.pad(v, (0, 128 - v.shape[0])).reshape(1, 128)
    dvec = jnp.repeat(ssm_D, SSM_HEAD_DIM).reshape(1, D_INNER)
    cmpw_k = _cmp_weights(cmp_pe_k, cmp_w1_k, cmp_w2_k)
    cmpw_v = _cmp_weights(cmp_pe_v, cmp_w1_v, cmp_w2_v)
    wbd, w1f, pe2, w2bd = (jnp.stack([a, b]) for a, b in zip(cmpw_k, cmpw_v))

    npad = (-(bp + nb)) % 8
    c_all = jnp.concatenate([c_prompt, c_sample, jnp.zeros((npad, D_MODEL), F32)], axis=0)
    mod = _ada_mod(c_all, ada_w_bf, row(ada_b))
    mod_p = mod[:bp].reshape(bp, 1, 6 * D_MODEL)
    mod_s = mod[bp:bp + nb].reshape(1, nb, 6 * D_MODEL)

    proj_p = _inproj(x_prompt, mod_p, row(norm1_w), w_r, 1024)
    yssm_p, conv_p, hlast_p = _ssd_prompt(proj_p, ssm_conv_w, row(ssm_conv_b), pad128(ssm_dt_bias),
                                          pad128(ssm_A_log), dvec, row(ssm_norm_w))
    tabs_p = _rope_tables(jnp.arange(tp, dtype=jnp.int32))
    q_p, kvrows_p, khm_p, gates_p, win_p = _rope_prep(proj_p, *tabs_p, 512)
    kc_p, vc_p = _compress_prompt(kvrows_p, wbd, w1f, pe2, w2bd)
    nblk = -(-tp // SEL_LEN)
    ncmp = tp // CMP_STRIDE
    ovt = np.zeros((nblk, ncmp), np.float32)
    ovt[:, :ncmp - 1] = _overlap(ncmp - 1, nblk).T
    eall = (np.arange(tp)[None, :] // SEL_LEN == np.arange(nblk)[:, None]).astype(np.float32)
    yatt_p = _nsa_prompt(q_p, gates_p, kc_p, vc_p, khm_p, jnp.asarray(ovt), jnp.asarray(eall, dtype=BF16))
    x1_p, h2_p = _merge_out(x_prompt, yssm_p, yatt_p, proj_p, mod_p, row(norm2_w), wss, wat, wo, 512)
    act_p, fa_p, fb_p = _ffn_up_prompt(h2_p, wup, ffn_conv_w, row(ffn_conv_b), 512)
    y_p = _ffn_down(act_p, x1_p, mod_p, row(final_norm_w), wdown, 512)

    x_s3 = x_sample.reshape(1, nb, D_MODEL)
    proj_s = _inproj(x_s3, mod_s, row(norm1_w), w_r, nb)
    eh = (np.arange(D_INNER)[None, :] // SSM_HEAD_DIM == np.arange(128)[:, None]).astype(np.float32)
    xdt, dec_e, yd, bm_c, cm_c, xdt_t, dec_t = _ssm_pre(
        proj_s, state_ssm_conv[:, 0], state_ssm_conv[:, 1], state_ssm_conv[:, 2], ssm_conv_w, row(ssm_conv_b),
        pad128(ssm_dt_bias), pad128(ssm_A_log), dvec, jnp.asarray(eh))
    h_new, yoff = _ssm_state(state_ssm.reshape(nb, D_INNER, SSM_STATE), xdt_t, dec_t, bm_c, cm_c)
    yssm_s = _ssm_post(yd, yoff[:, 0, :], dec_e, proj_s, row(ssm_norm_w))
    tabs_s = _rope_tables(jnp.full((nb,), past, dtype=jnp.int32))
    q_s, kvrows_s, _, _, wnew_s = _rope_prep(proj_s, *tabs_s, nb)
    cache3 = cache_nsa_kv.reshape(cache_nsa_kv.shape[0], PAGE_SIZE, 1024)
    kc_s, vc_s = _compress_sample(page_table, cache3, wbd, w1f, pe2, w2bd)
    ncmp_s = past // CMP_STRIDE
    nsel_s = past // SEL_LEN + 1
    ov_s = np.zeros((ncmp_s, 128), np.float32)
    ov_s[:ncmp_s - 1, :nsel_s] = _overlap(ncmp_s - 1, nsel_s)
    e_s = (np.arange(past)[None, :] // SEL_LEN == np.arange(128)[:, None]).astype(np.float32)
    att_g_s = proj_s[0, :, COL_TAIL + 32:COL_TAIL + 80].reshape(nb, N_ATT_HEADS, 3)
    yatt_s16, win_s = _nsa_sample(
        page_table, cache3, q_s.reshape(nb, N_ATT_HEADS, ATT_HEAD_DIM), att_g_s, kvrows_s.reshape(nb, 1, 1024),
        wnew_s.reshape(nb, 1, 512), kc_s, vc_s, cache_win_kv.reshape(nb, WINDOW, 512), jnp.asarray(ov_s),
        jnp.asarray(e_s, dtype=BF16))
    x1_s, h2_s = _merge_out(x_s3, yssm_s.reshape(1, nb, D_INNER), yatt_s16.reshape(1, nb, ATT_WIDTH), proj_s, mod_s,
                            row(norm2_w), wss, wat, wo, nb)
    act_s, ua_s, ub_s = _ffn_up_sample(h2_s.reshape(nb, D_MODEL), wup, state_ffn_conv[:, 0], state_ffn_conv[:, 1],
                                       ffn_conv_w, row(ffn_conv_b))
    y_s = _ffn_down(act_s.reshape(1, nb, D_FF), x1_s, mod_s, row(final_norm_w), wdown, nb)

    xbc_s = jnp.concatenate([proj_s[0, :, COL_XS:COL_XS + 2048], proj_s[0, :, COL_BM:COL_BM + 1024]], axis=1)
    conv_s = jnp.stack([state_ssm_conv[:, 1], state_ssm_conv[:, 2], xbc_s], axis=1)
    ffn_s = jnp.stack([state_ffn_conv[:, 1], jnp.concatenate([ua_s, ub_s], axis=1)], axis=1)
    return (y_p,
            y_s.reshape(nb, 1, D_MODEL),
            kvrows_p.reshape(bp, tp, 4, N_KV_HEADS, ATT_HEAD_DIM),
            kvrows_s.reshape(nb, 1, 4, N_KV_HEADS, ATT_HEAD_DIM),
            win_p.reshape(bp, WINDOW, 2, N_KV_HEADS, ATT_HEAD_DIM),
            win_s.reshape(nb, WINDOW, 2, N_KV_HEADS, ATT_HEAD_DIM),
            hlast_p.reshape(bp, N_SSM_HEADS, SSM_HEAD_DIM, SSM_STATE),
            h_new.reshape(nb, N_SSM_HEADS, SSM_HEAD_DIM, SSM_STATE),
            conv_p,
            conv_s,
            jnp.concatenate([fa_p[:, -1], fb_p[:, -1]], axis=2),
            ffn_s)
```

```python
import functools

import numpy as np
import jax
import jax.numpy as jnp
from jax import lax
from jax.experimental import pallas as pl
from jax.experimental.pallas import tpu as pltpu

F32 = jnp.float32
BF16 = jnp.bfloat16
HIGHEST = lax.Precision.HIGHEST

D_MODEL = 1024
D_INNER = 2048
N_SSM_HEADS = 32
SSM_HEAD_DIM = 64
SSM_STATE = 128
SSM_GROUPS = 4
SSM_CONV = 4
CONV_DIM = 3072
SSD_CHUNK = 128
N_ATT_HEADS = 16
ATT_HEAD_DIM = 64
N_KV_HEADS = 4
Q_PER_KV = 4
ATT_WIDTH = 1024
ATT_SCALE = ATT_HEAD_DIM ** -0.5
ROPE_DIM = 16
ROPE_THETA = 500000.0
CMP_LEN = 32
CMP_STRIDE = 16
CMP_HIDDEN = 128
SEL_LEN = 64
SEL_TOPK = 16
WINDOW = 512
Q_BLOCK = 128
PAGE_SIZE = 128
D_FF = 2816
FFN_CONV = 3
EPS = 1e-6
NEG = -1e30
BIG = 1e30
TINY = 1e-30

COL_Z = 0
COL_XS = 2048
COL_Q = 4096
COL_GM = 5120
COL_BM = 7168
COL_CM = 7680
COL_KV = 8192
COL_TAIL = 9728
N_PROJ = 9856
PROJ_TN = 896
VMEM_LIMIT = 56 * 1024 * 1024


def _sig(x):
    return 1.0 / (1.0 + jnp.exp(-x))


def _silu(x):
    return x * _sig(x)


def _softplus(x):
    return jnp.maximum(x, 0.0) + jnp.log1p(jnp.exp(-jnp.abs(x)))


def _dot(a, b):
    return jnp.dot(a, b, preferred_element_type=F32)


def _dot_hi(a, b):
    return jnp.dot(a, b, preferred_element_type=F32, precision=HIGHEST)


def _dot_nt(a, b):
    return lax.dot_general(a, b, (((1,), (1,)), ((), ())), preferred_element_type=F32)


def _dot_nt_hi(a, b):
    return lax.dot_general(a, b, (((1,), (1,)), ((), ())), preferred_element_type=F32, precision=HIGHEST)


def _params(sem):
    return pltpu.CompilerParams(dimension_semantics=sem, vmem_limit_bytes=VMEM_LIMIT)


def _ada_kernel(c_ref, w_ref, b_ref, o_ref):
    c = c_ref[...]
    o_ref[...] = _dot(_silu(c).astype(BF16), w_ref[...]) + b_ref[...]


def _ada_mod(c_all, w_bf, b_row):
    m = c_all.shape[0]
    tn = 512
    return pl.pallas_call(
        _ada_kernel,
        grid=(w_bf.shape[1] // tn,),
        in_specs=[pl.BlockSpec((m, D_MODEL), lambda j: (0, 0)),
                  pl.BlockSpec((D_MODEL, tn), lambda j: (0, j)),
                  pl.BlockSpec((1, tn), lambda j: (0, j))],
        out_specs=pl.BlockSpec((m, tn), lambda j: (0, j)),
        out_shape=jax.ShapeDtypeStruct((m, w_bf.shape[1]), F32),
        compiler_params=_params(("arbitrary",)),
        name="ada_mod",
    )(c_all, w_bf, b_row)


def _inproj_kernel(x_ref, sh_ref, sc_ref, nw_ref, w_ref, o_ref, h_scr):
    @pl.when(pl.program_id(2) == 0)
    def _():
        x = x_ref[...]
        y = x * lax.rsqrt(jnp.mean(x * x, axis=-1, keepdims=True) + EPS)
        h = y * nw_ref[...] * (1.0 + sc_ref[...]) + sh_ref[...]
        h_scr[...] = h.astype(BF16)

    o_ref[...] = _dot(h_scr[...], w_ref[...])


def _inproj(x3, mod3, nw_row, w_bf, tm):
    bx, tx, _ = x3.shape
    r = mod3.shape[1]
    mod_b = (lambda b: b) if mod3.shape[0] == bx else (lambda b: 0)
    return pl.pallas_call(
        _inproj_kernel,
        grid=(bx, tx // tm, N_PROJ // PROJ_TN),
        in_specs=[pl.BlockSpec((None, tm, D_MODEL), lambda b, i, j: (b, i, 0)),
                  pl.BlockSpec((None, r, D_MODEL), lambda b, i, j: (mod_b(b), 0, 0)),
                  pl.BlockSpec((None, r, D_MODEL), lambda b, i, j: (mod_b(b), 0, 1)),
                  pl.BlockSpec((1, D_MODEL), lambda b, i, j: (0, 0)),
                  pl.BlockSpec((D_MODEL, PROJ_TN), lambda b, i, j: (0, j))],
        out_specs=pl.BlockSpec((None, tm, PROJ_TN), lambda b, i, j: (b, i, j)),
        out_shape=jax.ShapeDtypeStruct((bx, tx, N_PROJ), F32),
        scratch_shapes=[pltpu.VMEM((tm, D_MODEL), BF16)],
        compiler_params=_params(("arbitrary", "arbitrary", "arbitrary")),
        name="inproj",
    )(x3, mod3, mod3, nw_row, w_bf)


def _ssd_kernel(z_ref, xs_ref, bm_ref, cm_ref, dt_ref, cw_ref, cb_ref, dtb_ref, alog_ref, dvec_ref, nw_ref,
                y_ref, conv_ref, hout_ref, xpad, h_t, *, nchunks):
    c = pl.program_id(1)
    ln = SSD_CHUNK

    @pl.when(c == 0)
    def _():
        xpad[0:8, :] = jnp.zeros((8, CONV_DIM), F32)
        h_t[...] = jnp.zeros_like(h_t)

    xpad[8:8 + ln, 0:2048] = xs_ref[...]
    xpad[8:8 + ln, 2048:2560] = bm_ref[...]
    xpad[8:8 + ln, 2560:3072] = cm_ref[...]
    cw = cw_ref[...]
    conv = (xpad[5:5 + ln, :] * cw[0:1, :] + xpad[6:6 + ln, :] * cw[1:2, :] + xpad[7:7 + ln, :] * cw[2:3, :]
            + xpad[8:8 + ln, :] * cw[3:4, :] + cb_ref[...])
    xc = _silu(conv)
    xs_c = xc[:, 0:2048]
    bm_c = xc[:, 2048:2560]
    cm_c = xc[:, 2560:3072]

    dt = _softplus(dt_ref[...] + dtb_ref[...])
    a = -jnp.exp(alog_ref[...])
    row = lax.broadcasted_iota(jnp.int32, (ln, ln), 0)
    col = lax.broadcasted_iota(jnp.int32, (ln, ln), 1)
    causal = row >= col
    cs = _dot_hi(causal.astype(F32), dt * a)
    cs_t = cs.T
    dt_t = dt.T

    x_bf = xs_c.astype(BF16)
    bm_bf = bm_c.astype(BF16)
    cm_bf = cm_c.astype(BF16)
    lo = lax.broadcasted_iota(jnp.int32, (1, 128), 1) < 64
    ys = []
    for g in range(SSM_GROUPS):
        bg = bm_bf[:, g * 128:(g + 1) * 128]
        cg = cm_bf[:, g * 128:(g + 1) * 128]
        cb = _dot_nt(cg, bg)
        b_t = bm_c[:, g * 128:(g + 1) * 128].T
        for r2 in range(4):
            pair = g * 4 + r2
            xp = x_bf[:, pair * 128:(pair + 1) * 128]
            yd, st = [], []
            for h in (2 * pair, 2 * pair + 1):
                cs_col = cs[:, h:h + 1]
                cs_row = cs_t[h:h + 1, :]
                dt_row = dt_t[h:h + 1, :]
                lmat = jnp.where(causal, jnp.exp(cs_col - cs_row), 0.0)
                yd.append(_dot((cb * lmat * dt_row).astype(BF16), xp))
                w_row = dt_row * jnp.exp(cs_t[h:h + 1, ln - 1:ln] - cs_row)
                st.append(_dot((b_t * w_row).astype(BF16), xp))
            ha, hb = 2 * pair, 2 * pair + 1
            ecol = jnp.where(lo, jnp.exp(cs[:, ha:ha + 1]), jnp.exp(cs[:, hb:hb + 1]))
            hprev = h_t[pair]
            yoff = _dot(cg, hprev.astype(BF16)) * ecol
            ys.append(jnp.where(lo, yd[0], yd[1]) + yoff)
            edec = jnp.where(lo, jnp.exp(cs_t[ha:ha + 1, ln - 1:ln]), jnp.exp(cs_t[hb:hb + 1, ln - 1:ln]))
            h_t[pair] = hprev * edec + jnp.where(lo, st[0], st[1])
    y = jnp.concatenate(ys, axis=1) + xs_c * dvec_ref[...]
    yz = y * _silu(z_ref[...])
    ms = jnp.mean(yz * yz, axis=-1, keepdims=True)
    y_ref[...] = (yz * lax.rsqrt(ms + EPS) * nw_ref[...]).astype(BF16)

    xpad[0:8, :] = xpad[ln:ln + 8, :]

    @pl.when(c == nchunks - 1)
    def _():
        conv_ref[...] = xpad[ln + 5:ln + 8, :]
        for pair in range(16):
            hout_ref[pair * 128:(pair + 1) * 128, :] = h_t[pair].T


def _ssd_prompt(proj3, cw, cb_row, dtb_row, alog_row, dvec_row, nw_row):
    bx, tx, _ = proj3.shape
    nchunks = tx // SSD_CHUNK
    ln = SSD_CHUNK
    const = lambda shape: pl.BlockSpec(shape, lambda b, c: (0, 0))
    return pl.pallas_call(
        functools.partial(_ssd_kernel, nchunks=nchunks),
        grid=(bx, nchunks),
        in_specs=[pl.BlockSpec((None, ln, 2048), lambda b, c: (b, c, COL_Z // 2048)),
                  pl.BlockSpec((None, ln, 2048), lambda b, c: (b, c, COL_XS // 2048)),
                  pl.BlockSpec((None, ln, 512), lambda b, c: (b, c, COL_BM // 512)),
                  pl.BlockSpec((None, ln, 512), lambda b, c: (b, c, COL_CM // 512)),
                  pl.BlockSpec((None, ln, 128), lambda b, c: (b, c, COL_TAIL // 128)),
                  const((SSM_CONV, CONV_DIM)), const((1, CONV_DIM)), const((1, 128)), const((1, 128)),
                  const((1, D_INNER)), const((1, D_INNER))],
        out_specs=[pl.BlockSpec((None, ln, D_INNER), lambda b, c: (b, c, 0)),
                   pl.BlockSpec((None, SSM_CONV - 1, CONV_DIM), lambda b, c: (b, 0, 0)),
                   pl.BlockSpec((None, N_SSM_HEADS * SSM_HEAD_DIM, SSM_STATE), lambda b, c: (b, 0, 0))],
        out_shape=[jax.ShapeDtypeStruct((bx, tx, D_INNER), BF16),
                   jax.ShapeDtypeStruct((bx, SSM_CONV - 1, CONV_DIM), F32),
                   jax.ShapeDtypeStruct((bx, N_SSM_HEADS * SSM_HEAD_DIM, SSM_STATE), F32)],
        scratch_shapes=[pltpu.VMEM((ln + 8, CONV_DIM), F32), pltpu.VMEM((16, 128, 128), F32)],
        compiler_params=_params(("arbitrary", "arbitrary")),
        name="ssd_prompt",
    )(proj3, proj3, proj3, proj3, proj3, cw, cb_row, dtb_row, alog_row, dvec_row, nw_row)


def _ssm_pre_kernel(xs_ref, bm_ref, cm_ref, dt_ref, s0_ref, s1_ref, s2_ref, cw_ref, cb_ref, dtb_ref, alog_ref,
                    dvec_ref, eh_ref, xdt_ref, dec_ref, yd_ref, bmc_ref, cmc_ref, xdt_t_ref, dec_t_ref):
    cw = cw_ref[...]
    xbc = jnp.concatenate([xs_ref[...], bm_ref[...], cm_ref[...]], axis=1)
    conv = s0_ref[...] * cw[0:1, :] + s1_ref[...] * cw[1:2, :] + s2_ref[...] * cw[2:3, :] + xbc * cw[3:4, :] + cb_ref[...]
    xc = _silu(conv)
    xs_c = xc[:, 0:2048]
    bm_c = xc[:, 2048:2560]
    cm_c = xc[:, 2560:3072]
    dt = _softplus(dt_ref[...] + dtb_ref[...])
    a = -jnp.exp(alog_ref[...])
    dec = jnp.exp(dt * a)
    eh = eh_ref[...]
    dt_e = _dot_hi(dt, eh)
    dec_e = _dot_hi(dec, eh)
    xdt = xs_c * dt_e
    cbs = []
    for g in range(SSM_GROUPS):
        cbg = jnp.sum(cm_c[:, g * 128:(g + 1) * 128] * bm_c[:, g * 128:(g + 1) * 128], axis=-1, keepdims=True)
        cbs.append(jnp.broadcast_to(cbg, (cbg.shape[0], 512)))
    cb_e = jnp.concatenate(cbs, axis=1)
    xdt_ref[...] = xdt
    dec_ref[...] = dec_e
    yd_ref[...] = cb_e * xdt + xs_c * dvec_ref[...]
    bmc_ref[...] = bm_c
    cmc_ref[...] = cm_c
    for k in range(16):
        xdt_t_ref[k * 128:(k + 1) * 128, :] = xdt[:, k * 128:(k + 1) * 128].T
        dec_t_ref[k * 128:(k + 1) * 128, :] = dec_e[:, k * 128:(k + 1) * 128].T


def _ssm_pre(proj3, s0, s1, s2, cw, cb_row, dtb_row, alog_row, dvec_row, eh):
    nb = proj3.shape[1]
    const = lambda shape: pl.BlockSpec(shape, lambda i: (0,) * len(shape))
    return pl.pallas_call(
        _ssm_pre_kernel,
        grid=(1,),
        in_specs=[pl.BlockSpec((None, nb, 2048), lambda i: (0, 0, COL_XS // 2048)),
                  pl.BlockSpec((None, nb, 512), lambda i: (0, 0, COL_BM // 512)),
                  pl.BlockSpec((None, nb, 512), lambda i: (0, 0, COL_CM // 512)),
                  pl.BlockSpec((None, nb, 128), lambda i: (0, 0, COL_TAIL // 128)),
                  const((nb, CONV_DIM)), const((nb, CONV_DIM)), const((nb, CONV_DIM)),
                  const((SSM_CONV, CONV_DIM)), const((1, CONV_DIM)), const((1, 128)), const((1, 128)),
                  const((1, D_INNER)), const((128, D_INNER))],
        out_specs=[const((nb, D_INNER)), const((nb, D_INNER)), const((nb, D_INNER)), const((nb, 512)),
                   const((nb, 512)), const((D_INNER, nb)), const((D_INNER, nb))],
        out_shape=[jax.ShapeDtypeStruct((nb, D_INNER), F32)] * 3 + [jax.ShapeDtypeStruct((nb, 512), F32)] * 2
        + [jax.ShapeDtypeStruct((D_INNER, nb), F32)] * 2,
        compiler_params=_params(("arbitrary",)),
        name="ssm_pre",
    )(proj3, proj3, proj3, proj3, s0, s1, s2, cw, cb_row, dtb_row, alog_row, dvec_row, eh)


def _ssm_state_kernel(h0_ref, xdt_t_ref, dec_t_ref, bm_ref, cm_ref, hn_ref, yoff_ref):
    b = pl.program_id(0)
    h0 = h0_ref[...]
    onehot = (lax.broadcasted_iota(jnp.int32, (128, 128), 0) == b).astype(F32)
    dec_b = _dot_hi(dec_t_ref[...], onehot)
    outs, yoffs = [], []
    for g in range(SSM_GROUPS):
        bc_row = bm_ref[:, g * 128:(g + 1) * 128]
        outs.append(_dot_hi(xdt_t_ref[g * 512:(g + 1) * 512, :], onehot * bc_row))
        cm_row = cm_ref[:, g * 128:(g + 1) * 128]
        cm8 = jnp.broadcast_to(cm_row, (8, 128)).astype(BF16)
        yoffs.append(_dot_nt(cm8, h0[g * 512:(g + 1) * 512, :].astype(BF16)))
    hn_ref[...] = h0 * dec_b + jnp.concatenate(outs, axis=0)
    yoff_ref[...] = jnp.concatenate(yoffs, axis=1)


def _ssm_state(h0, xdt_t, dec_t, bm_c, cm_c):
    nb = h0.shape[0]
    const = lambda shape: pl.BlockSpec(shape, lambda b: (0, 0))
    return pl.pallas_call(
        _ssm_state_kernel,
        grid=(nb,),
        in_specs=[pl.BlockSpec((None, D_INNER, SSM_STATE), lambda b: (b, 0, 0)),
                  const((D_INNER, nb)), const((D_INNER, nb)),
                  pl.BlockSpec((None, 1, 512), lambda b: (b, 0, 0)), pl.BlockSpec((None, 1, 512), lambda b: (b, 0, 0))],
        out_specs=[pl.BlockSpec((None, D_INNER, SSM_STATE), lambda b: (b, 0, 0)),
                   pl.BlockSpec((None, 8, D_INNER), lambda b: (b, 0, 0))],
        out_shape=[jax.ShapeDtypeStruct((nb, D_INNER, SSM_STATE), F32),
                   jax.ShapeDtypeStruct((nb, 8, D_INNER), F32)],
        compiler_params=_params(("arbitrary",)),
        name="ssm_state",
    )(h0, xdt_t, dec_t, bm_c.reshape(nb, 1, 512), cm_c.reshape(nb, 1, 512))


def _ssm_post_kernel(yd_ref, yoff_ref, dec_ref, z_ref, nw_ref, y_ref):
    y = yd_ref[...] + yoff_ref[...] * dec_ref[...]
    yz = y * _silu(z_ref[...])
    ms = jnp.mean(yz * yz, axis=-1, keepdims=True)
    y_ref[...] = (yz * lax.rsqrt(ms + EPS) * nw_ref[...]).astype(BF16)


def _ssm_post(yd, yoff, dec_e, proj3, nw_row):
    nb = yd.shape[0]
    const = lambda shape: pl.BlockSpec(shape, lambda i: (0, 0))
    return pl.pallas_call(
        _ssm_post_kernel,
        grid=(1,),
        in_specs=[const((nb, D_INNER)), const((nb, D_INNER)), const((nb, D_INNER)),
                  pl.BlockSpec((None, nb, 2048), lambda i: (0, 0, COL_Z // 2048)), const((1, D_INNER))],
        out_specs=const((nb, D_INNER)),
        out_shape=jax.ShapeDtypeStruct((nb, D_INNER), BF16),
        compiler_params=_params(("arbitrary",)),
        name="ssm_post",
    )(yd, yoff, dec_e, proj3, nw_row)


def _rope128(x, c, s1, s2):
    return x * c + pltpu.roll(x, 120, axis=1) * s1 + pltpu.roll(x, 8, axis=1) * s2


def _rope_kernel(q_ref, k01_ref, k23_ref, k45_ref, tail_ref, c_ref, s1_ref, s2_ref,
                 qo_ref, kvrows_ref, ksa_ref, khm_ref, gates_ref, win_ref, *, ntiles, tr):
    c = c_ref[...]
    s1 = s1_ref[...]
    s2 = s2_ref[...]
    q = q_ref[...]
    qo_ref[...] = jnp.concatenate(
        [_rope128(q[:, k * 128:(k + 1) * 128], c, s1, s2) for k in range(8)], axis=1
    ).astype(BF16) * jnp.asarray(ATT_SCALE, BF16)
    streams = []
    for pref in (k01_ref, k23_ref, k45_ref):
        blk = pref[...]
        kk = jnp.concatenate([_rope128(blk[:, k * 128:(k + 1) * 128], c, s1, s2) for k in range(2)], axis=1)
        streams.append(kk)
        streams.append(blk[:, 256:512])
    kvrows_ref[...] = jnp.concatenate(streams[0:4], axis=1)
    pos = pl.program_id(1) * tr + lax.broadcasted_iota(jnp.int32, (tr, 64), 0)
    own = lax.broadcasted_iota(jnp.int32, (tr, 64), 1) == lax.shift_right_logical(pos, 6)
    extra = jnp.where(own, NEG, 0.0).astype(BF16)
    ksel = streams[2].astype(BF16)
    for h in range(N_KV_HEADS):
        ksa_ref[h] = jnp.concatenate([ksel[:, h * 64:(h + 1) * 64], extra], axis=1)
    for i, s in enumerate((3, 4, 5)):
        sb = streams[s].astype(BF16)
        for h in range(N_KV_HEADS):
            khm_ref[i * 4 + h] = sb[:, h * 64:(h + 1) * 64]
    g = _sig(tail_ref[...])
    for hk in range(N_KV_HEADS):
        gates_ref[hk] = g[:, 32 + hk * 12:32 + (hk + 1) * 12]

    @pl.when(pl.program_id(1) == ntiles - 1)
    def _():
        win_ref[...] = jnp.concatenate(streams[4:6], axis=1)


def _rope_prep(proj3, ctab, s1tab, s2tab, tr):
    bx, tx, _ = proj3.shape
    ntiles = tx // tr
    tab = pl.BlockSpec((tr, 128), lambda b, i: (i, 0))
    return pl.pallas_call(
        functools.partial(_rope_kernel, ntiles=ntiles, tr=tr),
        grid=(bx, ntiles),
        in_specs=[pl.BlockSpec((None, tr, 1024), lambda b, i: (b, i, COL_Q // 1024)),
                  pl.BlockSpec((None, tr, 512), lambda b, i: (b, i, COL_KV // 512)),
                  pl.BlockSpec((None, tr, 512), lambda b, i: (b, i, COL_KV // 512 + 1)),
                  pl.BlockSpec((None, tr, 512), lambda b, i: (b, i, COL_KV // 512 + 2)),
                  pl.BlockSpec((None, tr, 128), lambda b, i: (b, i, COL_TAIL // 128)),
                  tab, tab, tab],
        out_specs=[pl.BlockSpec((None, tr, 1024), lambda b, i: (b, i, 0)),
                   pl.BlockSpec((None, tr, 1024), lambda b, i: (b, i, 0)),
                   pl.BlockSpec((None, 4, tr, 128), lambda b, i: (b, 0, i, 0)),
                   pl.BlockSpec((None, 12, tr, 64), lambda b, i: (b, 0, i, 0)),
                   pl.BlockSpec((None, 4, tr, 12), lambda b, i: (b, 0, i, 0)),
                   pl.BlockSpec((None, tr, 512), lambda b, i: (b, 0, 0))],
        out_shape=[jax.ShapeDtypeStruct((bx, tx, 1024), BF16),
                   jax.ShapeDtypeStruct((bx, tx, 1024), F32),
                   jax.ShapeDtypeStruct((bx, 4, tx, 128), BF16),
                   jax.ShapeDtypeStruct((bx, 12, tx, 64), BF16),
                   jax.ShapeDtypeStruct((bx, 4, tx, 12), F32),
                   jax.ShapeDtypeStruct((bx, tr, 512), F32)],
        compiler_params=_params(("arbitrary", "arbitrary")),
        name="rope_prep",
    )(proj3, proj3, proj3, proj3, proj3, ctab, s1tab, s2tab)


def _compress_kernel(*refs, nrefs, npages, head_major):
    if nrefs > 1:
        refs = refs[1:]
    page_refs = refs[:nrefs]
    wbd_ref, w1f_ref, pe_ref, w2bd_ref, kc_ref, vc_ref, shift = refs[nrefs:]
    nsub = PAGE_SIZE // CMP_STRIDE
    nj = npages * nsub
    shift[nj:nj + 8, :] = jnp.zeros((8, 512), F32)
    ri = lax.broadcasted_iota(jnp.int32, (PAGE_SIZE, PAGE_SIZE), 0)
    ci = lax.broadcasted_iota(jnp.int32, (PAGE_SIZE, PAGE_SIZE), 1)
    perm = (ci == (ri & (nsub - 1)) * CMP_STRIDE + lax.shift_right_logical(ri, 3)).astype(BF16)
    if nrefs == 1:
        pages = [page_refs[0][p * PAGE_SIZE:(p + 1) * PAGE_SIZE, :] for p in range(npages)]
    else:
        pages = [pr[...] for pr in page_refs]
    grouped = [_dot(perm, pg.astype(BF16)) for pg in pages]
    for st, o_ref in ((0, kc_ref), (1, vc_ref)):
        acc = None
        for s in range(CMP_STRIDE):
            parts = [y[s * nsub:(s + 1) * nsub, st * 256:(st + 1) * 256] for y in grouped]
            xs = parts[0] if npages == 1 else jnp.concatenate(parts, axis=0)
            t = _dot(xs.astype(BF16), wbd_ref[st, s])
            acc = t if acc is None else acc + t
        shift[0:nj, :] = acc[:, 512:1024]
        pe8 = jnp.broadcast_to(pe_ref[st], (8, CMP_LEN * ATT_HEAD_DIM)).astype(BF16)
        pe_t = _dot(pe8, w1f_ref[st])[0:1, :]
        hid = acc[:, 0:512] + shift[pl.ds(1, nj), :] + jnp.concatenate([pe_t] * 4, axis=1)
        out = _dot(_silu(hid).astype(BF16), w2bd_ref[st]).astype(BF16)
        if head_major:
            for h in range(N_KV_HEADS):
                o_ref[h] = out[:, h * 64:(h + 1) * 64]
        else:
            o_ref[...] = out


def _compress_prompt(kv_rows, wbd, w1f, pe, w2bd):
    bx, tx, _ = kv_rows.shape
    nj = tx // CMP_STRIDE
    c4 = lambda shape: pl.BlockSpec(shape, lambda b: (0,) * len(shape))
    return pl.pallas_call(
        functools.partial(_compress_kernel, nrefs=1, npages=tx // PAGE_SIZE, head_major=True),
        grid=(bx,),
        in_specs=[pl.BlockSpec((None, tx, 512), lambda b: (b, 0, 0)),
                  c4(wbd.shape), c4(w1f.shape), c4(pe.shape), c4(w2bd.shape)],
        out_specs=[pl.BlockSpec((None, 4, nj, 64), lambda b: (b, 0, 0, 0))] * 2,
        out_shape=[jax.ShapeDtypeStruct((bx, 4, nj, 64), BF16)] * 2,
        scratch_shapes=[pltpu.VMEM((nj + 8, 512), F32)],
        compiler_params=_params(("arbitrary",)),
        name="compress_prompt",
    )(kv_rows, wbd, w1f, pe, w2bd)


def _compress_sample(page_table, cache3, wbd, w1f, pe, w2bd):
    nb, npages = page_table.shape
    nsub = PAGE_SIZE // CMP_STRIDE
    nj = npages * nsub
    c4 = lambda shape: pl.BlockSpec(shape, lambda b, pt: (0,) * len(shape))
    page_specs = [pl.BlockSpec((None, PAGE_SIZE, 512), functools.partial(lambda b, pt, p: (pt[b, p], 0, 0), p=p))
                  for p in range(npages)]
    grid_spec = pltpu.PrefetchScalarGridSpec(
        num_scalar_prefetch=1,
        grid=(nb,),
        in_specs=page_specs + [c4(wbd.shape), c4(w1f.shape), c4(pe.shape), c4(w2bd.shape)],
        out_specs=[pl.BlockSpec((None, nj, 256), lambda b, pt: (b, 0, 0))] * 2,
        scratch_shapes=[pltpu.VMEM((nj + 8, 512), F32)],
    )
    return pl.pallas_call(
        functools.partial(_compress_kernel, nrefs=npages, npages=npages, head_major=False),
        grid_spec=grid_spec,
        out_shape=[jax.ShapeDtypeStruct((nb, nj, 256), BF16)] * 2,
        compiler_params=_params(("arbitrary",)),
        name="compress_sample",
    )(page_table, *([cache3] * npages), wbd, w1f, pe, w2bd)


KEY_BLOCK = 512


def _nsa_prompt_kernel(q_ref, g_ref, kc_ref, vc_ref, ksa_ref, vs_ref, kw_ref, vw_ref, ovt_ref,
                       o_ref, s_scr, m_scr, l_scr, acc_scr):
    qb = pl.program_id(2)
    q0 = qb * Q_BLOCK
    q4 = q_ref[...]
    nq = Q_PER_KV * Q_BLOCK
    qs = jnp.concatenate([q4[:, r * 64:(r + 1) * 64] for r in range(Q_PER_KV)], axis=0)
    ncmp = kc_ref.shape[0]
    trow = q0 + (lax.broadcasted_iota(jnp.int32, (nq, 1), 0) & (Q_BLOCK - 1))

    jl = lax.broadcasted_iota(jnp.int32, (nq, ncmp), 1)
    mask_c = (jl * CMP_STRIDE + (CMP_LEN - 1)) <= trow
    s = jnp.where(mask_c, _dot_nt(qs, kc_ref[...]), NEG)
    e = jnp.where(mask_c, jnp.exp(s - jnp.max(s, axis=-1, keepdims=True)), 0.0)
    p = e / jnp.maximum(jnp.sum(e, axis=-1, keepdims=True), TINY)
    o_c = _dot(p.astype(BF16), vc_ref[...])
    psum = p[0:128] + p[128:256] + p[256:384] + p[384:512]
    imp_t = _dot_nt_hi(ovt_ref[...], psum)
    nblk = imp_t.shape[0]
    tq = q0 + lax.broadcasted_iota(jnp.int32, (nblk, 128), 1)
    blk = lax.broadcasted_iota(jnp.int32, (nblk, 128), 0)
    cur = lax.shift_right_logical(tq, 6)
    valid = blk * SEL_LEN <= tq
    forced = (blk == 0) | (blk == cur) | (blk == cur - 1)
    imp = jnp.where(valid, jnp.where(forced, BIG, imp_t), -BIG)
    rank = jnp.zeros((nblk, 128), F32)
    for j in range(nblk):
        rj = imp[j:j + 1, :]
        beats = (rj > imp) | ((rj == imp) & (blk > j))
        rank = rank + jnp.where(beats, 1.0, 0.0)
    nsel_t = jnp.where(rank < float(SEL_TOPK), 0.0, 1.0)
    nsel_pad = jnp.concatenate([nsel_t, jnp.zeros((128 - nblk, 128), F32)], axis=0) if nblk < 128 else nsel_t
    nsel = nsel_pad.T[:, 0:64].astype(BF16)
    qaug = jnp.concatenate([qs, jnp.concatenate([nsel] * Q_PER_KV, axis=0)], axis=1)

    nck = KEY_BLOCK // 128

    def scores(jb, causal):
        base = pl.multiple_of(jb * KEY_BLOCK, KEY_BLOCK)
        mx = None
        for c in range(nck):
            k = ksa_ref[pl.ds(base + c * 128, 128), :]
            sc = _dot_nt(qaug, k)
            if causal:
                kpos = base + c * 128 + lax.broadcasted_iota(jnp.int32, (nq, 128), 1)
                sc = jnp.where(kpos <= trow, sc, NEG)
            s_scr[jb, :, c * 128:(c + 1) * 128] = sc
            mx = sc if mx is None else jnp.maximum(mx, sc)
        m_scr[...] = jnp.maximum(m_scr[...], mx)

    nfull = lax.shift_right_logical(qb, 2)
    m_scr[...] = jnp.full((nq, 128), -jnp.inf, F32)

    def pass_a(jb, carry):
        scores(jb, False)
        return carry

    lax.fori_loop(0, nfull, pass_a, 0)
    scores(nfull, True)
    mb = jnp.broadcast_to(jnp.max(m_scr[...], axis=-1, keepdims=True), (nq, 128))

    l_scr[...] = jnp.zeros((nq, 128), F32)
    acc_scr[...] = jnp.zeros((nq, 64), F32)

    def pass_b(jb, carry):
        base = pl.multiple_of(jb * KEY_BLOCK, KEY_BLOCK)
        ps, lsum = [], None
        for c in range(nck):
            pc = jnp.exp(s_scr[jb, :, c * 128:(c + 1) * 128] - mb)
            lsum = pc if lsum is None else lsum + pc
            ps.append(pc.astype(BF16))
        l_scr[...] = l_scr[...] + lsum
        acc_scr[...] = acc_scr[...] + _dot(jnp.concatenate(ps, axis=1), vs_ref[pl.ds(base, KEY_BLOCK), :])
        return carry

    lax.fori_loop(0, nfull + 1, pass_b, 0)
    o_s = acc_scr[...] / jnp.maximum(jnp.sum(l_scr[...], axis=-1, keepdims=True), TINY)

    wlen = WINDOW + Q_BLOCK
    wstart = pl.multiple_of(jnp.maximum(q0 - WINDOW, 0), 128)
    kw = kw_ref[pl.ds(wstart, wlen), :]
    vw = vw_ref[pl.ds(wstart, wlen), :]
    o_w = []
    for r in range(Q_PER_KV):
        d = trow[0:Q_BLOCK] - (wstart + lax.broadcasted_iota(jnp.int32, (Q_BLOCK, wlen), 1))
        ok = (d >= 0) & (d < WINDOW)
        sw = jnp.where(ok, _dot_nt(qs[r * Q_BLOCK:(r + 1) * Q_BLOCK], kw), NEG)
        ew = jnp.exp(sw - jnp.max(sw, axis=-1, keepdims=True))
        o_w.append(_dot(ew.astype(BF16), vw) / jnp.maximum(jnp.sum(ew, axis=-1, keepdims=True), TINY))

    g = g_ref[...]
    outs = []
    for r in range(Q_PER_KV):
        rows = slice(r * Q_BLOCK, (r + 1) * Q_BLOCK)
        outs.append(g[:, 3 * r:3 * r + 1] * o_c[rows] + g[:, 3 * r + 1:3 * r + 2] * o_s[rows]
                    + g[:, 3 * r + 2:3 * r + 3] * o_w[r])
    o_ref[...] = jnp.concatenate(outs, axis=1).astype(BF16)


def _nsa_prompt(q_r, gates, kc, vc, ksa, khm, ovt):
    bx, tx, _ = q_r.shape
    ntiles = tx // Q_BLOCK
    ncmp = kc.shape[2]
    nq = Q_PER_KV * Q_BLOCK
    assert tx % KEY_BLOCK == 0 and tx >= WINDOW + Q_BLOCK and tx <= 64 * SEL_LEN
    kv_spec = lambda s: pl.BlockSpec((None, None, tx, 64), lambda b, h, i: (b, s * 4 + h, 0, 0))
    return pl.pallas_call(
        _nsa_prompt_kernel,
        grid=(bx, N_KV_HEADS, ntiles),
        in_specs=[pl.BlockSpec((None, Q_BLOCK, 256), lambda b, h, i: (b, i, h)),
                  pl.BlockSpec((None, None, Q_BLOCK, 12), lambda b, h, i: (b, h, i, 0)),
                  pl.BlockSpec((None, None, ncmp, 64), lambda b, h, i: (b, h, 0, 0)),
                  pl.BlockSpec((None, None, ncmp, 64), lambda b, h, i: (b, h, 0, 0)),
                  pl.BlockSpec((None, None, tx, 128), lambda b, h, i: (b, h, 0, 0)),
                  kv_spec(0), kv_spec(1), kv_spec(2),
                  pl.BlockSpec(ovt.shape, lambda b, h, i: (0, 0))],
        out_specs=pl.BlockSpec((None, Q_BLOCK, 256), lambda b, h, i: (b, i, h)),
        out_shape=jax.ShapeDtypeStruct((bx, tx, ATT_WIDTH), BF16),
        scratch_shapes=[pltpu.VMEM((tx // KEY_BLOCK, nq, KEY_BLOCK), F32), pltpu.VMEM((nq, 128), F32),
                        pltpu.VMEM((nq, 128), F32), pltpu.VMEM((nq, 64), F32)],
        compiler_params=_params(("arbitrary", "arbitrary", "arbitrary")),
        name="nsa_prompt",
    )(q_r, gates, kc, vc, ksa, khm, khm, khm, ovt)


def _fold_heads(o256, hmask):
    o = o256 * hmask
    return o[:, 0:64] + o[:, 64:128] + o[:, 128:192] + o[:, 192:256]


def _nsa_sample_kernel(*refs, npages):
    refs = refs[1:]
    pages = refs[:npages]
    (q_ref, g_ref, knew_ref, wnew_ref, kc_ref, vc_ref, win_ref, ov_ref, e_ref,
     o_ref, wout_ref, wshift) = refs[npages:]
    past = npages * PAGE_SIZE
    ncmp = kc_ref.shape[0]
    ntok = (past - CMP_LEN) // CMP_STRIDE + 1
    nsel = past // SEL_LEN + 1

    q16 = q_ref[...].astype(F32)
    rowh = lax.broadcasted_iota(jnp.int32, (16, 256), 0)
    laneh = lax.broadcasted_iota(jnp.int32, (16, 256), 1)
    hmask = (lax.shift_right_logical(rowh, 2) == lax.shift_right_logical(laneh, 6)).astype(F32)
    qbd_f = jnp.concatenate([q16] * 4, axis=1) * hmask
    qbd = qbd_f.astype(BF16)

    jl = lax.broadcasted_iota(jnp.int32, (16, ncmp), 1)
    mask_c = jl < ntok
    s = jnp.where(mask_c, _dot_nt(qbd, kc_ref[...]), NEG)
    e = jnp.where(mask_c, jnp.exp(s - jnp.max(s, axis=-1, keepdims=True)), 0.0)
    p_c = e / jnp.maximum(jnp.sum(e, axis=-1, keepdims=True), TINY)
    o_c = _fold_heads(_dot(p_c.astype(BF16), vc_ref[...]), hmask)

    gm = (lax.shift_right_logical(lax.broadcasted_iota(jnp.int32, (8, 16), 1), 2)
          == lax.broadcasted_iota(jnp.int32, (8, 16), 0)).astype(F32)
    imp = _dot_hi(_dot_hi(gm, p_c), ov_ref[...])
    blk = lax.broadcasted_iota(jnp.int32, (8, 128), 1)
    exists = blk < nsel
    forced = (blk == 0) | (blk == nsel - 1) | (blk == nsel - 2)
    imp = jnp.where(exists, jnp.where(forced, BIG, imp), -BIG)
    rank = jnp.zeros((8, 128), F32)
    for j in range(nsel):
        cj = imp[:, j:j + 1]
        beats = (cj > imp) | ((cj == imp) & (blk > j))
        rank = rank + jnp.where(beats, 1.0, 0.0)
    sel8 = jnp.where((rank < float(min(SEL_TOPK, nsel))) & exists, 1.0, 0.0)
    gm_t = (lax.shift_right_logical(lax.broadcasted_iota(jnp.int32, (16, 8), 0), 2)
            == lax.broadcasted_iota(jnp.int32, (16, 8), 1)).astype(F32)
    sel16 = _dot(gm_t, sel8)
    bias_past = (_dot(sel16.astype(BF16), e_ref[...]) - 1.0) * BIG

    knew = knew_ref[...]
    s_new = jnp.sum(qbd_f * knew[:, 512:768], axis=-1, keepdims=True)
    s_past = jnp.concatenate(
        [_dot_nt(qbd, pg[:, 0:256].astype(BF16)) for pg in pages], axis=1) + bias_past
    m = jnp.maximum(jnp.max(s_past, axis=-1, keepdims=True), s_new)
    e_past = jnp.exp(s_past - m)
    e_new = jnp.exp(s_new - m)
    acc = e_new * knew[:, 768:1024]
    for p, pg in enumerate(pages):
        acc = acc + _dot(e_past[:, p * 128:(p + 1) * 128].astype(BF16), pg[:, 256:512].astype(BF16))
    lsum = jnp.sum(e_past, axis=-1, keepdims=True) + e_new
    o_s = _fold_heads(acc / jnp.maximum(lsum, TINY), hmask)

    win = win_ref[...]
    wnew = wnew_ref[...]
    wl = lax.broadcasted_iota(jnp.int32, (16, WINDOW), 1)
    s_w = jnp.where(wl >= 1, _dot_nt(qbd, win[:, 0:256].astype(BF16)), NEG)
    s_wn = jnp.sum(qbd_f * wnew[:, 0:256], axis=-1, keepdims=True)
    mw = jnp.maximum(jnp.max(s_w, axis=-1, keepdims=True), s_wn)
    e_w = jnp.where(wl >= 1, jnp.exp(s_w - mw), 0.0)
    e_wn = jnp.exp(s_wn - mw)
    acc_w = _dot(e_w.astype(BF16), win[:, 256:512].astype(BF16)) + e_wn * wnew[:, 256:512]
    o_w = _fold_heads(acc_w / jnp.maximum(jnp.sum(e_w, axis=-1, keepdims=True) + e_wn, TINY), hmask)

    g = _sig(g_ref[...])
    o_ref[...] = (g[:, 0:1] * o_c + g[:, 1:2] * o_s + g[:, 2:3] * o_w).astype(BF16)

    wshift[0:WINDOW, :] = win
    wshift[WINDOW:WINDOW + 8, :] = jnp.broadcast_to(wnew, (8, 512))
    wout_ref[...] = wshift[pl.ds(1, WINDOW), :]


def _nsa_sample(page_table, cache3, q16, g16, knew, wnew, kc, vc, win, ov, emat):
    nb, npages = page_table.shape
    ncmp = kc.shape[1]
    c2 = lambda shape: pl.BlockSpec(shape, lambda b, pt: (0,) * len(shape))
    page_specs = [pl.BlockSpec((None, PAGE_SIZE, 512), functools.partial(lambda b, pt, p: (pt[b, p], 0, 1), p=p))
                  for p in range(npages)]
    grid_spec = pltpu.PrefetchScalarGridSpec(
        num_scalar_prefetch=1,
        grid=(nb,),
        in_specs=page_specs + [
            pl.BlockSpec((None, 16, 64), lambda b, pt: (b, 0, 0)),
            pl.BlockSpec((None, 16, 3), lambda b, pt: (b, 0, 0)),
            pl.BlockSpec((None, 1, 1024), lambda b, pt: (b, 0, 0)),
            pl.BlockSpec((None, 1, 512), lambda b, pt: (b, 0, 0)),
            pl.BlockSpec((None, ncmp, 256), lambda b, pt: (b, 0, 0)),
            pl.BlockSpec((None, ncmp, 256), lambda b, pt: (b, 0, 0)),
            pl.BlockSpec((None, WINDOW, 512), lambda b, pt: (b, 0, 0)),
            c2(ov.shape), c2(emat.shape)],
        out_specs=[pl.BlockSpec((None, 16, 64), lambda b, pt: (b, 0, 0)),
                   pl.BlockSpec((None, WINDOW, 512), lambda b, pt: (b, 0, 0))],
        scratch_shapes=[pltpu.VMEM((WINDOW + 8, 512), F32)],
    )
    return pl.pallas_call(
        functools.partial(_nsa_sample_kernel, npages=npages),
        grid_spec=grid_spec,
        out_shape=[jax.ShapeDtypeStruct((nb, 16, 64), BF16),
                   jax.ShapeDtypeStruct((nb, WINDOW, 512), F32)],
        compiler_params=_params(("arbitrary",)),
        name="nsa_sample",
    )(page_table, *([cache3] * npages), q16, g16, knew, wnew, kc, vc, win, ov, emat)


def _merge_kernel(x_ref, yssm_ref, yatt_ref, gs_ref, ga_ref, g1_ref, sh2_ref, sc2_ref, nw_ref,
                  wss_ref, wat_ref, wo_ref, x1_ref, h2_ref):
    ms = _dot(yssm_ref[...], wss_ref[...])
    ma = _dot(yatt_ref[...], wat_ref[...])
    merged = _sig(gs_ref[...]) * ms + _sig(ga_ref[...]) * ma
    x1 = x_ref[...] + g1_ref[...] * _dot(merged.astype(BF16), wo_ref[...])
    x1_ref[...] = x1
    y = x1 * lax.rsqrt(jnp.mean(x1 * x1, axis=-1, keepdims=True) + EPS)
    h2_ref[...] = (y * nw_ref[...] * (1.0 + sc2_ref[...]) + sh2_ref[...]).astype(BF16)


def _merge_out(x3, yssm, yatt, proj3, mod3, nw_row, wss, wat, wo, tm):
    bx, tx, _ = x3.shape
    r = mod3.shape[1]
    mod_b = (lambda b: b) if mod3.shape[0] == bx else (lambda b: 0)
    row = lambda w, cb: pl.BlockSpec((None, tm, w), lambda b, i: (b, i, cb))
    modc = lambda cb: pl.BlockSpec((None, r, D_MODEL), lambda b, i: (mod_b(b), 0, cb))
    const = lambda shape: pl.BlockSpec(shape, lambda b, i: (0, 0))
    return pl.pallas_call(
        _merge_kernel,
        grid=(bx, tx // tm),
        in_specs=[row(D_MODEL, 0), row(D_INNER, 0), row(ATT_WIDTH, 0),
                  row(1024, COL_GM // 1024), row(1024, COL_GM // 1024 + 1),
                  modc(2), modc(3), modc(4), const((1, D_MODEL)),
                  const(wss.shape), const(wat.shape), const(wo.shape)],
        out_specs=[row(D_MODEL, 0), row(D_MODEL, 0)],
        out_shape=[jax.ShapeDtypeStruct((bx, tx, D_MODEL), F32), jax.ShapeDtypeStruct((bx, tx, D_MODEL), BF16)],
        compiler_params=_params(("arbitrary", "arbitrary")),
        name="merge_out",
    )(x3, yssm, yatt, proj3, proj3, mod3, mod3, mod3, nw_row, wss, wat, wo)


FFN_TN = 256
FFN_NT = D_FF // FFN_TN


def _ffn_up_prompt_kernel(h2_ref, wa_ref, wb_ref, cwa_ref, cwb_ref, cba_ref, cbb_ref,
                          act_ref, fa_ref, fb_ref, work, tails, *, tm, ntiles):
    i = pl.program_id(1)
    j = pl.program_id(2)
    h2 = h2_ref[...]
    u = jnp.concatenate([_dot(h2, wa_ref[...]), _dot(h2, wb_ref[...])], axis=1)
    work[8:8 + tm, :] = u

    @pl.when(i == 0)
    def _():
        work[0:8, :] = jnp.zeros((8, 2 * FFN_TN), F32)

    @pl.when(i > 0)
    def _():
        work[0:8, :] = tails[j]

    cw = jnp.concatenate([cwa_ref[...], cwb_ref[...]], axis=1)
    cb = jnp.concatenate([cba_ref[...], cbb_ref[...]], axis=1)
    conv = work[6:6 + tm, :] * cw[0:1, :] + work[7:7 + tm, :] * cw[1:2, :] + work[8:8 + tm, :] * cw[2:3, :] + cb
    act_ref[...] = (_silu(conv[:, 0:FFN_TN]) * conv[:, FFN_TN:2 * FFN_TN]).astype(BF16)
    tails[j] = work[tm:tm + 8, :]
    fa_ref[...] = work[tm + 6:tm + 8, 0:FFN_TN]
    fb_ref[...] = work[tm + 6:tm + 8, FFN_TN:2 * FFN_TN]


def _ffn_up_prompt(h2, wup, cw, cb_row, tm):
    bx, tx, _ = h2.shape
    ntiles = tx // tm
    return pl.pallas_call(
        functools.partial(_ffn_up_prompt_kernel, tm=tm, ntiles=ntiles),
        grid=(bx, ntiles, FFN_NT),
        in_specs=[pl.BlockSpec((None, tm, D_MODEL), lambda b, i, j: (b, i, 0)),
                  pl.BlockSpec((D_MODEL, FFN_TN), lambda b, i, j: (0, j)),
                  pl.BlockSpec((D_MODEL, FFN_TN), lambda b, i, j: (0, j + FFN_NT)),
                  pl.BlockSpec((FFN_CONV, FFN_TN), lambda b, i, j: (0, j)),
                  pl.BlockSpec((FFN_CONV, FFN_TN), lambda b, i, j: (0, j + FFN_NT)),
                  pl.BlockSpec((1, FFN_TN), lambda b, i, j: (0, j)),
                  pl.BlockSpec((1, FFN_TN), lambda b, i, j: (0, j + FFN_NT))],
        out_specs=[pl.BlockSpec((None, tm, FFN_TN), lambda b, i, j: (b, i, j)),
                   pl.BlockSpec((None, None, FFN_CONV - 1, FFN_TN), lambda b, i, j: (b, i, 0, j)),
                   pl.BlockSpec((None, None, FFN_CONV - 1, FFN_TN), lambda b, i, j: (b, i, 0, j))],
        out_shape=[jax.ShapeDtypeStruct((bx, tx, D_FF), BF16),
                   jax.ShapeDtypeStruct((bx, ntiles, FFN_CONV - 1, D_FF), F32),
                   jax.ShapeDtypeStruct((bx, ntiles, FFN_CONV - 1, D_FF), F32)],
        scratch_shapes=[pltpu.VMEM((tm + 8, 2 * FFN_TN), F32), pltpu.VMEM((FFN_NT, 8, 2 * FFN_TN), F32)],
        compiler_params=_params(("arbitrary", "arbitrary", "arbitrary")),
        name="ffn_up_prompt",
    )(h2, wup, wup, cw, cw, cb_row, cb_row)


def _ffn_up_sample_kernel(h2_ref, wa_ref, wb_ref, h0a_ref, h0b_ref, h1a_ref, h1b_ref, cwa_ref, cwb_ref,
                          cba_ref, cbb_ref, act_ref, ua_ref, ub_ref):
    h2 = h2_ref[...]
    ua = _dot(h2, wa_ref[...])
    ub = _dot(h2, wb_ref[...])
    cwa = cwa_ref[...]
    cwb = cwb_ref[...]
    ca = h0a_ref[...] * cwa[0:1, :] + h1a_ref[...] * cwa[1:2, :] + ua * cwa[2:3, :] + cba_ref[...]
    cb = h0b_ref[...] * cwb[0:1, :] + h1b_ref[...] * cwb[1:2, :] + ub * cwb[2:3, :] + cbb_ref[...]
    act_ref[...] = (_silu(ca) * cb).astype(BF16)
    ua_ref[...] = ua
    ub_ref[...] = ub


def _ffn_up_sample(h2, wup, hist0, hist1, cw, cb_row):
    nb = h2.shape[0]
    col = lambda rows, off: pl.BlockSpec((rows, FFN_TN), lambda j: (0, j + off))
    return pl.pallas_call(
        _ffn_up_sample_kernel,
        grid=(FFN_NT,),
        in_specs=[pl.BlockSpec((nb, D_MODEL), lambda j: (0, 0)),
                  col(D_MODEL, 0), col(D_MODEL, FFN_NT), col(nb, 0), col(nb, FFN_NT), col(nb, 0), col(nb, FFN_NT),
                  col(FFN_CONV, 0), col(FFN_CONV, FFN_NT), col(1, 0), col(1, FFN_NT)],
        out_specs=[col(nb, 0), col(nb, 0), col(nb, 0)],
        out_shape=[jax.ShapeDtypeStruct((nb, D_FF), BF16), jax.ShapeDtypeStruct((nb, D_FF), F32),
                   jax.ShapeDtypeStruct((nb, D_FF), F32)],
        compiler_params=_params(("arbitrary",)),
        name="ffn_up_sample",
    )(h2, wup, wup, hist0, hist0, hist1, hist1, cw, cw, cb_row, cb_row)


def _ffn_down_kernel(act_ref, x1_ref, g2_ref, nw_ref, w_ref, y_ref):
    x2 = x1_ref[...] + g2_ref[...] * _dot(act_ref[...], w_ref[...])
    y_ref[...] = x2 * lax.rsqrt(jnp.mean(x2 * x2, axis=-1, keepdims=True) + EPS) * nw_ref[...]


def _ffn_down(act, x1, mod3, nw_row, wdown, tm):
    bx, tx, _ = x1.shape
    r = mod3.shape[1]
    mod_b = (lambda b: b) if mod3.shape[0] == bx else (lambda b: 0)
    return pl.pallas_call(
        _ffn_down_kernel,
        grid=(bx, tx // tm),
        in_specs=[pl.BlockSpec((None, tm, D_FF), lambda b, i: (b, i, 0)),
                  pl.BlockSpec((None, tm, D_MODEL), lambda b, i: (b, i, 0)),
                  pl.BlockSpec((None, r, D_MODEL), lambda b, i: (mod_b(b), 0, 5)),
                  pl.BlockSpec((1, D_MODEL), lambda b, i: (0, 0)),
                  pl.BlockSpec((D_FF, D_MODEL), lambda b, i: (0, 0))],
        out_specs=pl.BlockSpec((None, tm, D_MODEL), lambda b, i: (b, i, 0)),
        out_shape=jax.ShapeDtypeStruct((bx, tx, D_MODEL), F32),
        compiler_params=_params(("arbitrary", "arbitrary")),
        name="ffn_down",
    )(act, x1, mod3, nw_row, wdown)


def _rope_tables(pos):
    half = ROPE_DIM // 2
    inv_freq = ROPE_THETA ** (-jnp.arange(half, dtype=F32) / half)
    ang = pos.astype(F32)[:, None] * inv_freq[None, :]
    cos, sin = jnp.cos(ang), jnp.sin(ang)
    n = pos.shape[0]
    ones = jnp.ones((n, ATT_HEAD_DIM - ROPE_DIM), F32)
    zeros8 = jnp.zeros((n, half), F32)
    zeros48 = jnp.zeros((n, ATT_HEAD_DIM - ROPE_DIM), F32)
    c = jnp.concatenate([cos, cos, ones], axis=1)
    s1 = jnp.concatenate([-sin, zeros8, zeros48], axis=1)
    s2 = jnp.concatenate([zeros8, sin, zeros48], axis=1)
    return tuple(jnp.concatenate([t, t], axis=1) for t in (c, s1, s2))


def _overlap(nc, ns):
    cst = np.arange(nc)[:, None] * CMP_STRIDE
    sst = np.arange(ns)[None, :] * SEL_LEN
    ov = np.clip(np.minimum(cst + CMP_LEN, sst + SEL_LEN) - np.maximum(cst, sst), 0, None)
    return (ov / CMP_STRIDE).astype(np.float32)


def _cmp_weights(pe, w1, w2):
    eye = jnp.eye(N_KV_HEADS, dtype=F32)
    wab = jnp.stack([w1[:CMP_STRIDE], w1[CMP_STRIDE:]], axis=1)
    wbd = jnp.einsum("hk,sade->shdake", eye, wab).reshape(CMP_STRIDE, 256, 1024)
    w2bd = jnp.einsum("hk,ed->hekd", eye, w2).reshape(4 * CMP_HIDDEN, 256)
    return wbd.astype(BF16), w1.reshape(CMP_LEN * ATT_HEAD_DIM, CMP_HIDDEN).astype(BF16), pe.reshape(1, -1), w2bd.astype(BF16)


def kernel(x_prompt, x_sample, c_prompt, c_sample, cache_nsa_kv, page_table, cache_win_kv, state_ssm, state_ssm_conv, state_ffn_conv, ada_w, ada_b, norm1_w, norm2_w, final_norm_w, w_in, ssm_conv_w, ssm_conv_b, ssm_dt_bias, ssm_A_log, ssm_D, ssm_norm_w, cmp_pe_k, cmp_w1_k, cmp_w2_k, cmp_pe_v, cmp_w1_v, cmp_w2_v, w_ssm_out, w_att_out, w_out, ffn_w_up, ffn_conv_w, ffn_conv_b, ffn_w_down):
    bp, tp, _ = x_prompt.shape
    nb = x_sample.shape[0]
    npages = page_table.shape[1]
    past = npages * PAGE_SIZE

    o_z, o_xbc, o_dt = 0, D_INNER, D_INNER + CONV_DIM
    o_q = o_dt + N_SSM_HEADS
    o_kv = o_q + ATT_WIDTH
    o_ag = o_kv + 6 * 256
    o_mg = o_ag + 3 * N_ATT_HEADS
    w_r = jnp.concatenate([
        w_in[:, o_z:o_z + 2048], w_in[:, o_xbc:o_xbc + 2048], w_in[:, o_q:o_q + 1024], w_in[:, o_mg:o_mg + 2048],
        w_in[:, o_xbc + 2048:o_xbc + 3072], w_in[:, o_kv:o_kv + 1536], w_in[:, o_dt:o_dt + 32],
        w_in[:, o_ag:o_ag + 48], jnp.zeros((D_MODEL, 48), F32)], axis=1).astype(BF16)
    ada_w_bf = ada_w.astype(BF16)
    wss = w_ssm_out.astype(BF16)
    wat = w_att_out.astype(BF16)
    wo = w_out.astype(BF16)
    wup = ffn_w_up.astype(BF16)
    wdown = ffn_w_down.astype(BF16)
    row = lambda v: v.reshape(1, -1)
    pad128 = lambda v: jnp.pad(v, (0, 128 - v.shape[0])).reshape(1, 128)
    dvec = jnp.repeat(ssm_D, SSM_HEAD_DIM).reshape(1, D_INNER)
    cmpw_k = _cmp_weights(cmp_pe_k, cmp_w1_k, cmp_w2_k)
    cmpw_v = _cmp_weights(cmp_pe_v, cmp_w1_v, cmp_w2_v)
    wbd, w1f, pe2, w2bd = (jnp.stack([a, b]) for a, b in zip(cmpw_k, cmpw_v))

    npad = (-(bp + nb)) % 8
    c_all = jnp.concatenate([c_prompt, c_sample, jnp.zeros((npad, D_MODEL), F32)], axis=0)
    mod = _ada_mod(c_all, ada_w_bf, row(ada_b))
    mod_p = mod[:bp].reshape(bp, 1, 6 * D_MODEL)
    mod_s = mod[bp:bp + nb].reshape(1, nb, 6 * D_MODEL)

    proj_p = _inproj(x_prompt, mod_p, row(norm1_w), w_r, 1024)
    yssm_p, conv_p, hlast_p = _ssd_prompt(proj_p, ssm_conv_w, row(ssm_conv_b), pad128(ssm_dt_bias),
                                          pad128(ssm_A_log), dvec, row(ssm_norm_w))
    tabs_p = _rope_tables(jnp.arange(tp, dtype=jnp.int32))
    q_p, kvrows_p, ksa_p, khm_p, gates_p, win_p = _rope_prep(proj_p, *tabs_p, 512)
    kc_p, vc_p = _compress_prompt(kvrows_p, wbd, w1f, pe2, w2bd)
    nblk = -(-tp // SEL_LEN)
    ncmp = tp // CMP_STRIDE
    ovt = np.zeros((nblk, ncmp), np.float32)
    ovt[:, :ncmp - 1] = _overlap(ncmp - 1, nblk).T
    yatt_p = _nsa_prompt(q_p, gates_p, kc_p, vc_p, ksa_p, khm_p, jnp.asarray(ovt))
    x1_p, h2_p = _merge_out(x_prompt, yssm_p, yatt_p, proj_p, mod_p, row(norm2_w), wss, wat, wo, 512)
    act_p, fa_p, fb_p = _ffn_up_prompt(h2_p, wup, ffn_conv_w, row(ffn_conv_b), 512)
    y_p = _ffn_down(act_p, x1_p, mod_p, row(final_norm_w), wdown, 512)

    x_s3 = x_sample.reshape(1, nb, D_MODEL)
    proj_s = _inproj(x_s3, mod_s, row(norm1_w), w_r, nb)
    eh = (np.arange(D_INNER)[None, :] // SSM_HEAD_DIM == np.arange(128)[:, None]).astype(np.float32)
    xdt, dec_e, yd, bm_c, cm_c, xdt_t, dec_t = _ssm_pre(
        proj_s, state_ssm_conv[:, 0], state_ssm_conv[:, 1], state_ssm_conv[:, 2], ssm_conv_w, row(ssm_conv_b),
        pad128(ssm_dt_bias), pad128(ssm_A_log), dvec, jnp.asarray(eh))
    h_new, yoff = _ssm_state(state_ssm.reshape(nb, D_INNER, SSM_STATE), xdt_t, dec_t, bm_c, cm_c)
    yssm_s = _ssm_post(yd, yoff[:, 0, :], dec_e, proj_s, row(ssm_norm_w))
    tabs_s = _rope_tables(jnp.full((nb,), past, dtype=jnp.int32))
    q_s, kvrows_s, _, _, _, wnew_s = _rope_prep(proj_s, *tabs_s, nb)
    cache3 = cache_nsa_kv.reshape(cache_nsa_kv.shape[0], PAGE_SIZE, 1024)
    kc_s, vc_s = _compress_sample(page_table, cache3, wbd, w1f, pe2, w2bd)
    ncmp_s = past // CMP_STRIDE
    nsel_s = past // SEL_LEN + 1
    ov_s = np.zeros((ncmp_s, 128), np.float32)
    ov_s[:ncmp_s - 1, :nsel_s] = _overlap(ncmp_s - 1, nsel_s)
    e_s = (np.arange(past)[None, :] // SEL_LEN == np.arange(128)[:, None]).astype(np.float32)
    att_g_s = proj_s[0, :, COL_TAIL + 32:COL_TAIL + 80].reshape(nb, N_ATT_HEADS, 3)
    yatt_s16, win_s = _nsa_sample(
        page_table, cache3, q_s.reshape(nb, N_ATT_HEADS, ATT_HEAD_DIM), att_g_s, kvrows_s.reshape(nb, 1, 1024),
        wnew_s.reshape(nb, 1, 512), kc_s, vc_s, cache_win_kv.reshape(nb, WINDOW, 512), jnp.asarray(ov_s),
        jnp.asarray(e_s, dtype=BF16))
    x1_s, h2_s = _merge_out(x_s3, yssm_s.reshape(1, nb, D_INNER), yatt_s16.reshape(1, nb, ATT_WIDTH), proj_s, mod_s,
                            row(norm2_w), wss, wat, wo, nb)
    act_s, ua_s, ub_s = _ffn_up_sample(h2_s.reshape(nb, D_MODEL), wup, state_ffn_conv[:, 0], state_ffn_conv[:, 1],
                                       ffn_conv_w, row(ffn_conv_b))
    y_s = _ffn_down(act_s.reshape(1, nb, D_FF), x1_s, mod_s, row(final_norm_w), wdown, nb)

    xbc_s = jnp.concatenate([proj_s[0, :, COL_XS:COL_XS + 2048], proj_s[0, :, COL_BM:COL_BM + 1024]], axis=1)
    conv_s = jnp.stack([state_ssm_conv[:, 1], state_ssm_conv[:, 2], xbc_s], axis=1)
    ffn_s = jnp.stack([state_ffn_conv[:, 1], jnp.concatenate([ua_s, ub_s], axis=1)], axis=1)
    return (y_p,
            y_s.reshape(nb, 1, D_MODEL),
            kvrows_p.reshape(bp, tp, 4, N_KV_HEADS, ATT_HEAD_DIM),
            kvrows_s.reshape(nb, 1, 4, N_KV_HEADS, ATT_HEAD_DIM),
            win_p.reshape(bp, WINDOW, 2, N_KV_HEADS, ATT_HEAD_DIM),
            win_s.reshape(nb, WINDOW, 2, N_KV_HEADS, ATT_HEAD_DIM),
            hlast_p.reshape(bp, N_SSM_HEADS, SSM_HEAD_DIM, SSM_STATE),
            h_new.reshape(nb, N_SSM_HEADS, SSM_HEAD_DIM, SSM_STATE),
            conv_p,
            conv_s,
            jnp.concatenate([fa_p[:, -1], fb_p[:, -1]], axis=2),
            ffn_s)
```

```python
import functools

import numpy as np
import jax
import jax.numpy as jnp
from jax import lax
from jax.experimental import pallas as pl
from jax.experimental.pallas import tpu as pltpu

F32 = jnp.float32
BF16 = jnp.bfloat16
HIGHEST = lax.Precision.HIGHEST

D_MODEL = 1024
D_INNER = 2048
N_SSM_HEADS = 32
SSM_HEAD_DIM = 64
SSM_STATE = 128
SSM_GROUPS = 4
SSM_CONV = 4
CONV_DIM = 3072
SSD_CHUNK = 128
N_ATT_HEADS = 16
ATT_HEAD_DIM = 64
N_KV_HEADS = 4
Q_PER_KV = 4
ATT_WIDTH = 1024
ATT_SCALE = ATT_HEAD_DIM ** -0.5
LOG2E = 1.4426950408889634
ROPE_DIM = 16
ROPE_THETA = 500000.0
CMP_LEN = 32
CMP_STRIDE = 16
CMP_HIDDEN = 128
SEL_LEN = 64
SEL_TOPK = 16
WINDOW = 512
Q_BLOCK = 128
PAGE_SIZE = 128
D_FF = 2816
FFN_CONV = 3
EPS = 1e-6
NEG = -1e30
BIG = 1e30
TINY = 1e-30

COL_Z = 0
COL_XS = 2048
COL_Q = 4096
COL_GM = 5120
COL_BM = 7168
COL_CM = 7680
COL_KV = 8192
COL_TAIL = 9728
N_PROJ = 9856
PROJ_TN = 896
VMEM_LIMIT = 56 * 1024 * 1024


def _sig(x):
    return 1.0 / (1.0 + jnp.exp(-x))


def _silu(x):
    return x * _sig(x)


def _softplus(x):
    return jnp.maximum(x, 0.0) + jnp.log1p(jnp.exp(-jnp.abs(x)))


def _dot(a, b):
    return jnp.dot(a, b, preferred_element_type=F32)


def _dot_hi(a, b):
    return jnp.dot(a, b, preferred_element_type=F32, precision=HIGHEST)


def _dot_nt(a, b):
    return lax.dot_general(a, b, (((1,), (1,)), ((), ())), preferred_element_type=F32)


def _dot_nt_hi(a, b):
    return lax.dot_general(a, b, (((1,), (1,)), ((), ())), preferred_element_type=F32, precision=HIGHEST)


def _params(sem):
    return pltpu.CompilerParams(dimension_semantics=sem, vmem_limit_bytes=VMEM_LIMIT)


def _ada_kernel(c_ref, w_ref, b_ref, o_ref):
    c = c_ref[...]
    o_ref[...] = _dot(_silu(c).astype(BF16), w_ref[...]) + b_ref[...]


def _ada_mod(c_all, w_bf, b_row):
    m = c_all.shape[0]
    tn = 512
    return pl.pallas_call(
        _ada_kernel,
        grid=(w_bf.shape[1] // tn,),
        in_specs=[pl.BlockSpec((m, D_MODEL), lambda j: (0, 0)),
                  pl.BlockSpec((D_MODEL, tn), lambda j: (0, j)),
                  pl.BlockSpec((1, tn), lambda j: (0, j))],
        out_specs=pl.BlockSpec((m, tn), lambda j: (0, j)),
        out_shape=jax.ShapeDtypeStruct((m, w_bf.shape[1]), F32),
        compiler_params=_params(("arbitrary",)),
        name="ada_mod",
    )(c_all, w_bf, b_row)


def _inproj_kernel(x_ref, sh_ref, sc_ref, nw_ref, w_ref, o_ref, h_scr):
    @pl.when(pl.program_id(2) == 0)
    def _():
        x = x_ref[...]
        y = x * lax.rsqrt(jnp.mean(x * x, axis=-1, keepdims=True) + EPS)
        h = y * nw_ref[...] * (1.0 + sc_ref[...]) + sh_ref[...]
        h_scr[...] = h.astype(BF16)

    o_ref[...] = _dot(h_scr[...], w_ref[...])


def _inproj(x3, mod3, nw_row, w_bf, tm):
    bx, tx, _ = x3.shape
    r = mod3.shape[1]
    mod_b = (lambda b: b) if mod3.shape[0] == bx else (lambda b: 0)
    return pl.pallas_call(
        _inproj_kernel,
        grid=(bx, tx // tm, N_PROJ // PROJ_TN),
        in_specs=[pl.BlockSpec((None, tm, D_MODEL), lambda b, i, j: (b, i, 0)),
                  pl.BlockSpec((None, r, D_MODEL), lambda b, i, j: (mod_b(b), 0, 0)),
                  pl.BlockSpec((None, r, D_MODEL), lambda b, i, j: (mod_b(b), 0, 1)),
                  pl.BlockSpec((1, D_MODEL), lambda b, i, j: (0, 0)),
                  pl.BlockSpec((D_MODEL, PROJ_TN), lambda b, i, j: (0, j))],
        out_specs=pl.BlockSpec((None, tm, PROJ_TN), lambda b, i, j: (b, i, j)),
        out_shape=jax.ShapeDtypeStruct((bx, tx, N_PROJ), F32),
        scratch_shapes=[pltpu.VMEM((tm, D_MODEL), BF16)],
        compiler_params=_params(("arbitrary", "arbitrary", "arbitrary")),
        name="inproj",
    )(x3, mod3, mod3, nw_row, w_bf)


def _ssd_kernel(z_ref, xs_ref, bm_ref, cm_ref, dt_ref, cw_ref, cb_ref, dtb_ref, alog_ref, dvec_ref, nw_ref,
                y_ref, conv_ref, hout_ref, xpad, h_t, *, nchunks):
    c = pl.program_id(1)
    ln = SSD_CHUNK

    @pl.when(c == 0)
    def _():
        xpad[0:8, :] = jnp.zeros((8, CONV_DIM), F32)
        h_t[...] = jnp.zeros_like(h_t)

    xpad[8:8 + ln, 0:2048] = xs_ref[...]
    xpad[8:8 + ln, 2048:2560] = bm_ref[...]
    xpad[8:8 + ln, 2560:3072] = cm_ref[...]
    cw = cw_ref[...]
    conv = (xpad[5:5 + ln, :] * cw[0:1, :] + xpad[6:6 + ln, :] * cw[1:2, :] + xpad[7:7 + ln, :] * cw[2:3, :]
            + xpad[8:8 + ln, :] * cw[3:4, :] + cb_ref[...])
    xc = _silu(conv)
    xs_c = xc[:, 0:2048]
    bm_c = xc[:, 2048:2560]
    cm_c = xc[:, 2560:3072]

    dt = _softplus(dt_ref[...] + dtb_ref[...])
    a = -jnp.exp(alog_ref[...])
    row = lax.broadcasted_iota(jnp.int32, (ln, ln), 0)
    col = lax.broadcasted_iota(jnp.int32, (ln, ln), 1)
    causal = row >= col
    cs = _dot_hi(causal.astype(F32), dt * a)
    cs_t = cs.T
    dt_t = dt.T

    x_bf = xs_c.astype(BF16)
    bm_bf = bm_c.astype(BF16)
    cm_bf = cm_c.astype(BF16)
    lo = lax.broadcasted_iota(jnp.int32, (1, 128), 1) < 64
    ys = []
    for g in range(SSM_GROUPS):
        bg = bm_bf[:, g * 128:(g + 1) * 128]
        cg = cm_bf[:, g * 128:(g + 1) * 128]
        cb = _dot_nt(cg, bg)
        b_t = bm_c[:, g * 128:(g + 1) * 128].T
        for r2 in range(4):
            pair = g * 4 + r2
            xp = x_bf[:, pair * 128:(pair + 1) * 128]
            yd, st = [], []
            for h in (2 * pair, 2 * pair + 1):
                cs_col = cs[:, h:h + 1]
                cs_row = cs_t[h:h + 1, :]
                dt_row = dt_t[h:h + 1, :]
                lmat = jnp.where(causal, jnp.exp(cs_col - cs_row), 0.0)
                yd.append(_dot((cb * lmat * dt_row).astype(BF16), xp))
                w_row = dt_row * jnp.exp(cs_t[h:h + 1, ln - 1:ln] - cs_row)
                st.append(_dot((b_t * w_row).astype(BF16), xp))
            ha, hb = 2 * pair, 2 * pair + 1
            ecol = jnp.where(lo, jnp.exp(cs[:, ha:ha + 1]), jnp.exp(cs[:, hb:hb + 1]))
            hprev = h_t[pair]
            yoff = _dot(cg, hprev.astype(BF16)) * ecol
            ys.append(jnp.where(lo, yd[0], yd[1]) + yoff)
            edec = jnp.where(lo, jnp.exp(cs_t[ha:ha + 1, ln - 1:ln]), jnp.exp(cs_t[hb:hb + 1, ln - 1:ln]))
            h_t[pair] = hprev * edec + jnp.where(lo, st[0], st[1])
    y = jnp.concatenate(ys, axis=1) + xs_c * dvec_ref[...]
    yz = y * _silu(z_ref[...])
    ms = jnp.mean(yz * yz, axis=-1, keepdims=True)
    y_ref[...] = (yz * lax.rsqrt(ms + EPS) * nw_ref[...]).astype(BF16)

    xpad[0:8, :] = xpad[ln:ln + 8, :]

    @pl.when(c == nchunks - 1)
    def _():
        conv_ref[...] = xpad[ln + 5:ln + 8, :]
        for pair in range(16):
            hout_ref[pair * 128:(pair + 1) * 128, :] = h_t[pair].T


def _ssd_prompt(proj3, cw, cb_row, dtb_row, alog_row, dvec_row, nw_row):
    bx, tx, _ = proj3.shape
    nchunks = tx // SSD_CHUNK
    ln = SSD_CHUNK
    const = lambda shape: pl.BlockSpec(shape, lambda b, c: (0, 0))
    return pl.pallas_call(
        functools.partial(_ssd_kernel, nchunks=nchunks),
        grid=(bx, nchunks),
        in_specs=[pl.BlockSpec((None, ln, 2048), lambda b, c: (b, c, COL_Z // 2048)),
                  pl.BlockSpec((None, ln, 2048), lambda b, c: (b, c, COL_XS // 2048)),
                  pl.BlockSpec((None, ln, 512), lambda b, c: (b, c, COL_BM // 512)),
                  pl.BlockSpec((None, ln, 512), lambda b, c: (b, c, COL_CM // 512)),
                  pl.BlockSpec((None, ln, 128), lambda b, c: (b, c, COL_TAIL // 128)),
                  const((SSM_CONV, CONV_DIM)), const((1, CONV_DIM)), const((1, 128)), const((1, 128)),
                  const((1, D_INNER)), const((1, D_INNER))],
        out_specs=[pl.BlockSpec((None, ln, D_INNER), lambda b, c: (b, c, 0)),
                   pl.BlockSpec((None, SSM_CONV - 1, CONV_DIM), lambda b, c: (b, 0, 0)),
                   pl.BlockSpec((None, N_SSM_HEADS * SSM_HEAD_DIM, SSM_STATE), lambda b, c: (b, 0, 0))],
        out_shape=[jax.ShapeDtypeStruct((bx, tx, D_INNER), BF16),
                   jax.ShapeDtypeStruct((bx, SSM_CONV - 1, CONV_DIM), F32),
                   jax.ShapeDtypeStruct((bx, N_SSM_HEADS * SSM_HEAD_DIM, SSM_STATE), F32)],
        scratch_shapes=[pltpu.VMEM((ln + 8, CONV_DIM), F32), pltpu.VMEM((16, 128, 128), F32)],
        compiler_params=_params(("arbitrary", "arbitrary")),
        name="ssd_prompt",
    )(proj3, proj3, proj3, proj3, proj3, cw, cb_row, dtb_row, alog_row, dvec_row, nw_row)


def _ssm_pre_kernel(xs_ref, bm_ref, cm_ref, dt_ref, s0_ref, s1_ref, s2_ref, cw_ref, cb_ref, dtb_ref, alog_ref,
                    dvec_ref, eh_ref, xdt_ref, dec_ref, yd_ref, bmc_ref, cmc_ref, xdt_t_ref, dec_t_ref):
    cw = cw_ref[...]
    xbc = jnp.concatenate([xs_ref[...], bm_ref[...], cm_ref[...]], axis=1)
    conv = s0_ref[...] * cw[0:1, :] + s1_ref[...] * cw[1:2, :] + s2_ref[...] * cw[2:3, :] + xbc * cw[3:4, :] + cb_ref[...]
    xc = _silu(conv)
    xs_c = xc[:, 0:2048]
    bm_c = xc[:, 2048:2560]
    cm_c = xc[:, 2560:3072]
    dt = _softplus(dt_ref[...] + dtb_ref[...])
    a = -jnp.exp(alog_ref[...])
    dec = jnp.exp(dt * a)
    eh = eh_ref[...]
    dt_e = _dot_hi(dt, eh)
    dec_e = _dot_hi(dec, eh)
    xdt = xs_c * dt_e
    cbs = []
    for g in range(SSM_GROUPS):
        cbg = jnp.sum(cm_c[:, g * 128:(g + 1) * 128] * bm_c[:, g * 128:(g + 1) * 128], axis=-1, keepdims=True)
        cbs.append(jnp.broadcast_to(cbg, (cbg.shape[0], 512)))
    cb_e = jnp.concatenate(cbs, axis=1)
    xdt_ref[...] = xdt
    dec_ref[...] = dec_e
    yd_ref[...] = cb_e * xdt + xs_c * dvec_ref[...]
    bmc_ref[...] = bm_c
    cmc_ref[...] = cm_c
    for k in range(16):
        xdt_t_ref[k * 128:(k + 1) * 128, :] = xdt[:, k * 128:(k + 1) * 128].T
        dec_t_ref[k * 128:(k + 1) * 128, :] = dec_e[:, k * 128:(k + 1) * 128].T


def _ssm_pre(proj3, s0, s1, s2, cw, cb_row, dtb_row, alog_row, dvec_row, eh):
    nb = proj3.shape[1]
    const = lambda shape: pl.BlockSpec(shape, lambda i: (0,) * len(shape))
    return pl.pallas_call(
        _ssm_pre_kernel,
        grid=(1,),
        in_specs=[pl.BlockSpec((None, nb, 2048), lambda i: (0, 0, COL_XS // 2048)),
                  pl.BlockSpec((None, nb, 512), lambda i: (0, 0, COL_BM // 512)),
                  pl.BlockSpec((None, nb, 512), lambda i: (0, 0, COL_CM // 512)),
                  pl.BlockSpec((None, nb, 128), lambda i: (0, 0, COL_TAIL // 128)),
                  const((nb, CONV_DIM)), const((nb, CONV_DIM)), const((nb, CONV_DIM)),
                  const((SSM_CONV, CONV_DIM)), const((1, CONV_DIM)), const((1, 128)), const((1, 128)),
                  const((1, D_INNER)), const((128, D_INNER))],
        out_specs=[const((nb, D_INNER)), const((nb, D_INNER)), const((nb, D_INNER)), const((nb, 512)),
                   const((nb, 512)), const((D_INNER, nb)), const((D_INNER, nb))],
        out_shape=[jax.ShapeDtypeStruct((nb, D_INNER), F32)] * 3 + [jax.ShapeDtypeStruct((nb, 512), F32)] * 2
        + [jax.ShapeDtypeStruct((D_INNER, nb), F32)] * 2,
        compiler_params=_params(("arbitrary",)),
        name="ssm_pre",
    )(proj3, proj3, proj3, proj3, s0, s1, s2, cw, cb_row, dtb_row, alog_row, dvec_row, eh)


def _ssm_state_kernel(h0_ref, xdt_t_ref, dec_t_ref, bm_ref, cm_ref, hn_ref, yoff_ref):
    b = pl.program_id(0)
    h0 = h0_ref[...]
    onehot = (lax.broadcasted_iota(jnp.int32, (128, 128), 0) == b).astype(F32)
    dec_b = _dot_hi(dec_t_ref[...], onehot)
    outs, yoffs = [], []
    for g in range(SSM_GROUPS):
        bc_row = bm_ref[:, g * 128:(g + 1) * 128]
        outs.append(_dot_hi(xdt_t_ref[g * 512:(g + 1) * 512, :], onehot * bc_row))
        cm_row = cm_ref[:, g * 128:(g + 1) * 128]
        cm8 = jnp.broadcast_to(cm_row, (8, 128)).astype(BF16)
        yoffs.append(_dot_nt(cm8, h0[g * 512:(g + 1) * 512, :].astype(BF16)))
    hn_ref[...] = h0 * dec_b + jnp.concatenate(outs, axis=0)
    yoff_ref[...] = jnp.concatenate(yoffs, axis=1)


def _ssm_state(h0, xdt_t, dec_t, bm_c, cm_c):
    nb = h0.shape[0]
    const = lambda shape: pl.BlockSpec(shape, lambda b: (0, 0))
    return pl.pallas_call(
        _ssm_state_kernel,
        grid=(nb,),
        in_specs=[pl.BlockSpec((None, D_INNER, SSM_STATE), lambda b: (b, 0, 0)),
                  const((D_INNER, nb)), const((D_INNER, nb)),
                  pl.BlockSpec((None, 1, 512), lambda b: (b, 0, 0)), pl.BlockSpec((None, 1, 512), lambda b: (b, 0, 0))],
        out_specs=[pl.BlockSpec((None, D_INNER, SSM_STATE), lambda b: (b, 0, 0)),
                   pl.BlockSpec((None, 8, D_INNER), lambda b: (b, 0, 0))],
        out_shape=[jax.ShapeDtypeStruct((nb, D_INNER, SSM_STATE), F32),
                   jax.ShapeDtypeStruct((nb, 8, D_INNER), F32)],
        compiler_params=_params(("arbitrary",)),
        name="ssm_state",
    )(h0, xdt_t, dec_t, bm_c.reshape(nb, 1, 512), cm_c.reshape(nb, 1, 512))


def _ssm_post_kernel(yd_ref, yoff_ref, dec_ref, z_ref, nw_ref, y_ref):
    y = yd_ref[...] + yoff_ref[...] * dec_ref[...]
    yz = y * _silu(z_ref[...])
    ms = jnp.mean(yz * yz, axis=-1, keepdims=True)
    y_ref[...] = (yz * lax.rsqrt(ms + EPS) * nw_ref[...]).astype(BF16)


def _ssm_post(yd, yoff, dec_e, proj3, nw_row):
    nb = yd.shape[0]
    const = lambda shape: pl.BlockSpec(shape, lambda i: (0, 0))
    return pl.pallas_call(
        _ssm_post_kernel,
        grid=(1,),
        in_specs=[const((nb, D_INNER)), const((nb, D_INNER)), const((nb, D_INNER)),
                  pl.BlockSpec((None, nb, 2048), lambda i: (0, 0, COL_Z // 2048)), const((1, D_INNER))],
        out_specs=const((nb, D_INNER)),
        out_shape=jax.ShapeDtypeStruct((nb, D_INNER), BF16),
        compiler_params=_params(("arbitrary",)),
        name="ssm_post",
    )(yd, yoff, dec_e, proj3, nw_row)


def _rope128(x, c, s1, s2):
    return x * c + pltpu.roll(x, 120, axis=1) * s1 + pltpu.roll(x, 8, axis=1) * s2


def _rope_kernel(q_ref, k01_ref, k23_ref, k45_ref, tail_ref, c_ref, s1_ref, s2_ref,
                 qo_ref, kvrows_ref, ksa_ref, khm_ref, gates_ref, win_ref, *, ntiles, tr):
    c = c_ref[...]
    s1 = s1_ref[...]
    s2 = s2_ref[...]
    q = q_ref[...]
    qo_ref[...] = (jnp.concatenate(
        [_rope128(q[:, k * 128:(k + 1) * 128], c, s1, s2) for k in range(8)], axis=1
    ) * (ATT_SCALE * LOG2E)).astype(BF16)
    streams = []
    for pref in (k01_ref, k23_ref, k45_ref):
        blk = pref[...]
        kk = jnp.concatenate([_rope128(blk[:, k * 128:(k + 1) * 128], c, s1, s2) for k in range(2)], axis=1)
        streams.append(kk)
        streams.append(blk[:, 256:512])
    kvrows_ref[...] = jnp.concatenate(streams[0:4], axis=1)
    pos = pl.program_id(1) * tr + lax.broadcasted_iota(jnp.int32, (tr, 64), 0)
    own = lax.broadcasted_iota(jnp.int32, (tr, 64), 1) == lax.shift_right_logical(pos, 6)
    extra = jnp.where(own, NEG, 0.0).astype(BF16)
    ksel = streams[2].astype(BF16)
    for h in range(N_KV_HEADS):
        ksa_ref[h] = jnp.concatenate([ksel[:, h * 64:(h + 1) * 64], extra], axis=1)
    for i, s in enumerate((3, 4, 5)):
        sb = streams[s].astype(BF16)
        for h in range(N_KV_HEADS):
            khm_ref[i * 4 + h] = sb[:, h * 64:(h + 1) * 64]
    g = _sig(tail_ref[...])
    for hk in range(N_KV_HEADS):
        gates_ref[hk] = g[:, 32 + hk * 12:32 + (hk + 1) * 12]

    @pl.when(pl.program_id(1) == ntiles - 1)
    def _():
        win_ref[...] = jnp.concatenate(streams[4:6], axis=1)


def _rope_prep(proj3, ctab, s1tab, s2tab, tr):
    bx, tx, _ = proj3.shape
    ntiles = tx // tr
    tab = pl.BlockSpec((tr, 128), lambda b, i: (i, 0))
    return pl.pallas_call(
        functools.partial(_rope_kernel, ntiles=ntiles, tr=tr),
        grid=(bx, ntiles),
        in_specs=[pl.BlockSpec((None, tr, 1024), lambda b, i: (b, i, COL_Q // 1024)),
                  pl.BlockSpec((None, tr, 512), lambda b, i: (b, i, COL_KV // 512)),
                  pl.BlockSpec((None, tr, 512), lambda b, i: (b, i, COL_KV // 512 + 1)),
                  pl.BlockSpec((None, tr, 512), lambda b, i: (b, i, COL_KV // 512 + 2)),
                  pl.BlockSpec((None, tr, 128), lambda b, i: (b, i, COL_TAIL // 128)),
                  tab, tab, tab],
        out_specs=[pl.BlockSpec((None, tr, 1024), lambda b, i: (b, i, 0)),
                   pl.BlockSpec((None, tr, 1024), lambda b, i: (b, i, 0)),
                   pl.BlockSpec((None, 4, tr, 128), lambda b, i: (b, 0, i, 0)),
                   pl.BlockSpec((None, 12, tr, 64), lambda b, i: (b, 0, i, 0)),
                   pl.BlockSpec((None, 4, tr, 12), lambda b, i: (b, 0, i, 0)),
                   pl.BlockSpec((None, tr, 512), lambda b, i: (b, 0, 0))],
        out_shape=[jax.ShapeDtypeStruct((bx, tx, 1024), BF16),
                   jax.ShapeDtypeStruct((bx, tx, 1024), F32),
                   jax.ShapeDtypeStruct((bx, 4, tx, 128), BF16),
                   jax.ShapeDtypeStruct((bx, 12, tx, 64), BF16),
                   jax.ShapeDtypeStruct((bx, 4, tx, 12), F32),
                   jax.ShapeDtypeStruct((bx, tr, 512), F32)],
        compiler_params=_params(("arbitrary", "arbitrary")),
        name="rope_prep",
    )(proj3, proj3, proj3, proj3, proj3, ctab, s1tab, s2tab)


def _compress_kernel(*refs, nrefs, npages, head_major):
    if nrefs > 1:
        refs = refs[1:]
    page_refs = refs[:nrefs]
    w1ab_ref, w1f_ref, pe_ref, w2_ref, kc_ref, vc_ref, shift = refs[nrefs:]
    nsub = PAGE_SIZE // CMP_STRIDE
    nj = npages * nsub
    nr = N_KV_HEADS * nj
    shift[nr:nr + 8, :] = jnp.zeros((8, CMP_HIDDEN), F32)
    ri = lax.broadcasted_iota(jnp.int32, (PAGE_SIZE, PAGE_SIZE), 0)
    ci = lax.broadcasted_iota(jnp.int32, (PAGE_SIZE, PAGE_SIZE), 1)
    perm = (ci == (ri & (nsub - 1)) * CMP_STRIDE + lax.shift_right_logical(ri, 3)).astype(BF16)
    if nrefs == 1:
        pages = [page_refs[0][p * PAGE_SIZE:(p + 1) * PAGE_SIZE, :] for p in range(npages)]
        grouped = [_dot(perm, pg.astype(BF16)) for pg in pages]
    else:
        grouped = [jnp.concatenate(
            [_dot_nt(perm, pr[st].reshape(N_KV_HEADS * ATT_HEAD_DIM, PAGE_SIZE).astype(BF16)) for st in range(2)],
            axis=1) for pr in page_refs]
    for st, o_ref in ((0, kc_ref), (1, vc_ref)):
        heads = []
        for h in range(N_KV_HEADS):
            c0 = st * 256 + h * ATT_HEAD_DIM
            rows = [jnp.concatenate([y[s * nsub:(s + 1) * nsub, c0:c0 + ATT_HEAD_DIM] for s in range(CMP_STRIDE)],
                                    axis=1) for y in grouped]
            heads.append(rows[0] if npages == 1 else jnp.concatenate(rows, axis=0))
        a = jnp.concatenate(heads, axis=0).astype(BF16)
        p2 = _dot(a, w1ab_ref[st])
        shift[0:nr, :] = p2[:, CMP_HIDDEN:2 * CMP_HIDDEN]
        pe8 = jnp.broadcast_to(pe_ref[st], (8, CMP_LEN * ATT_HEAD_DIM)).astype(BF16)
        pe_t = _dot(pe8, w1f_ref[st])[0:1, :]
        hid = p2[:, 0:CMP_HIDDEN] + shift[pl.ds(1, nr), :] + pe_t
        out = _dot(_silu(hid).astype(BF16), w2_ref[st]).astype(BF16)
        if head_major:
            for h in range(N_KV_HEADS):
                o_ref[h] = out[h * nj:(h + 1) * nj, :]
        else:
            o_ref[...] = jnp.concatenate([out[h * nj:(h + 1) * nj, :] for h in range(N_KV_HEADS)], axis=1)


def _compress_prompt(kv_rows, wbd, w1f, pe, w2bd):
    bx, tx, _ = kv_rows.shape
    nj = tx // CMP_STRIDE
    c4 = lambda shape: pl.BlockSpec(shape, lambda b: (0,) * len(shape))
    return pl.pallas_call(
        functools.partial(_compress_kernel, nrefs=1, npages=tx // PAGE_SIZE, head_major=True),
        grid=(bx,),
        in_specs=[pl.BlockSpec((None, tx, 512), lambda b: (b, 0, 0)),
                  c4(wbd.shape), c4(w1f.shape), c4(pe.shape), c4(w2bd.shape)],
        out_specs=[pl.BlockSpec((None, 4, nj, 64), lambda b: (b, 0, 0, 0))] * 2,
        out_shape=[jax.ShapeDtypeStruct((bx, 4, nj, 64), BF16)] * 2,
        scratch_shapes=[pltpu.VMEM((N_KV_HEADS * nj + 8, CMP_HIDDEN), F32)],
        compiler_params=_params(("arbitrary",)),
        name="compress_prompt",
    )(kv_rows, wbd, w1f, pe, w2bd)


def _compress_sample(page_table, cache3, wbd, w1f, pe, w2bd):
    nb, npages = page_table.shape
    nsub = PAGE_SIZE // CMP_STRIDE
    nj = npages * nsub
    c4 = lambda shape: pl.BlockSpec(shape, lambda b, pt: (0,) * len(shape))
    page_specs = [pl.BlockSpec((None, 2, N_KV_HEADS, ATT_HEAD_DIM, PAGE_SIZE),
                               functools.partial(lambda b, pt, p: (pt[b, p], 0, 0, 0, 0), p=p))
                  for p in range(npages)]
    grid_spec = pltpu.PrefetchScalarGridSpec(
        num_scalar_prefetch=1,
        grid=(nb,),
        in_specs=page_specs + [c4(wbd.shape), c4(w1f.shape), c4(pe.shape), c4(w2bd.shape)],
        out_specs=[pl.BlockSpec((None, nj, 256), lambda b, pt: (b, 0, 0))] * 2,
        scratch_shapes=[pltpu.VMEM((N_KV_HEADS * nj + 8, CMP_HIDDEN), F32)],
    )
    return pl.pallas_call(
        functools.partial(_compress_kernel, nrefs=npages, npages=npages, head_major=False),
        grid_spec=grid_spec,
        out_shape=[jax.ShapeDtypeStruct((nb, nj, 256), BF16)] * 2,
        compiler_params=_params(("arbitrary",)),
        name="compress_sample",
    )(page_table, *([cache3] * npages), wbd, w1f, pe, w2bd)


KEY_BLOCK = 1024


def _nsa_prompt_kernel(q_ref, g_ref, kc_ref, vc_ref, ksa_ref, vs_ref, kw_ref, vw_ref, ovt_ref,
                       o_ref, s_scr, m_scr, l_scr, acc_scr):
    qb = pl.program_id(2)
    q0 = qb * Q_BLOCK
    q4 = q_ref[...]
    nq = Q_PER_KV * Q_BLOCK
    qs = jnp.concatenate([q4[:, r * 64:(r + 1) * 64] for r in range(Q_PER_KV)], axis=0)
    ncmp = kc_ref.shape[0]
    trow = q0 + (lax.broadcasted_iota(jnp.int32, (nq, 1), 0) & (Q_BLOCK - 1))

    jl = lax.broadcasted_iota(jnp.int32, (nq, ncmp), 1)
    mask_c = (jl * CMP_STRIDE + (CMP_LEN - 1)) <= trow
    s = jnp.where(mask_c, _dot_nt(qs, kc_ref[...]), NEG)
    e = jnp.where(mask_c, jnp.exp2(s - jnp.max(s, axis=-1, keepdims=True)), 0.0)
    p = e / jnp.maximum(jnp.sum(e, axis=-1, keepdims=True), TINY)
    o_c = _dot(p.astype(BF16), vc_ref[...])
    psum = p[0:128] + p[128:256] + p[256:384] + p[384:512]
    imp_t = _dot_nt_hi(ovt_ref[...], psum)
    nblk = imp_t.shape[0]
    tq = q0 + lax.broadcasted_iota(jnp.int32, (nblk, 128), 1)
    blk = lax.broadcasted_iota(jnp.int32, (nblk, 128), 0)
    cur = lax.shift_right_logical(tq, 6)
    valid = blk * SEL_LEN <= tq
    forced = (blk == 0) | (blk == cur) | (blk == cur - 1)
    imp = jnp.where(valid, jnp.where(forced, BIG, imp_t), -BIG)
    rank = jnp.zeros((nblk, 128), F32)
    for j in range(nblk):
        rj = imp[j:j + 1, :]
        beats = (rj > imp) | ((rj == imp) & (blk > j))
        rank = rank + jnp.where(beats, 1.0, 0.0)
    nsel_t = jnp.where(rank < float(SEL_TOPK), 0.0, 1.0)
    nsel_pad = jnp.concatenate([nsel_t, jnp.zeros((128 - nblk, 128), F32)], axis=0) if nblk < 128 else nsel_t
    nsel = nsel_pad.T[:, 0:64].astype(BF16)
    qaug = jnp.concatenate([qs, jnp.concatenate([nsel] * Q_PER_KV, axis=0)], axis=1)

    cw = 256
    nck = KEY_BLOCK // cw

    def block_scores(jb, causal):
        base = pl.multiple_of(jb * KEY_BLOCK, KEY_BLOCK)
        mx = None
        for c in range(nck):
            sc = _dot_nt(qaug, ksa_ref[pl.ds(base + c * cw, cw), :])
            if causal:
                kpos = base + c * cw + lax.broadcasted_iota(jnp.int32, (nq, cw), 1)
                sc = jnp.where(kpos <= trow, sc, NEG)
            s_scr[jb, :, c * cw:(c + 1) * cw] = sc
            for i in range(cw // 128):
                part = sc[:, i * 128:(i + 1) * 128]
                mx = part if mx is None else jnp.maximum(mx, part)
        m_scr[...] = jnp.maximum(m_scr[...], mx)

    nfull = lax.shift_right_logical(qb, (KEY_BLOCK // Q_BLOCK).bit_length() - 1)
    m_scr[...] = jnp.full((nq, 128), -jnp.inf, F32)

    def pass_a(jb, carry):
        block_scores(jb, False)
        return carry

    lax.fori_loop(0, nfull, pass_a, 0)
    block_scores(nfull, True)
    mb = jnp.broadcast_to(jnp.max(m_scr[...], axis=-1, keepdims=True), (nq, 128))

    l_scr[...] = jnp.zeros((nq, 128), F32)
    acc_scr[...] = jnp.zeros((nq, 64), F32)

    def pass_b(jb, carry):
        base = pl.multiple_of(jb * KEY_BLOCK, KEY_BLOCK)
        ps, lsum = [], None
        for c in range(KEY_BLOCK // 128):
            pc = jnp.exp2(s_scr[jb, :, c * 128:(c + 1) * 128] - mb)
            lsum = pc if lsum is None else lsum + pc
            ps.append(pc.astype(BF16))
        l_scr[...] = l_scr[...] + lsum
        acc_scr[...] = acc_scr[...] + _dot(jnp.concatenate(ps, axis=1), vs_ref[pl.ds(base, KEY_BLOCK), :])
        return carry

    lax.fori_loop(0, nfull + 1, pass_b, 0)
    o_s = acc_scr[...] / jnp.maximum(jnp.sum(l_scr[...], axis=-1, keepdims=True), TINY)

    wlen = WINDOW + Q_BLOCK
    wstart = pl.multiple_of(jnp.maximum(q0 - WINDOW, 0), 128)
    kw = kw_ref[pl.ds(wstart, wlen), :]
    vw = vw_ref[pl.ds(wstart, wlen), :]
    d = trow[0:Q_BLOCK] - (wstart + lax.broadcasted_iota(jnp.int32, (Q_BLOCK, wlen), 1))
    ok = (d >= 0) & (d < WINDOW)
    o_w = []
    for r in range(Q_PER_KV):
        sw = jnp.where(ok, _dot_nt(qs[r * Q_BLOCK:(r + 1) * Q_BLOCK], kw), NEG)
        ew = jnp.exp2(sw - jnp.max(sw, axis=-1, keepdims=True))
        o_w.append(_dot(ew.astype(BF16), vw) / jnp.maximum(jnp.sum(ew, axis=-1, keepdims=True), TINY))

    g = g_ref[...]
    outs = []
    for r in range(Q_PER_KV):
        rows = slice(r * Q_BLOCK, (r + 1) * Q_BLOCK)
        outs.append(g[:, 3 * r:3 * r + 1] * o_c[rows] + g[:, 3 * r + 1:3 * r + 2] * o_s[rows]
                    + g[:, 3 * r + 2:3 * r + 3] * o_w[r])
    o_ref[...] = jnp.concatenate(outs, axis=1).astype(BF16)


def _nsa_prompt(q_r, gates, kc, vc, ksa, khm, ovt):
    bx, tx, _ = q_r.shape
    ntiles = tx // Q_BLOCK
    ncmp = kc.shape[2]
    nq = Q_PER_KV * Q_BLOCK
    assert tx % KEY_BLOCK == 0 and tx >= WINDOW + Q_BLOCK and tx <= 64 * SEL_LEN
    kv_spec = lambda s: pl.BlockSpec((None, None, tx, 64), lambda b, h, i: (b, s * 4 + h, 0, 0))
    return pl.pallas_call(
        _nsa_prompt_kernel,
        grid=(bx, N_KV_HEADS, ntiles),
        in_specs=[pl.BlockSpec((None, Q_BLOCK, 256), lambda b, h, i: (b, i, h)),
                  pl.BlockSpec((None, None, Q_BLOCK, 12), lambda b, h, i: (b, h, i, 0)),
                  pl.BlockSpec((None, None, ncmp, 64), lambda b, h, i: (b, h, 0, 0)),
                  pl.BlockSpec((None, None, ncmp, 64), lambda b, h, i: (b, h, 0, 0)),
                  pl.BlockSpec((None, None, tx, 128), lambda b, h, i: (b, h, 0, 0)),
                  kv_spec(0), kv_spec(1), kv_spec(2),
                  pl.BlockSpec(ovt.shape, lambda b, h, i: (0, 0))],
        out_specs=pl.BlockSpec((None, Q_BLOCK, 256), lambda b, h, i: (b, i, h)),
        out_shape=jax.ShapeDtypeStruct((bx, tx, ATT_WIDTH), BF16),
        scratch_shapes=[pltpu.VMEM((tx // KEY_BLOCK, nq, KEY_BLOCK), F32), pltpu.VMEM((nq, 128), F32),
                        pltpu.VMEM((nq, 128), F32), pltpu.VMEM((nq, 64), F32)],
        compiler_params=_params(("arbitrary", "arbitrary", "arbitrary")),
        name="nsa_prompt",
    )(q_r, gates, kc, vc, ksa, khm, khm, khm, ovt)


def _fold_heads(o256, hmask):
    o = o256 * hmask
    return o[:, 0:64] + o[:, 64:128] + o[:, 128:192] + o[:, 192:256]


def _nsa_sample_kernel(*refs, npages):
    refs = refs[1:]
    pages = refs[:npages]
    (q_ref, g_ref, knew_ref, wnew_ref, kc_ref, vc_ref, win_ref, ov_ref, e_ref,
     o_ref, wout_ref) = refs[npages:]
    past = npages * PAGE_SIZE
    ncmp = kc_ref.shape[0]
    ntok = (past - CMP_LEN) // CMP_STRIDE + 1
    nsel = past // SEL_LEN + 1

    q16 = q_ref[...].astype(F32)
    rowh = lax.broadcasted_iota(jnp.int32, (16, 256), 0)
    laneh = lax.broadcasted_iota(jnp.int32, (16, 256), 1)
    hmask = (lax.shift_right_logical(rowh, 2) == lax.shift_right_logical(laneh, 6)).astype(F32)
    qbd_f = jnp.concatenate([q16] * 4, axis=1) * hmask
    qbd = qbd_f.astype(BF16)

    jl = lax.broadcasted_iota(jnp.int32, (16, ncmp), 1)
    mask_c = jl < ntok
    s = jnp.where(mask_c, _dot_nt(qbd, kc_ref[...]), NEG)
    e = jnp.where(mask_c, jnp.exp2(s - jnp.max(s, axis=-1, keepdims=True)), 0.0)
    p_c = e / jnp.maximum(jnp.sum(e, axis=-1, keepdims=True), TINY)
    o_c = _fold_heads(_dot(p_c.astype(BF16), vc_ref[...]), hmask)

    gm = (lax.shift_right_logical(lax.broadcasted_iota(jnp.int32, (8, 16), 1), 2)
          == lax.broadcasted_iota(jnp.int32, (8, 16), 0)).astype(F32)
    imp = _dot_hi(_dot_hi(gm, p_c), ov_ref[...])
    blk = lax.broadcasted_iota(jnp.int32, (8, 128), 1)
    exists = blk < nsel
    forced = (blk == 0) | (blk == nsel - 1) | (blk == nsel - 2)
    imp = jnp.where(exists, jnp.where(forced, BIG, imp), -BIG)
    rank = jnp.zeros((8, 128), F32)
    for j in range(nsel):
        cj = imp[:, j:j + 1]
        beats = (cj > imp) | ((cj == imp) & (blk > j))
        rank = rank + jnp.where(beats, 1.0, 0.0)
    sel8 = jnp.where((rank < float(min(SEL_TOPK, nsel))) & exists, 1.0, 0.0)
    gm_t = (lax.shift_right_logical(lax.broadcasted_iota(jnp.int32, (16, 8), 0), 2)
            == lax.broadcasted_iota(jnp.int32, (16, 8), 1)).astype(F32)
    sel16 = _dot(gm_t, sel8)
    bias_past = (_dot(sel16.astype(BF16), e_ref[...]) - 1.0) * BIG

    knew = knew_ref[...]
    s_new = jnp.sum(qbd_f * knew[:, 512:768], axis=-1, keepdims=True)
    hd = N_KV_HEADS * ATT_HEAD_DIM
    s_past = jnp.concatenate(
        [_dot(qbd, pg[0].reshape(hd, PAGE_SIZE).astype(BF16)) for pg in pages], axis=1) + bias_past
    m = jnp.maximum(jnp.max(s_past, axis=-1, keepdims=True), s_new)
    e_past = jnp.exp2(s_past - m)
    e_new = jnp.exp2(s_new - m)
    acc = e_new * knew[:, 768:1024]
    for p, pg in enumerate(pages):
        acc = acc + _dot_nt(e_past[:, p * 128:(p + 1) * 128].astype(BF16), pg[1].reshape(hd, PAGE_SIZE).astype(BF16))
    lsum = jnp.sum(e_past, axis=-1, keepdims=True) + e_new
    o_s = _fold_heads(acc / jnp.maximum(lsum, TINY), hmask)

    kw_t = win_ref[0].reshape(hd, WINDOW)
    vw_t = win_ref[1].reshape(hd, WINDOW)
    wnew = wnew_ref[...]
    wl = lax.broadcasted_iota(jnp.int32, (16, WINDOW), 1)
    s_w = jnp.where(wl >= 1, _dot(qbd, kw_t.astype(BF16)), NEG)
    s_wn = jnp.sum(qbd_f * wnew[:, 0:256], axis=-1, keepdims=True)
    mw = jnp.maximum(jnp.max(s_w, axis=-1, keepdims=True), s_wn)
    e_w = jnp.where(wl >= 1, jnp.exp2(s_w - mw), 0.0)
    e_wn = jnp.exp2(s_wn - mw)
    acc_w = _dot_nt(e_w.astype(BF16), vw_t.astype(BF16)) + e_wn * wnew[:, 256:512]
    o_w = _fold_heads(acc_w / jnp.maximum(jnp.sum(e_w, axis=-1, keepdims=True) + e_wn, TINY), hmask)

    g = _sig(g_ref[...])
    o_ref[...] = (g[:, 0:1] * o_c + g[:, 1:2] * o_s + g[:, 2:3] * o_w).astype(BF16)

    last = lax.broadcasted_iota(jnp.int32, (hd, 128), 1) == 127
    for i, src in enumerate((kw_t, vw_t)):
        col = jnp.broadcast_to(wnew[:, i * hd:(i + 1) * hd], (128, hd)).T
        rolled = pltpu.roll(src, WINDOW - 1, axis=1)
        out = jnp.concatenate([rolled[:, 0:WINDOW - 128], jnp.where(last, col, rolled[:, WINDOW - 128:])], axis=1)
        wout_ref[i] = out.reshape(N_KV_HEADS, ATT_HEAD_DIM, WINDOW)


def _nsa_sample(page_table, cache3, q16, g16, knew, wnew, kc, vc, win, ov, emat):
    nb, npages = page_table.shape
    ncmp = kc.shape[1]
    c2 = lambda shape: pl.BlockSpec(shape, lambda b, pt: (0,) * len(shape))
    page_specs = [pl.BlockSpec((None, 2, N_KV_HEADS, ATT_HEAD_DIM, PAGE_SIZE),
                               functools.partial(lambda b, pt, p: (pt[b, p], 1, 0, 0, 0), p=p))
                  for p in range(npages)]
    win_spec = pl.BlockSpec((None, 2, N_KV_HEADS, ATT_HEAD_DIM, WINDOW), lambda b, pt: (b, 0, 0, 0, 0))
    grid_spec = pltpu.PrefetchScalarGridSpec(
        num_scalar_prefetch=1,
        grid=(nb,),
        in_specs=page_specs + [
            pl.BlockSpec((None, 16, 64), lambda b, pt: (b, 0, 0)),
            pl.BlockSpec((None, 16, 3), lambda b, pt: (b, 0, 0)),
            pl.BlockSpec((None, 1, 1024), lambda b, pt: (b, 0, 0)),
            pl.BlockSpec((None, 1, 512), lambda b, pt: (b, 0, 0)),
            pl.BlockSpec((None, ncmp, 256), lambda b, pt: (b, 0, 0)),
            pl.BlockSpec((None, ncmp, 256), lambda b, pt: (b, 0, 0)),
            win_spec, c2(ov.shape), c2(emat.shape)],
        out_specs=[pl.BlockSpec((None, 16, 64), lambda b, pt: (b, 0, 0)), win_spec],
    )
    return pl.pallas_call(
        functools.partial(_nsa_sample_kernel, npages=npages),
        grid_spec=grid_spec,
        out_shape=[jax.ShapeDtypeStruct((nb, 16, 64), BF16),
                   jax.ShapeDtypeStruct((nb, 2, N_KV_HEADS, ATT_HEAD_DIM, WINDOW), F32)],
        compiler_params=_params(("arbitrary",)),
        name="nsa_sample",
    )(page_table, *([cache3] * npages), q16, g16, knew, wnew, kc, vc, win, ov, emat)


def _merge_kernel(x_ref, yssm_ref, yatt_ref, gs_ref, ga_ref, g1_ref, sh2_ref, sc2_ref, nw_ref,
                  wss_ref, wat_ref, wo_ref, x1_ref, h2_ref):
    ms = _dot(yssm_ref[...], wss_ref[...])
    ma = _dot(yatt_ref[...], wat_ref[...])
    merged = _sig(gs_ref[...]) * ms + _sig(ga_ref[...]) * ma
    x1 = x_ref[...] + g1_ref[...] * _dot(merged.astype(BF16), wo_ref[...])
    x1_ref[...] = x1
    y = x1 * lax.rsqrt(jnp.mean(x1 * x1, axis=-1, keepdims=True) + EPS)
    h2_ref[...] = (y * nw_ref[...] * (1.0 + sc2_ref[...]) + sh2_ref[...]).astype(BF16)


def _merge_out(x3, yssm, yatt, proj3, mod3, nw_row, wss, wat, wo, tm):
    bx, tx, _ = x3.shape
    r = mod3.shape[1]
    mod_b = (lambda b: b) if mod3.shape[0] == bx else (lambda b: 0)
    row = lambda w, cb: pl.BlockSpec((None, tm, w), lambda b, i: (b, i, cb))
    modc = lambda cb: pl.BlockSpec((None, r, D_MODEL), lambda b, i: (mod_b(b), 0, cb))
    const = lambda shape: pl.BlockSpec(shape, lambda b, i: (0, 0))
    return pl.pallas_call(
        _merge_kernel,
        grid=(bx, tx // tm),
        in_specs=[row(D_MODEL, 0), row(D_INNER, 0), row(ATT_WIDTH, 0),
                  row(1024, COL_GM // 1024), row(1024, COL_GM // 1024 + 1),
                  modc(2), modc(3), modc(4), const((1, D_MODEL)),
                  const(wss.shape), const(wat.shape), const(wo.shape)],
        out_specs=[row(D_MODEL, 0), row(D_MODEL, 0)],
        out_shape=[jax.ShapeDtypeStruct((bx, tx, D_MODEL), F32), jax.ShapeDtypeStruct((bx, tx, D_MODEL), BF16)],
        compiler_params=_params(("arbitrary", "arbitrary")),
        name="merge_out",
    )(x3, yssm, yatt, proj3, proj3, mod3, mod3, mod3, nw_row, wss, wat, wo)


FFN_TN = 1408
FFN_NT = D_FF // FFN_TN


def _ffn_up_prompt_kernel(h2_ref, wa_ref, wb_ref, cwa_ref, cwb_ref, cba_ref, cbb_ref,
                          act_ref, fa_ref, fb_ref, work, tails, *, tm, ntiles):
    i = pl.program_id(1)
    j = pl.program_id(2)
    h2 = h2_ref[...]
    u = jnp.concatenate([_dot(h2, wa_ref[...]), _dot(h2, wb_ref[...])], axis=1)
    work[8:8 + tm, :] = u

    @pl.when(i == 0)
    def _():
        work[0:8, :] = jnp.zeros((8, 2 * FFN_TN), F32)

    @pl.when(i > 0)
    def _():
        work[0:8, :] = tails[j]

    cw = jnp.concatenate([cwa_ref[...], cwb_ref[...]], axis=1)
    cb = jnp.concatenate([cba_ref[...], cbb_ref[...]], axis=1)
    conv = work[6:6 + tm, :] * cw[0:1, :] + work[7:7 + tm, :] * cw[1:2, :] + work[8:8 + tm, :] * cw[2:3, :] + cb
    act_ref[...] = (_silu(conv[:, 0:FFN_TN]) * conv[:, FFN_TN:2 * FFN_TN]).astype(BF16)
    tails[j] = work[tm:tm + 8, :]
    fa_ref[...] = work[tm + 6:tm + 8, 0:FFN_TN]
    fb_ref[...] = work[tm + 6:tm + 8, FFN_TN:2 * FFN_TN]


def _ffn_up_prompt(h2, wup, cw, cb_row, tm):
    bx, tx, _ = h2.shape
    ntiles = tx // tm
    return pl.pallas_call(
        functools.partial(_ffn_up_prompt_kernel, tm=tm, ntiles=ntiles),
        grid=(bx, ntiles, FFN_NT),
        in_specs=[pl.BlockSpec((None, tm, D_MODEL), lambda b, i, j: (b, i, 0)),
                  pl.BlockSpec((D_MODEL, FFN_TN), lambda b, i, j: (0, j)),
                  pl.BlockSpec((D_MODEL, FFN_TN), lambda b, i, j: (0, j + FFN_NT)),
                  pl.BlockSpec((FFN_CONV, FFN_TN), lambda b, i, j: (0, j)),
                  pl.BlockSpec((FFN_CONV, FFN_TN), lambda b, i, j: (0, j + FFN_NT)),
                  pl.BlockSpec((1, FFN_TN), lambda b, i, j: (0, j)),
                  pl.BlockSpec((1, FFN_TN), lambda b, i, j: (0, j + FFN_NT))],
        out_specs=[pl.BlockSpec((None, tm, FFN_TN), lambda b, i, j: (b, i, j)),
                   pl.BlockSpec((None, None, FFN_CONV - 1, FFN_TN), lambda b, i, j: (b, i, 0, j)),
                   pl.BlockSpec((None, None, FFN_CONV - 1, FFN_TN), lambda b, i, j: (b, i, 0, j))],
        out_shape=[jax.ShapeDtypeStruct((bx, tx, D_FF), BF16),
                   jax.ShapeDtypeStruct((bx, ntiles, FFN_CONV - 1, D_FF), F32),
                   jax.ShapeDtypeStruct((bx, ntiles, FFN_CONV - 1, D_FF), F32)],
        scratch_shapes=[pltpu.VMEM((tm + 8, 2 * FFN_TN), F32), pltpu.VMEM((FFN_NT, 8, 2 * FFN_TN), F32)],
        compiler_params=_params(("arbitrary", "arbitrary", "arbitrary")),
        name="ffn_up_prompt",
    )(h2, wup, wup, cw, cw, cb_row, cb_row)


def _ffn_up_sample_kernel(h2_ref, wa_ref, wb_ref, h0a_ref, h0b_ref, h1a_ref, h1b_ref, cwa_ref, cwb_ref,
                          cba_ref, cbb_ref, act_ref, ua_ref, ub_ref):
    h2 = h2_ref[...]
    ua = _dot(h2, wa_ref[...])
    ub = _dot(h2, wb_ref[...])
    cwa = cwa_ref[...]
    cwb = cwb_ref[...]
    ca = h0a_ref[...] * cwa[0:1, :] + h1a_ref[...] * cwa[1:2, :] + ua * cwa[2:3, :] + cba_ref[...]
    cb = h0b_ref[...] * cwb[0:1, :] + h1b_ref[...] * cwb[1:2, :] + ub * cwb[2:3, :] + cbb_ref[...]
    act_ref[...] = (_silu(ca) * cb).astype(BF16)
    ua_ref[...] = ua
    ub_ref[...] = ub


def _ffn_up_sample(h2, wup, hist0, hist1, cw, cb_row):
    nb = h2.shape[0]
    col = lambda rows, off: pl.BlockSpec((rows, FFN_TN), lambda j: (0, j + off))
    return pl.pallas_call(
        _ffn_up_sample_kernel,
        grid=(FFN_NT,),
        in_specs=[pl.BlockSpec((nb, D_MODEL), lambda j: (0, 0)),
                  col(D_MODEL, 0), col(D_MODEL, FFN_NT), col(nb, 0), col(nb, FFN_NT), col(nb, 0), col(nb, FFN_NT),
                  col(FFN_CONV, 0), col(FFN_CONV, FFN_NT), col(1, 0), col(1, FFN_NT)],
        out_specs=[col(nb, 0), col(nb, 0), col(nb, 0)],
        out_shape=[jax.ShapeDtypeStruct((nb, D_FF), BF16), jax.ShapeDtypeStruct((nb, D_FF), F32),
                   jax.ShapeDtypeStruct((nb, D_FF), F32)],
        compiler_params=_params(("arbitrary",)),
        name="ffn_up_sample",
    )(h2, wup, wup, hist0, hist0, hist1, hist1, cw, cw, cb_row, cb_row)


def _ffn_down_kernel(act_ref, x1_ref, g2_ref, nw_ref, w_ref, y_ref):
    x2 = x1_ref[...] + g2_ref[...] * _dot(act_ref[...], w_ref[...])
    y_ref[...] = x2 * lax.rsqrt(jnp.mean(x2 * x2, axis=-1, keepdims=True) + EPS) * nw_ref[...]


def _ffn_down(act, x1, mod3, nw_row, wdown, tm):
    bx, tx, _ = x1.shape
    r = mod3.shape[1]
    mod_b = (lambda b: b) if mod3.shape[0] == bx else (lambda b: 0)
    return pl.pallas_call(
        _ffn_down_kernel,
        grid=(bx, tx // tm),
        in_specs=[pl.BlockSpec((None, tm, D_FF), lambda b, i: (b, i, 0)),
                  pl.BlockSpec((None, tm, D_MODEL), lambda b, i: (b, i, 0)),
                  pl.BlockSpec((None, r, D_MODEL), lambda b, i: (mod_b(b), 0, 5)),
                  pl.BlockSpec((1, D_MODEL), lambda b, i: (0, 0)),
                  pl.BlockSpec((D_FF, D_MODEL), lambda b, i: (0, 0))],
        out_specs=pl.BlockSpec((None, tm, D_MODEL), lambda b, i: (b, i, 0)),
        out_shape=jax.ShapeDtypeStruct((bx, tx, D_MODEL), F32),
        compiler_params=_params(("arbitrary", "arbitrary")),
        name="ffn_down",
    )(act, x1, mod3, nw_row, wdown)


def _rope_tables(pos):
    half = ROPE_DIM // 2
    inv_freq = ROPE_THETA ** (-jnp.arange(half, dtype=F32) / half)
    ang = pos.astype(F32)[:, None] * inv_freq[None, :]
    cos, sin = jnp.cos(ang), jnp.sin(ang)
    n = pos.shape[0]
    ones = jnp.ones((n, ATT_HEAD_DIM - ROPE_DIM), F32)
    zeros8 = jnp.zeros((n, half), F32)
    zeros48 = jnp.zeros((n, ATT_HEAD_DIM - ROPE_DIM), F32)
    c = jnp.concatenate([cos, cos, ones], axis=1)
    s1 = jnp.concatenate([-sin, zeros8, zeros48], axis=1)
    s2 = jnp.concatenate([zeros8, sin, zeros48], axis=1)
    return tuple(jnp.concatenate([t, t], axis=1) for t in (c, s1, s2))


def _overlap(nc, ns):
    cst = np.arange(nc)[:, None] * CMP_STRIDE
    sst = np.arange(ns)[None, :] * SEL_LEN
    ov = np.clip(np.minimum(cst + CMP_LEN, sst + SEL_LEN) - np.maximum(cst, sst), 0, None)
    return (ov / CMP_STRIDE).astype(np.float32)


def _cmp_weights(pe, w1, w2):
    k16 = CMP_STRIDE * ATT_HEAD_DIM
    w1ab = jnp.concatenate([w1[:CMP_STRIDE].reshape(k16, CMP_HIDDEN), w1[CMP_STRIDE:].reshape(k16, CMP_HIDDEN)], axis=1)
    return (w1ab.astype(BF16), w1.reshape(CMP_LEN * ATT_HEAD_DIM, CMP_HIDDEN).astype(BF16), pe.reshape(1, -1),
            w2.astype(BF16))


def kernel(x_prompt, x_sample, c_prompt, c_sample, cache_nsa_kv, page_table, cache_win_kv, state_ssm, state_ssm_conv, state_ffn_conv, ada_w, ada_b, norm1_w, norm2_w, final_norm_w, w_in, ssm_conv_w, ssm_conv_b, ssm_dt_bias, ssm_A_log, ssm_D, ssm_norm_w, cmp_pe_k, cmp_w1_k, cmp_w2_k, cmp_pe_v, cmp_w1_v, cmp_w2_v, w_ssm_out, w_att_out, w_out, ffn_w_up, ffn_conv_w, ffn_conv_b, ffn_w_down):
    bp, tp, _ = x_prompt.shape
    nb = x_sample.shape[0]
    npages = page_table.shape[1]
    past = npages * PAGE_SIZE

    o_z, o_xbc, o_dt = 0, D_INNER, D_INNER + CONV_DIM
    o_q = o_dt + N_SSM_HEADS
    o_kv = o_q + ATT_WIDTH
    o_ag = o_kv + 6 * 256
    o_mg = o_ag + 3 * N_ATT_HEADS
    w_r = jnp.concatenate([
        w_in[:, o_z:o_z + 2048], w_in[:, o_xbc:o_xbc + 2048], w_in[:, o_q:o_q + 1024], w_in[:, o_mg:o_mg + 2048],
        w_in[:, o_xbc + 2048:o_xbc + 3072], w_in[:, o_kv:o_kv + 1536], w_in[:, o_dt:o_dt + 32],
        w_in[:, o_ag:o_ag + 48], jnp.zeros((D_MODEL, 48), F32)], axis=1).astype(BF16)
    ada_w_bf = ada_w.astype(BF16)
    wss = w_ssm_out.astype(BF16)
    wat = w_att_out.astype(BF16)
    wo = w_out.astype(BF16)
    wup = ffn_w_up.astype(BF16)
    wdown = ffn_w_down.astype(BF16)
    row = lambda v: v.reshape(1, -1)
    pad128 = lambda v: jnp.pad(v, (0, 128 - v.shape[0])).reshape(1, 128)
    dvec = jnp.repeat(ssm_D, SSM_HEAD_DIM).reshape(1, D_INNER)
    cmpw_k = _cmp_weights(cmp_pe_k, cmp_w1_k, cmp_w2_k)
    cmpw_v = _cmp_weights(cmp_pe_v, cmp_w1_v, cmp_w2_v)
    wbd, w1f, pe2, w2bd = (jnp.stack([a, b]) for a, b in zip(cmpw_k, cmpw_v))

    npad = (-(bp + nb)) % 8
    c_all = jnp.concatenate([c_prompt, c_sample, jnp.zeros((npad, D_MODEL), F32)], axis=0)
    mod = _ada_mod(c_all, ada_w_bf, row(ada_b))
    mod_p = mod[:bp].reshape(bp, 1, 6 * D_MODEL)
    mod_s = mod[bp:bp + nb].reshape(1, nb, 6 * D_MODEL)

    proj_p = _inproj(x_prompt, mod_p, row(norm1_w), w_r, 1024)
    yssm_p, conv_p, hlast_p = _ssd_prompt(proj_p, ssm_conv_w, row(ssm_conv_b), pad128(ssm_dt_bias),
                                          pad128(ssm_A_log), dvec, row(ssm_norm_w))
    tabs_p = _rope_tables(jnp.arange(tp, dtype=jnp.int32))
    q_p, kvrows_p, ksa_p, khm_p, gates_p, win_p = _rope_prep(proj_p, *tabs_p, 512)
    kc_p, vc_p = _compress_prompt(kvrows_p, wbd, w1f, pe2, w2bd)
    nblk = -(-tp // SEL_LEN)
    ncmp = tp // CMP_STRIDE
    ovt = np.zeros((nblk, ncmp), np.float32)
    ovt[:, :ncmp - 1] = _overlap(ncmp - 1, nblk).T
    yatt_p = _nsa_prompt(q_p, gates_p, kc_p, vc_p, ksa_p, khm_p, jnp.asarray(ovt))
    x1_p, h2_p = _merge_out(x_prompt, yssm_p, yatt_p, proj_p, mod_p, row(norm2_w), wss, wat, wo, 512)
    act_p, fa_p, fb_p = _ffn_up_prompt(h2_p, wup, ffn_conv_w, row(ffn_conv_b), 512)
    y_p = _ffn_down(act_p, x1_p, mod_p, row(final_norm_w), wdown, 512)

    x_s3 = x_sample.reshape(1, nb, D_MODEL)
    proj_s = _inproj(x_s3, mod_s, row(norm1_w), w_r, nb)
    eh = (np.arange(D_INNER)[None, :] // SSM_HEAD_DIM == np.arange(128)[:, None]).astype(np.float32)
    xdt, dec_e, yd, bm_c, cm_c, xdt_t, dec_t = _ssm_pre(
        proj_s, state_ssm_conv[:, 0], state_ssm_conv[:, 1], state_ssm_conv[:, 2], ssm_conv_w, row(ssm_conv_b),
        pad128(ssm_dt_bias), pad128(ssm_A_log), dvec, jnp.asarray(eh))
    h_new, yoff = _ssm_state(state_ssm.reshape(nb, D_INNER, SSM_STATE), xdt_t, dec_t, bm_c, cm_c)
    yssm_s = _ssm_post(yd, yoff[:, 0, :], dec_e, proj_s, row(ssm_norm_w))
    tabs_s = _rope_tables(jnp.full((nb,), past, dtype=jnp.int32))
    q_s, kvrows_s, _, _, _, wnew_s = _rope_prep(proj_s, *tabs_s, nb)
    cache3 = jnp.transpose(cache_nsa_kv, (0, 2, 3, 4, 1))
    win_t = jnp.transpose(cache_win_kv, (0, 2, 3, 4, 1))
    kc_s, vc_s = _compress_sample(page_table, cache3, wbd, w1f, pe2, w2bd)
    ncmp_s = past // CMP_STRIDE
    nsel_s = past // SEL_LEN + 1
    ov_s = np.zeros((ncmp_s, 128), np.float32)
    ov_s[:ncmp_s - 1, :nsel_s] = _overlap(ncmp_s - 1, nsel_s)
    e_s = (np.arange(past)[None, :] // SEL_LEN == np.arange(128)[:, None]).astype(np.float32)
    att_g_s = proj_s[0, :, COL_TAIL + 32:COL_TAIL + 80].reshape(nb, N_ATT_HEADS, 3)
    yatt_s16, win_s = _nsa_sample(
        page_table, cache3, q_s.reshape(nb, N_ATT_HEADS, ATT_HEAD_DIM), att_g_s, kvrows_s.reshape(nb, 1, 1024),
        wnew_s.reshape(nb, 1, 512), kc_s, vc_s, win_t, jnp.asarray(ov_s), jnp.asarray(e_s, dtype=BF16))
    x1_s, h2_s = _merge_out(x_s3, yssm_s.reshape(1, nb, D_INNER), yatt_s16.reshape(1, nb, ATT_WIDTH), proj_s, mod_s,
                            row(norm2_w), wss, wat, wo, nb)
    act_s, ua_s, ub_s = _ffn_up_sample(h2_s.reshape(nb, D_MODEL), wup, state_ffn_conv[:, 0], state_ffn_conv[:, 1],
                                       ffn_conv_w, row(ffn_conv_b))
    y_s = _ffn_down(act_s.reshape(1, nb, D_FF), x1_s, mod_s, row(final_norm_w), wdown, nb)

    xbc_s = jnp.concatenate([proj_s[0, :, COL_XS:COL_XS + 2048], proj_s[0, :, COL_BM:COL_BM + 1024]], axis=1)
    conv_s = jnp.stack([state_ssm_conv[:, 1], state_ssm_conv[:, 2], xbc_s], axis=1)
    ffn_s = jnp.stack([state_ffn_conv[:, 1], jnp.concatenate([ua_s, ub_s], axis=1)], axis=1)
    return (y_p,
            y_s.reshape(nb, 1, D_MODEL),
            kvrows_p.reshape(bp, tp, 4, N_KV_HEADS, ATT_HEAD_DIM),
            kvrows_s.reshape(nb, 1, 4, N_KV_HEADS, ATT_HEAD_DIM),
            win_p.reshape(bp, WINDOW, 2, N_KV_HEADS, ATT_HEAD_DIM),
            jnp.transpose(win_s, (0, 4, 1, 2, 3)),
            hlast_p.reshape(bp, N_SSM_HEADS, SSM_HEAD_DIM, SSM_STATE),
            h_new.reshape(nb, N_SSM_HEADS, SSM_HEAD_DIM, SSM_STATE),
            conv_p,
            conv_s,
            jnp.concatenate([fa_p[:, -1], fb_p[:, -1]], axis=2),
            ffn_s)
```

```python
import functools

import numpy as np
import jax
import jax.numpy as jnp
from jax import lax
from jax.experimental import pallas as pl
from jax.experimental.pallas import tpu as pltpu

F32 = jnp.float32
BF16 = jnp.bfloat16
HIGHEST = lax.Precision.HIGHEST

D_MODEL = 1024
D_INNER = 2048
N_SSM_HEADS = 32
SSM_HEAD_DIM = 64
SSM_STATE = 128
SSM_GROUPS = 4
SSM_CONV = 4
CONV_DIM = 3072
SSD_CHUNK = 128
N_ATT_HEADS = 16
ATT_HEAD_DIM = 64
N_KV_HEADS = 4
Q_PER_KV = 4
ATT_WIDTH = 1024
ATT_SCALE = ATT_HEAD_DIM ** -0.5
LOG2E = 1.4426950408889634
ROPE_DIM = 16
ROPE_THETA = 500000.0
CMP_LEN = 32
CMP_STRIDE = 16
CMP_HIDDEN = 128
SEL_LEN = 64
SEL_TOPK = 16
WINDOW = 512
Q_BLOCK = 128
PAGE_SIZE = 128
D_FF = 2816
FFN_CONV = 3
EPS = 1e-6
NEG = -1e30
BIG = 1e30
TINY = 1e-30

COL_Z = 0
COL_XS = 2048
COL_Q = 4096
COL_GM = 5120
COL_BM = 7168
COL_CM = 7680
COL_KV = 8192
COL_TAIL = 9728
N_PROJ = 9856
PROJ_TN = 896
VMEM_LIMIT = 56 * 1024 * 1024


def _sig(x):
    return 1.0 / (1.0 + jnp.exp(-x))


def _silu(x):
    return x * _sig(x)


def _softplus(x):
    return jnp.maximum(x, 0.0) + jnp.log1p(jnp.exp(-jnp.abs(x)))


def _dot(a, b):
    return jnp.dot(a, b, preferred_element_type=F32)


def _dot_hi(a, b):
    return jnp.dot(a, b, preferred_element_type=F32, precision=HIGHEST)


def _dot_nt(a, b):
    return lax.dot_general(a, b, (((1,), (1,)), ((), ())), preferred_element_type=F32)


def _dot_nt_hi(a, b):
    return lax.dot_general(a, b, (((1,), (1,)), ((), ())), preferred_element_type=F32, precision=HIGHEST)


def _params(sem):
    return pltpu.CompilerParams(dimension_semantics=sem, vmem_limit_bytes=VMEM_LIMIT)


def _ada_kernel(c_ref, w_ref, b_ref, o_ref):
    c = c_ref[...]
    o_ref[...] = _dot(_silu(c).astype(BF16), w_ref[...]) + b_ref[...]


def _ada_mod(c_all, w_bf, b_row):
    m = c_all.shape[0]
    tn = 512
    return pl.pallas_call(
        _ada_kernel,
        grid=(w_bf.shape[1] // tn,),
        in_specs=[pl.BlockSpec((m, D_MODEL), lambda j: (0, 0)),
                  pl.BlockSpec((D_MODEL, tn), lambda j: (0, j)),
                  pl.BlockSpec((1, tn), lambda j: (0, j))],
        out_specs=pl.BlockSpec((m, tn), lambda j: (0, j)),
        out_shape=jax.ShapeDtypeStruct((m, w_bf.shape[1]), F32),
        compiler_params=_params(("arbitrary",)),
        name="ada_mod",
    )(c_all, w_bf, b_row)


def _inproj_kernel(x_ref, sh_ref, sc_ref, nw_ref, w_ref, o_ref, h_scr):
    @pl.when(pl.program_id(2) == 0)
    def _():
        x = x_ref[...]
        y = x * lax.rsqrt(jnp.mean(x * x, axis=-1, keepdims=True) + EPS)
        h = y * nw_ref[...] * (1.0 + sc_ref[...]) + sh_ref[...]
        h_scr[...] = h.astype(BF16)

    o_ref[...] = _dot(h_scr[...], w_ref[...])


def _inproj(x3, mod3, nw_row, w_bf, tm):
    bx, tx, _ = x3.shape
    r = mod3.shape[1]
    mod_b = (lambda b: b) if mod3.shape[0] == bx else (lambda b: 0)
    return pl.pallas_call(
        _inproj_kernel,
        grid=(bx, tx // tm, N_PROJ // PROJ_TN),
        in_specs=[pl.BlockSpec((None, tm, D_MODEL), lambda b, i, j: (b, i, 0)),
                  pl.BlockSpec((None, r, D_MODEL), lambda b, i, j: (mod_b(b), 0, 0)),
                  pl.BlockSpec((None, r, D_MODEL), lambda b, i, j: (mod_b(b), 0, 1)),
                  pl.BlockSpec((1, D_MODEL), lambda b, i, j: (0, 0)),
                  pl.BlockSpec((D_MODEL, PROJ_TN), lambda b, i, j: (0, j))],
        out_specs=pl.BlockSpec((None, tm, PROJ_TN), lambda b, i, j: (b, i, j)),
        out_shape=jax.ShapeDtypeStruct((bx, tx, N_PROJ), F32),
        scratch_shapes=[pltpu.VMEM((tm, D_MODEL), BF16)],
        compiler_params=_params(("arbitrary", "arbitrary", "arbitrary")),
        name="inproj",
    )(x3, mod3, mod3, nw_row, w_bf)


def _ssd_kernel(z_ref, xs_ref, bm_ref, cm_ref, dt_ref, cw_ref, cb_ref, dtb_ref, alog_ref, dvec_ref, nw_ref,
                y_ref, conv_ref, hout_ref, xpad, h_t, *, nchunks):
    c = pl.program_id(1)
    ln = SSD_CHUNK

    @pl.when(c == 0)
    def _():
        xpad[0:8, :] = jnp.zeros((8, CONV_DIM), F32)
        h_t[...] = jnp.zeros_like(h_t)

    xpad[8:8 + ln, 0:2048] = xs_ref[...]
    xpad[8:8 + ln, 2048:2560] = bm_ref[...]
    xpad[8:8 + ln, 2560:3072] = cm_ref[...]
    cw = cw_ref[...]
    conv = (xpad[5:5 + ln, :] * cw[0:1, :] + xpad[6:6 + ln, :] * cw[1:2, :] + xpad[7:7 + ln, :] * cw[2:3, :]
            + xpad[8:8 + ln, :] * cw[3:4, :] + cb_ref[...])
    xc = _silu(conv)
    xs_c = xc[:, 0:2048]
    bm_c = xc[:, 2048:2560]
    cm_c = xc[:, 2560:3072]

    dt = _softplus(dt_ref[...] + dtb_ref[...])
    a = -jnp.exp(alog_ref[...])
    row = lax.broadcasted_iota(jnp.int32, (ln, ln), 0)
    col = lax.broadcasted_iota(jnp.int32, (ln, ln), 1)
    causal = row >= col
    cs = _dot_hi(causal.astype(F32), dt * a)
    cs_t = cs.T
    dt_t = dt.T

    x_bf = xs_c.astype(BF16)
    bm_bf = bm_c.astype(BF16)
    cm_bf = cm_c.astype(BF16)
    lo = lax.broadcasted_iota(jnp.int32, (1, 128), 1) < 64
    ys = []
    for g in range(SSM_GROUPS):
        bg = bm_bf[:, g * 128:(g + 1) * 128]
        cg = cm_bf[:, g * 128:(g + 1) * 128]
        cb = _dot_nt(cg, bg)
        b_t = bm_c[:, g * 128:(g + 1) * 128].T
        for r2 in range(4):
            pair = g * 4 + r2
            xp = x_bf[:, pair * 128:(pair + 1) * 128]
            yd, st = [], []
            for h in (2 * pair, 2 * pair + 1):
                cs_col = cs[:, h:h + 1]
                cs_row = cs_t[h:h + 1, :]
                dt_row = dt_t[h:h + 1, :]
                lmat = jnp.where(causal, jnp.exp(cs_col - cs_row), 0.0)
                yd.append(_dot((cb * lmat * dt_row).astype(BF16), xp))
                w_row = dt_row * jnp.exp(cs_t[h:h + 1, ln - 1:ln] - cs_row)
                st.append(_dot((b_t * w_row).astype(BF16), xp))
            ha, hb = 2 * pair, 2 * pair + 1
            ecol = jnp.where(lo, jnp.exp(cs[:, ha:ha + 1]), jnp.exp(cs[:, hb:hb + 1]))
            hprev = h_t[pair]
            yoff = _dot(cg, hprev.astype(BF16)) * ecol
            ys.append(jnp.where(lo, yd[0], yd[1]) + yoff)
            edec = jnp.where(lo, jnp.exp(cs_t[ha:ha + 1, ln - 1:ln]), jnp.exp(cs_t[hb:hb + 1, ln - 1:ln]))
            h_t[pair] = hprev * edec + jnp.where(lo, st[0], st[1])
    y = jnp.concatenate(ys, axis=1) + xs_c * dvec_ref[...]
    yz = y * _silu(z_ref[...])
    ms = jnp.mean(yz * yz, axis=-1, keepdims=True)
    y_ref[...] = (yz * lax.rsqrt(ms + EPS) * nw_ref[...]).astype(BF16)

    xpad[0:8, :] = xpad[ln:ln + 8, :]

    @pl.when(c == nchunks - 1)
    def _():
        conv_ref[...] = xpad[ln + 5:ln + 8, :]
        for pair in range(16):
            hout_ref[pair * 128:(pair + 1) * 128, :] = h_t[pair].T


def _ssd_prompt(proj3, cw, cb_row, dtb_row, alog_row, dvec_row, nw_row):
    bx, tx, _ = proj3.shape
    nchunks = tx // SSD_CHUNK
    ln = SSD_CHUNK
    const = lambda shape: pl.BlockSpec(shape, lambda b, c: (0, 0))
    return pl.pallas_call(
        functools.partial(_ssd_kernel, nchunks=nchunks),
        grid=(bx, nchunks),
        in_specs=[pl.BlockSpec((None, ln, 2048), lambda b, c: (b, c, COL_Z // 2048)),
                  pl.BlockSpec((None, ln, 2048), lambda b, c: (b, c, COL_XS // 2048)),
                  pl.BlockSpec((None, ln, 512), lambda b, c: (b, c, COL_BM // 512)),
                  pl.BlockSpec((None, ln, 512), lambda b, c: (b, c, COL_CM // 512)),
                  pl.BlockSpec((None, ln, 128), lambda b, c: (b, c, COL_TAIL // 128)),
                  const((SSM_CONV, CONV_DIM)), const((1, CONV_DIM)), const((1, 128)), const((1, 128)),
                  const((1, D_INNER)), const((1, D_INNER))],
        out_specs=[pl.BlockSpec((None, ln, D_INNER), lambda b, c: (b, c, 0)),
                   pl.BlockSpec((None, SSM_CONV - 1, CONV_DIM), lambda b, c: (b, 0, 0)),
                   pl.BlockSpec((None, N_SSM_HEADS * SSM_HEAD_DIM, SSM_STATE), lambda b, c: (b, 0, 0))],
        out_shape=[jax.ShapeDtypeStruct((bx, tx, D_INNER), BF16),
                   jax.ShapeDtypeStruct((bx, SSM_CONV - 1, CONV_DIM), F32),
                   jax.ShapeDtypeStruct((bx, N_SSM_HEADS * SSM_HEAD_DIM, SSM_STATE), F32)],
        scratch_shapes=[pltpu.VMEM((ln + 8, CONV_DIM), F32), pltpu.VMEM((16, 128, 128), F32)],
        compiler_params=_params(("arbitrary", "arbitrary")),
        name="ssd_prompt",
    )(proj3, proj3, proj3, proj3, proj3, cw, cb_row, dtb_row, alog_row, dvec_row, nw_row)


def _ssm_pre_kernel(xs_ref, bm_ref, cm_ref, dt_ref, s0_ref, s1_ref, s2_ref, cw_ref, cb_ref, dtb_ref, alog_ref,
                    dvec_ref, eh_ref, xdt_ref, dec_ref, yd_ref, bmc_ref, cmc_ref, xdt_t_ref, dec_t_ref):
    cw = cw_ref[...]
    xbc = jnp.concatenate([xs_ref[...], bm_ref[...], cm_ref[...]], axis=1)
    conv = s0_ref[...] * cw[0:1, :] + s1_ref[...] * cw[1:2, :] + s2_ref[...] * cw[2:3, :] + xbc * cw[3:4, :] + cb_ref[...]
    xc = _silu(conv)
    xs_c = xc[:, 0:2048]
    bm_c = xc[:, 2048:2560]
    cm_c = xc[:, 2560:3072]
    dt = _softplus(dt_ref[...] + dtb_ref[...])
    a = -jnp.exp(alog_ref[...])
    dec = jnp.exp(dt * a)
    eh = eh_ref[...]
    dt_e = _dot_hi(dt, eh)
    dec_e = _dot_hi(dec, eh)
    xdt = xs_c * dt_e
    cbs = []
    for g in range(SSM_GROUPS):
        cbg = jnp.sum(cm_c[:, g * 128:(g + 1) * 128] * bm_c[:, g * 128:(g + 1) * 128], axis=-1, keepdims=True)
        cbs.append(jnp.broadcast_to(cbg, (cbg.shape[0], 512)))
    cb_e = jnp.concatenate(cbs, axis=1)
    xdt_ref[...] = xdt
    dec_ref[...] = dec_e
    yd_ref[...] = cb_e * xdt + xs_c * dvec_ref[...]
    bmc_ref[...] = bm_c
    cmc_ref[...] = cm_c
    for k in range(16):
        xdt_t_ref[k * 128:(k + 1) * 128, :] = xdt[:, k * 128:(k + 1) * 128].T
        dec_t_ref[k * 128:(k + 1) * 128, :] = dec_e[:, k * 128:(k + 1) * 128].T


def _ssm_pre(proj3, s0, s1, s2, cw, cb_row, dtb_row, alog_row, dvec_row, eh):
    nb = proj3.shape[1]
    const = lambda shape: pl.BlockSpec(shape, lambda i: (0,) * len(shape))
    return pl.pallas_call(
        _ssm_pre_kernel,
        grid=(1,),
        in_specs=[pl.BlockSpec((None, nb, 2048), lambda i: (0, 0, COL_XS // 2048)),
                  pl.BlockSpec((None, nb, 512), lambda i: (0, 0, COL_BM // 512)),
                  pl.BlockSpec((None, nb, 512), lambda i: (0, 0, COL_CM // 512)),
                  pl.BlockSpec((None, nb, 128), lambda i: (0, 0, COL_TAIL // 128)),
                  const((nb, CONV_DIM)), const((nb, CONV_DIM)), const((nb, CONV_DIM)),
                  const((SSM_CONV, CONV_DIM)), const((1, CONV_DIM)), const((1, 128)), const((1, 128)),
                  const((1, D_INNER)), const((128, D_INNER))],
        out_specs=[const((nb, D_INNER)), const((nb, D_INNER)), const((nb, D_INNER)), const((nb, 512)),
                   const((nb, 512)), const((D_INNER, nb)), const((D_INNER, nb))],
        out_shape=[jax.ShapeDtypeStruct((nb, D_INNER), F32)] * 3 + [jax.ShapeDtypeStruct((nb, 512), F32)] * 2
        + [jax.ShapeDtypeStruct((D_INNER, nb), F32)] * 2,
        compiler_params=_params(("arbitrary",)),
        name="ssm_pre",
    )(proj3, proj3, proj3, proj3, s0, s1, s2, cw, cb_row, dtb_row, alog_row, dvec_row, eh)


def _ssm_state_kernel(h0_ref, xdt_t_ref, dec_t_ref, bm_ref, cm_ref, hn_ref, yoff_ref):
    b = pl.program_id(0)
    h0 = h0_ref[...]
    onehot = (lax.broadcasted_iota(jnp.int32, (128, 128), 0) == b).astype(F32)
    dec_b = _dot_hi(dec_t_ref[...], onehot)
    outs, yoffs = [], []
    for g in range(SSM_GROUPS):
        bc_row = bm_ref[:, g * 128:(g + 1) * 128]
        outs.append(_dot_hi(xdt_t_ref[g * 512:(g + 1) * 512, :], onehot * bc_row))
        cm_row = cm_ref[:, g * 128:(g + 1) * 128]
        cm8 = jnp.broadcast_to(cm_row, (8, 128)).astype(BF16)
        yoffs.append(_dot_nt(cm8, h0[g * 512:(g + 1) * 512, :].astype(BF16)))
    hn_ref[...] = h0 * dec_b + jnp.concatenate(outs, axis=0)
    yoff_ref[...] = jnp.concatenate(yoffs, axis=1)


def _ssm_state(h0, xdt_t, dec_t, bm_c, cm_c):
    nb = h0.shape[0]
    const = lambda shape: pl.BlockSpec(shape, lambda b: (0, 0))
    return pl.pallas_call(
        _ssm_state_kernel,
        grid=(nb,),
        in_specs=[pl.BlockSpec((None, D_INNER, SSM_STATE), lambda b: (b, 0, 0)),
                  const((D_INNER, nb)), const((D_INNER, nb)),
                  pl.BlockSpec((None, 1, 512), lambda b: (b, 0, 0)), pl.BlockSpec((None, 1, 512), lambda b: (b, 0, 0))],
        out_specs=[pl.BlockSpec((None, D_INNER, SSM_STATE), lambda b: (b, 0, 0)),
                   pl.BlockSpec((None, 8, D_INNER), lambda b: (b, 0, 0))],
        out_shape=[jax.ShapeDtypeStruct((nb, D_INNER, SSM_STATE), F32),
                   jax.ShapeDtypeStruct((nb, 8, D_INNER), F32)],
        compiler_params=_params(("arbitrary",)),
        name="ssm_state",
    )(h0, xdt_t, dec_t, bm_c.reshape(nb, 1, 512), cm_c.reshape(nb, 1, 512))


def _ssm_post_kernel(yd_ref, yoff_ref, dec_ref, z_ref, nw_ref, y_ref):
    y = yd_ref[...] + yoff_ref[...] * dec_ref[...]
    yz = y * _silu(z_ref[...])
    ms = jnp.mean(yz * yz, axis=-1, keepdims=True)
    y_ref[...] = (yz * lax.rsqrt(ms + EPS) * nw_ref[...]).astype(BF16)


def _ssm_post(yd, yoff, dec_e, proj3, nw_row):
    nb = yd.shape[0]
    const = lambda shape: pl.BlockSpec(shape, lambda i: (0, 0))
    return pl.pallas_call(
        _ssm_post_kernel,
        grid=(1,),
        in_specs=[const((nb, D_INNER)), const((nb, D_INNER)), const((nb, D_INNER)),
                  pl.BlockSpec((None, nb, 2048), lambda i: (0, 0, COL_Z // 2048)), const((1, D_INNER))],
        out_specs=const((nb, D_INNER)),
        out_shape=jax.ShapeDtypeStruct((nb, D_INNER), BF16),
        compiler_params=_params(("arbitrary",)),
        name="ssm_post",
    )(yd, yoff, dec_e, proj3, nw_row)


def _rope128(x, c, s1, s2):
    return x * c + pltpu.roll(x, 120, axis=1) * s1 + pltpu.roll(x, 8, axis=1) * s2


def _rope_kernel(q_ref, k01_ref, k23_ref, k45_ref, tail_ref, c_ref, s1_ref, s2_ref,
                 qo_ref, kvrows_ref, ksa_ref, khm_ref, gates_ref, win_ref, *, ntiles, tr):
    c = c_ref[...]
    s1 = s1_ref[...]
    s2 = s2_ref[...]
    q = q_ref[...]
    qo_ref[...] = (jnp.concatenate(
        [_rope128(q[:, k * 128:(k + 1) * 128], c, s1, s2) for k in range(8)], axis=1
    ) * (ATT_SCALE * LOG2E)).astype(BF16)
    streams = []
    for pref in (k01_ref, k23_ref, k45_ref):
        blk = pref[...]
        kk = jnp.concatenate([_rope128(blk[:, k * 128:(k + 1) * 128], c, s1, s2) for k in range(2)], axis=1)
        streams.append(kk)
        streams.append(blk[:, 256:512])
    kvrows_ref[...] = jnp.concatenate(streams[0:4], axis=1)
    pos = pl.program_id(1) * tr + lax.broadcasted_iota(jnp.int32, (tr, 64), 0)
    own = lax.broadcasted_iota(jnp.int32, (tr, 64), 1) == lax.shift_right_logical(pos, 6)
    extra = jnp.where(own, NEG, 0.0).astype(BF16)
    ksel = streams[2].astype(BF16)
    for h in range(N_KV_HEADS):
        ksa_ref[h] = jnp.concatenate([ksel[:, h * 64:(h + 1) * 64], extra], axis=1)
    for i, s in enumerate((3, 4, 5)):
        sb = streams[s].astype(BF16)
        for h in range(N_KV_HEADS):
            khm_ref[i * 4 + h] = sb[:, h * 64:(h + 1) * 64]
    g = _sig(tail_ref[...])
    for hk in range(N_KV_HEADS):
        gates_ref[hk] = g[:, 32 + hk * 12:32 + (hk + 1) * 12]

    @pl.when(pl.program_id(1) == ntiles - 1)
    def _():
        win_ref[...] = jnp.concatenate(streams[4:6], axis=1)


def _rope_prep(proj3, ctab, s1tab, s2tab, tr):
    bx, tx, _ = proj3.shape
    ntiles = tx // tr
    tab = pl.BlockSpec((tr, 128), lambda b, i: (i, 0))
    return pl.pallas_call(
        functools.partial(_rope_kernel, ntiles=ntiles, tr=tr),
        grid=(bx, ntiles),
        in_specs=[pl.BlockSpec((None, tr, 1024), lambda b, i: (b, i, COL_Q // 1024)),
                  pl.BlockSpec((None, tr, 512), lambda b, i: (b, i, COL_KV // 512)),
                  pl.BlockSpec((None, tr, 512), lambda b, i: (b, i, COL_KV // 512 + 1)),
                  pl.BlockSpec((None, tr, 512), lambda b, i: (b, i, COL_KV // 512 + 2)),
                  pl.BlockSpec((None, tr, 128), lambda b, i: (b, i, COL_TAIL // 128)),
                  tab, tab, tab],
        out_specs=[pl.BlockSpec((None, tr, 1024), lambda b, i: (b, i, 0)),
                   pl.BlockSpec((None, tr, 1024), lambda b, i: (b, i, 0)),
                   pl.BlockSpec((None, 4, tr, 128), lambda b, i: (b, 0, i, 0)),
                   pl.BlockSpec((None, 12, tr, 64), lambda b, i: (b, 0, i, 0)),
                   pl.BlockSpec((None, 4, tr, 12), lambda b, i: (b, 0, i, 0)),
                   pl.BlockSpec((None, tr, 512), lambda b, i: (b, 0, 0))],
        out_shape=[jax.ShapeDtypeStruct((bx, tx, 1024), BF16),
                   jax.ShapeDtypeStruct((bx, tx, 1024), F32),
                   jax.ShapeDtypeStruct((bx, 4, tx, 128), BF16),
                   jax.ShapeDtypeStruct((bx, 12, tx, 64), BF16),
                   jax.ShapeDtypeStruct((bx, 4, tx, 12), F32),
                   jax.ShapeDtypeStruct((bx, tr, 512), F32)],
        compiler_params=_params(("arbitrary", "arbitrary")),
        name="rope_prep",
    )(proj3, proj3, proj3, proj3, proj3, ctab, s1tab, s2tab)


def _compress_kernel(*refs, nrefs, npages, head_major):
    if nrefs > 1:
        refs = refs[1:]
    page_refs = refs[:nrefs]
    w1ab_ref, w1f_ref, pe_ref, w2_ref, kc_ref, vc_ref, shift = refs[nrefs:]
    nsub = PAGE_SIZE // CMP_STRIDE
    nj = npages * nsub
    nr = N_KV_HEADS * nj
    shift[nr:nr + 8, :] = jnp.zeros((8, CMP_HIDDEN), F32)
    ri = lax.broadcasted_iota(jnp.int32, (PAGE_SIZE, PAGE_SIZE), 0)
    ci = lax.broadcasted_iota(jnp.int32, (PAGE_SIZE, PAGE_SIZE), 1)
    perm = (ci == (ri & (nsub - 1)) * CMP_STRIDE + lax.shift_right_logical(ri, 3)).astype(BF16)
    if nrefs == 1:
        pages = [page_refs[0][p * PAGE_SIZE:(p + 1) * PAGE_SIZE, :] for p in range(npages)]
        grouped = [_dot(perm, pg.astype(BF16)) for pg in pages]
    else:
        grouped = [jnp.concatenate(
            [_dot_nt(perm, pr[st].reshape(N_KV_HEADS * ATT_HEAD_DIM, PAGE_SIZE).astype(BF16)) for st in range(2)],
            axis=1) for pr in page_refs]
    for st, o_ref in ((0, kc_ref), (1, vc_ref)):
        heads = []
        for h in range(N_KV_HEADS):
            c0 = st * 256 + h * ATT_HEAD_DIM
            rows = [jnp.concatenate([y[s * nsub:(s + 1) * nsub, c0:c0 + ATT_HEAD_DIM] for s in range(CMP_STRIDE)],
                                    axis=1) for y in grouped]
            heads.append(rows[0] if npages == 1 else jnp.concatenate(rows, axis=0))
        a = jnp.concatenate(heads, axis=0).astype(BF16)
        p2 = _dot(a, w1ab_ref[st])
        shift[0:nr, :] = p2[:, CMP_HIDDEN:2 * CMP_HIDDEN]
        pe8 = jnp.broadcast_to(pe_ref[st], (8, CMP_LEN * ATT_HEAD_DIM)).astype(BF16)
        pe_t = _dot(pe8, w1f_ref[st])[0:1, :]
        hid = p2[:, 0:CMP_HIDDEN] + shift[pl.ds(1, nr), :] + pe_t
        out = _dot(_silu(hid).astype(BF16), w2_ref[st]).astype(BF16)
        if head_major:
            for h in range(N_KV_HEADS):
                o_ref[h] = out[h * nj:(h + 1) * nj, :]
        else:
            o_ref[...] = jnp.concatenate([out[h * nj:(h + 1) * nj, :] for h in range(N_KV_HEADS)], axis=1)


def _compress_prompt(kv_rows, wbd, w1f, pe, w2bd):
    bx, tx, _ = kv_rows.shape
    nj = tx // CMP_STRIDE
    c4 = lambda shape: pl.BlockSpec(shape, lambda b: (0,) * len(shape))
    return pl.pallas_call(
        functools.partial(_compress_kernel, nrefs=1, npages=tx // PAGE_SIZE, head_major=True),
        grid=(bx,),
        in_specs=[pl.BlockSpec((None, tx, 512), lambda b: (b, 0, 0)),
                  c4(wbd.shape), c4(w1f.shape), c4(pe.shape), c4(w2bd.shape)],
        out_specs=[pl.BlockSpec((None, 4, nj, 64), lambda b: (b, 0, 0, 0))] * 2,
        out_shape=[jax.ShapeDtypeStruct((bx, 4, nj, 64), BF16)] * 2,
        scratch_shapes=[pltpu.VMEM((N_KV_HEADS * nj + 8, CMP_HIDDEN), F32)],
        compiler_params=_params(("arbitrary",)),
        name="compress_prompt",
    )(kv_rows, wbd, w1f, pe, w2bd)


def _compress_sample(page_table, cache3, wbd, w1f, pe, w2bd):
    nb, npages = page_table.shape
    nsub = PAGE_SIZE // CMP_STRIDE
    nj = npages * nsub
    c4 = lambda shape: pl.BlockSpec(shape, lambda b, pt: (0,) * len(shape))
    page_specs = [pl.BlockSpec((None, 2, N_KV_HEADS, ATT_HEAD_DIM, PAGE_SIZE),
                               functools.partial(lambda b, pt, p: (pt[b, p], 0, 0, 0, 0), p=p))
                  for p in range(npages)]
    grid_spec = pltpu.PrefetchScalarGridSpec(
        num_scalar_prefetch=1,
        grid=(nb,),
        in_specs=page_specs + [c4(wbd.shape), c4(w1f.shape), c4(pe.shape), c4(w2bd.shape)],
        out_specs=[pl.BlockSpec((None, nj, 256), lambda b, pt: (b, 0, 0))] * 2,
        scratch_shapes=[pltpu.VMEM((N_KV_HEADS * nj + 8, CMP_HIDDEN), F32)],
    )
    return pl.pallas_call(
        functools.partial(_compress_kernel, nrefs=npages, npages=npages, head_major=False),
        grid_spec=grid_spec,
        out_shape=[jax.ShapeDtypeStruct((nb, nj, 256), BF16)] * 2,
        compiler_params=_params(("arbitrary",)),
        name="compress_sample",
    )(page_table, *([cache3] * npages), wbd, w1f, pe, w2bd)


KEY_BLOCK = 1024
NSA_QUERY_BLOCK = 256


def _nsa_prompt_kernel(q_ref, g_ref, kc_ref, vc_ref, ksa_ref, vs_ref, kw_ref, vw_ref, ovt_ref,
                       o_ref, s_scr, m_scr, l_scr, acc_scr, *, qblk):
    qb = pl.program_id(2)
    q0 = qb * qblk
    q4 = q_ref[...]
    nq = Q_PER_KV * qblk
    qs = jnp.concatenate([q4[:, r * 64:(r + 1) * 64] for r in range(Q_PER_KV)], axis=0)
    ncmp = kc_ref.shape[0]
    trow = q0 + (lax.broadcasted_iota(jnp.int32, (nq, 1), 0) & (qblk - 1))

    jl = lax.broadcasted_iota(jnp.int32, (nq, ncmp), 1)
    mask_c = (jl * CMP_STRIDE + (CMP_LEN - 1)) <= trow
    s = jnp.where(mask_c, _dot_nt(qs, kc_ref[...]), NEG)
    e = jnp.where(mask_c, jnp.exp2(s - jnp.max(s, axis=-1, keepdims=True)), 0.0)
    p = e / jnp.maximum(jnp.sum(e, axis=-1, keepdims=True), TINY)
    o_c = _dot(p.astype(BF16), vc_ref[...])
    psum = p[0:qblk] + p[qblk:2 * qblk] + p[2 * qblk:3 * qblk] + p[3 * qblk:4 * qblk]
    imp_t = _dot_nt_hi(ovt_ref[...], psum)
    nblk = imp_t.shape[0]
    tq = q0 + lax.broadcasted_iota(jnp.int32, (nblk, qblk), 1)
    blk = lax.broadcasted_iota(jnp.int32, (nblk, qblk), 0)
    cur = lax.shift_right_logical(tq, 6)
    valid = blk * SEL_LEN <= tq
    forced = (blk == 0) | (blk == cur) | (blk == cur - 1)
    imp = jnp.where(valid, jnp.where(forced, BIG, imp_t), -BIG)
    rank = jnp.zeros((nblk, qblk), F32)
    for j in range(nblk):
        rj = imp[j:j + 1, :]
        beats = (rj > imp) | ((rj == imp) & (blk > j))
        rank = rank + jnp.where(beats, 1.0, 0.0)
    nsel_t = jnp.where(rank < float(SEL_TOPK), 0.0, 1.0)
    nsel_pad = jnp.concatenate([nsel_t, jnp.zeros((128 - nblk, qblk), F32)], axis=0) if nblk < 128 else nsel_t
    nsel = jnp.concatenate([nsel_pad[:, i * 128:(i + 1) * 128].T for i in range(qblk // 128)],
                           axis=0)[:, 0:64].astype(BF16)
    qaug = jnp.concatenate([qs, jnp.concatenate([nsel] * Q_PER_KV, axis=0)], axis=1)

    wlen = WINDOW + qblk
    wstart = pl.multiple_of(jnp.maximum(q0 - WINDOW, 0), 128)
    kw = kw_ref[pl.ds(wstart, wlen), :]
    vw = vw_ref[pl.ds(wstart, wlen), :]
    d = trow[0:qblk] - (wstart + lax.broadcasted_iota(jnp.int32, (qblk, wlen), 1))
    ok = (d >= 0) & (d < WINDOW)
    o_w = []
    for r in range(Q_PER_KV):
        sw = jnp.where(ok, _dot_nt(qs[r * qblk:(r + 1) * qblk], kw), NEG)
        ew = jnp.exp2(sw - jnp.max(sw, axis=-1, keepdims=True))
        o_w.append(_dot(ew.astype(BF16), vw) / jnp.maximum(jnp.sum(ew, axis=-1, keepdims=True), TINY))

    cw = 256
    nck = KEY_BLOCK // cw

    def block_scores(jb, causal):
        base = pl.multiple_of(jb * KEY_BLOCK, KEY_BLOCK)
        mx = None
        for c in range(nck):
            sc = _dot_nt(qaug, ksa_ref[pl.ds(base + c * cw, cw), :])
            if causal:
                kpos = base + c * cw + lax.broadcasted_iota(jnp.int32, (nq, cw), 1)
                sc = jnp.where(kpos <= trow, sc, NEG)
            s_scr[jb, :, c * cw:(c + 1) * cw] = sc
            for i in range(cw // 128):
                part = sc[:, i * 128:(i + 1) * 128]
                mx = part if mx is None else jnp.maximum(mx, part)
        m_scr[...] = jnp.maximum(m_scr[...], mx)

    nfull = lax.shift_right_logical(qb, (KEY_BLOCK // qblk).bit_length() - 1)
    m_scr[...] = jnp.full((nq, 128), -jnp.inf, F32)

    def pass_a(jb, carry):
        block_scores(jb, False)
        return carry

    lax.fori_loop(0, nfull, pass_a, 0)
    block_scores(nfull, True)
    mb = jnp.broadcast_to(jnp.max(m_scr[...], axis=-1, keepdims=True), (nq, 128))

    l_scr[...] = jnp.zeros((nq, 128), F32)
    acc_scr[...] = jnp.zeros((nq, 64), F32)

    def pass_b(jb, carry):
        base = pl.multiple_of(jb * KEY_BLOCK, KEY_BLOCK)
        ps, lsum = [], None
        for c in range(KEY_BLOCK // 128):
            pc = jnp.exp2(s_scr[jb, :, c * 128:(c + 1) * 128] - mb)
            lsum = pc if lsum is None else lsum + pc
            ps.append(pc.astype(BF16))
        l_scr[...] = l_scr[...] + lsum
        acc_scr[...] = acc_scr[...] + _dot(jnp.concatenate(ps, axis=1), vs_ref[pl.ds(base, KEY_BLOCK), :])
        return carry

    lax.fori_loop(0, nfull + 1, pass_b, 0)
    o_s = acc_scr[...] / jnp.maximum(jnp.sum(l_scr[...], axis=-1, keepdims=True), TINY)

    g = g_ref[...]
    outs = []
    for r in range(Q_PER_KV):
        rows = slice(r * qblk, (r + 1) * qblk)
        outs.append(g[:, 3 * r:3 * r + 1] * o_c[rows] + g[:, 3 * r + 1:3 * r + 2] * o_s[rows]
                    + g[:, 3 * r + 2:3 * r + 3] * o_w[r])
    o_ref[...] = jnp.concatenate(outs, axis=1).astype(BF16)


def _nsa_prompt(q_r, gates, kc, vc, ksa, khm, ovt):
    bx, tx, _ = q_r.shape
    qblk = NSA_QUERY_BLOCK
    ntiles = tx // qblk
    ncmp = kc.shape[2]
    nq = Q_PER_KV * qblk
    assert tx % KEY_BLOCK == 0 and tx >= WINDOW + qblk and tx <= 64 * SEL_LEN
    kv_spec = lambda s: pl.BlockSpec((None, None, tx, 64), lambda b, h, i: (b, s * 4 + h, 0, 0))
    return pl.pallas_call(
        functools.partial(_nsa_prompt_kernel, qblk=qblk),
        grid=(bx, N_KV_HEADS, ntiles),
        in_specs=[pl.BlockSpec((None, qblk, 256), lambda b, h, i: (b, i, h)),
                  pl.BlockSpec((None, None, qblk, 12), lambda b, h, i: (b, h, i, 0)),
                  pl.BlockSpec((None, None, ncmp, 64), lambda b, h, i: (b, h, 0, 0)),
                  pl.BlockSpec((None, None, ncmp, 64), lambda b, h, i: (b, h, 0, 0)),
                  pl.BlockSpec((None, None, tx, 128), lambda b, h, i: (b, h, 0, 0)),
                  kv_spec(0), kv_spec(1), kv_spec(2),
                  pl.BlockSpec(ovt.shape, lambda b, h, i: (0, 0))],
        out_specs=pl.BlockSpec((None, qblk, 256), lambda b, h, i: (b, i, h)),
        out_shape=jax.ShapeDtypeStruct((bx, tx, ATT_WIDTH), BF16),
        scratch_shapes=[pltpu.VMEM((tx // KEY_BLOCK, nq, KEY_BLOCK), F32), pltpu.VMEM((nq, 128), F32),
                        pltpu.VMEM((nq, 128), F32), pltpu.VMEM((nq, 64), F32)],
        compiler_params=_params(("arbitrary", "arbitrary", "arbitrary")),
        name="nsa_prompt",
    )(q_r, gates, kc, vc, ksa, khm, khm, khm, ovt)


NSA_SAMPLE_ROWS = 2


def _fold_heads(o256, hmask):
    o = o256 * hmask
    return o[:, 0:64] + o[:, 64:128] + o[:, 128:192] + o[:, 192:256]


def _nsa_sample_kernel(*refs, npages, nbb):
    refs = refs[1:]
    pages = refs[:npages * nbb]
    per_row = refs[npages * nbb:npages * nbb + 7]
    ov_ref, e_ref, o_ref, wout_ref = refs[npages * nbb + 7:]
    for bb in range(nbb):
        _nsa_sample_row(pages[bb * npages:(bb + 1) * npages], *[r.at[bb] for r in per_row], ov_ref, e_ref,
                        o_ref.at[bb], wout_ref.at[bb])


def _nsa_sample_row(pages, q_ref, g_ref, knew_ref, wnew_ref, kc_ref, vc_ref, win_ref, ov_ref, e_ref, o_ref, wout_ref):
    npages = len(pages)
    past = npages * PAGE_SIZE
    ncmp = kc_ref.shape[0]
    ntok = (past - CMP_LEN) // CMP_STRIDE + 1
    nsel = past // SEL_LEN + 1

    q16 = q_ref[...].astype(F32)
    rowh = lax.broadcasted_iota(jnp.int32, (16, 256), 0)
    laneh = lax.broadcasted_iota(jnp.int32, (16, 256), 1)
    hmask = (lax.shift_right_logical(rowh, 2) == lax.shift_right_logical(laneh, 6)).astype(F32)
    qbd_f = jnp.concatenate([q16] * 4, axis=1) * hmask
    qbd = qbd_f.astype(BF16)

    jl = lax.broadcasted_iota(jnp.int32, (16, ncmp), 1)
    mask_c = jl < ntok
    s = jnp.where(mask_c, _dot_nt(qbd, kc_ref[...]), NEG)
    e = jnp.where(mask_c, jnp.exp2(s - jnp.max(s, axis=-1, keepdims=True)), 0.0)
    p_c = e / jnp.maximum(jnp.sum(e, axis=-1, keepdims=True), TINY)
    o_c = _fold_heads(_dot(p_c.astype(BF16), vc_ref[...]), hmask)

    gm = (lax.shift_right_logical(lax.broadcasted_iota(jnp.int32, (8, 16), 1), 2)
          == lax.broadcasted_iota(jnp.int32, (8, 16), 0)).astype(F32)
    imp = _dot_hi(_dot_hi(gm, p_c), ov_ref[...])
    blk = lax.broadcasted_iota(jnp.int32, (8, 128), 1)
    exists = blk < nsel
    forced = (blk == 0) | (blk == nsel - 1) | (blk == nsel - 2)
    imp = jnp.where(exists, jnp.where(forced, BIG, imp), -BIG)
    rank = jnp.zeros((8, 128), F32)
    for j in range(nsel):
        cj = imp[:, j:j + 1]
        beats = (cj > imp) | ((cj == imp) & (blk > j))
        rank = rank + jnp.where(beats, 1.0, 0.0)
    sel8 = jnp.where((rank < float(min(SEL_TOPK, nsel))) & exists, 1.0, 0.0)
    gm_t = (lax.shift_right_logical(lax.broadcasted_iota(jnp.int32, (16, 8), 0), 2)
            == lax.broadcasted_iota(jnp.int32, (16, 8), 1)).astype(F32)
    sel16 = _dot(gm_t, sel8)
    bias_past = (_dot(sel16.astype(BF16), e_ref[...]) - 1.0) * BIG

    knew = knew_ref[...]
    s_new = jnp.sum(qbd_f * knew[:, 512:768], axis=-1, keepdims=True)
    hd = N_KV_HEADS * ATT_HEAD_DIM
    s_past = jnp.concatenate(
        [_dot(qbd, pg[0].reshape(hd, PAGE_SIZE).astype(BF16)) for pg in pages], axis=1) + bias_past
    m = jnp.maximum(jnp.max(s_past, axis=-1, keepdims=True), s_new)
    e_past = jnp.exp2(s_past - m)
    e_new = jnp.exp2(s_new - m)
    acc = e_new * knew[:, 768:1024]
    for p, pg in enumerate(pages):
        acc = acc + _dot_nt(e_past[:, p * 128:(p + 1) * 128].astype(BF16), pg[1].reshape(hd, PAGE_SIZE).astype(BF16))
    lsum = jnp.sum(e_past, axis=-1, keepdims=True) + e_new
    o_s = _fold_heads(acc / jnp.maximum(lsum, TINY), hmask)

    kw_t = win_ref[0].reshape(hd, WINDOW)
    vw_t = win_ref[1].reshape(hd, WINDOW)
    wnew = wnew_ref[...]
    wl = lax.broadcasted_iota(jnp.int32, (16, WINDOW), 1)
    s_w = jnp.where(wl >= 1, _dot(qbd, kw_t.astype(BF16)), NEG)
    s_wn = jnp.sum(qbd_f * wnew[:, 0:256], axis=-1, keepdims=True)
    mw = jnp.maximum(jnp.max(s_w, axis=-1, keepdims=True), s_wn)
    e_w = jnp.where(wl >= 1, jnp.exp2(s_w - mw), 0.0)
    e_wn = jnp.exp2(s_wn - mw)
    acc_w = _dot_nt(e_w.astype(BF16), vw_t.astype(BF16)) + e_wn * wnew[:, 256:512]
    o_w = _fold_heads(acc_w / jnp.maximum(jnp.sum(e_w, axis=-1, keepdims=True) + e_wn, TINY), hmask)

    g = _sig(g_ref[...])
    o_ref[...] = (g[:, 0:1] * o_c + g[:, 1:2] * o_s + g[:, 2:3] * o_w).astype(BF16)

    last = lax.broadcasted_iota(jnp.int32, (hd, 128), 1) == 127
    for i, src in enumerate((kw_t, vw_t)):
        col = jnp.broadcast_to(wnew[:, i * hd:(i + 1) * hd], (128, hd)).T
        rolled = pltpu.roll(src, WINDOW - 1, axis=1)
        out = jnp.concatenate([rolled[:, 0:WINDOW - 128], jnp.where(last, col, rolled[:, WINDOW - 128:])], axis=1)
        wout_ref[i] = out.reshape(N_KV_HEADS, ATT_HEAD_DIM, WINDOW)


def _nsa_sample(page_table, cache3, q16, g16, knew, wnew, kc, vc, win, ov, emat):
    nb, npages = page_table.shape
    ncmp = kc.shape[1]
    c2 = lambda shape: pl.BlockSpec(shape, lambda b, pt: (0,) * len(shape))
    nbb = NSA_SAMPLE_ROWS
    assert nb % nbb == 0
    page_specs = [pl.BlockSpec((None, 2, N_KV_HEADS, ATT_HEAD_DIM, PAGE_SIZE),
                               functools.partial(lambda b, pt, bb, p: (pt[b * nbb + bb, p], 1, 0, 0, 0), bb=bb, p=p))
                  for bb in range(nbb) for p in range(npages)]
    win_spec = pl.BlockSpec((nbb, 2, N_KV_HEADS, ATT_HEAD_DIM, WINDOW), lambda b, pt: (b, 0, 0, 0, 0))
    rows = lambda r, c: pl.BlockSpec((nbb, r, c), lambda b, pt: (b, 0, 0))
    grid_spec = pltpu.PrefetchScalarGridSpec(
        num_scalar_prefetch=1,
        grid=(nb // nbb,),
        in_specs=page_specs + [rows(16, 64), rows(16, 3), rows(1, 1024), rows(1, 512), rows(ncmp, 256),
                               rows(ncmp, 256), win_spec, c2(ov.shape), c2(emat.shape)],
        out_specs=[rows(16, 64), win_spec],
    )
    return pl.pallas_call(
        functools.partial(_nsa_sample_kernel, npages=npages, nbb=nbb),
        grid_spec=grid_spec,
        out_shape=[jax.ShapeDtypeStruct((nb, 16, 64), BF16),
                   jax.ShapeDtypeStruct((nb, 2, N_KV_HEADS, ATT_HEAD_DIM, WINDOW), F32)],
        compiler_params=_params(("arbitrary",)),
        name="nsa_sample",
    )(page_table, *([cache3] * (npages * nbb)), q16, g16, knew, wnew, kc, vc, win, ov, emat)


def _merge_kernel(x_ref, yssm_ref, yatt_ref, gs_ref, ga_ref, g1_ref, sh2_ref, sc2_ref, nw_ref,
                  wss_ref, wat_ref, wo_ref, x1_ref, h2_ref):
    ms = _dot(yssm_ref[...], wss_ref[...])
    ma = _dot(yatt_ref[...], wat_ref[...])
    merged = _sig(gs_ref[...]) * ms + _sig(ga_ref[...]) * ma
    x1 = x_ref[...] + g1_ref[...] * _dot(merged.astype(BF16), wo_ref[...])
    x1_ref[...] = x1
    y = x1 * lax.rsqrt(jnp.mean(x1 * x1, axis=-1, keepdims=True) + EPS)
    h2_ref[...] = (y * nw_ref[...] * (1.0 + sc2_ref[...]) + sh2_ref[...]).astype(BF16)


def _merge_out(x3, yssm, yatt, proj3, mod3, nw_row, wss, wat, wo, tm):
    bx, tx, _ = x3.shape
    r = mod3.shape[1]
    mod_b = (lambda b: b) if mod3.shape[0] == bx else (lambda b: 0)
    row = lambda w, cb: pl.BlockSpec((None, tm, w), lambda b, i: (b, i, cb))
    modc = lambda cb: pl.BlockSpec((None, r, D_MODEL), lambda b, i: (mod_b(b), 0, cb))
    const = lambda shape: pl.BlockSpec(shape, lambda b, i: (0, 0))
    return pl.pallas_call(
        _merge_kernel,
        grid=(bx, tx // tm),
        in_specs=[row(D_MODEL, 0), row(D_INNER, 0), row(ATT_WIDTH, 0),
                  row(1024, COL_GM // 1024), row(1024, COL_GM // 1024 + 1),
                  modc(2), modc(3), modc(4), const((1, D_MODEL)),
                  const(wss.shape), const(wat.shape), const(wo.shape)],
        out_specs=[row(D_MODEL, 0), row(D_MODEL, 0)],
        out_shape=[jax.ShapeDtypeStruct((bx, tx, D_MODEL), F32), jax.ShapeDtypeStruct((bx, tx, D_MODEL), BF16)],
        compiler_params=_params(("arbitrary", "arbitrary")),
        name="merge_out",
    )(x3, yssm, yatt, proj3, proj3, mod3, mod3, mod3, nw_row, wss, wat, wo)


FFN_TN = 1408
FFN_NT = D_FF // FFN_TN


def _ffn_up_prompt_kernel(h2_ref, wa_ref, wb_ref, cwa_ref, cwb_ref, cba_ref, cbb_ref,
                          act_ref, fa_ref, fb_ref, work, tails, *, tm, ntiles):
    i = pl.program_id(1)
    j = pl.program_id(2)
    h2 = h2_ref[...]
    u = jnp.concatenate([_dot(h2, wa_ref[...]), _dot(h2, wb_ref[...])], axis=1)
    work[8:8 + tm, :] = u

    @pl.when(i == 0)
    def _():
        work[0:8, :] = jnp.zeros((8, 2 * FFN_TN), F32)

    @pl.when(i > 0)
    def _():
        work[0:8, :] = tails[j]

    cw = jnp.concatenate([cwa_ref[...], cwb_ref[...]], axis=1)
    cb = jnp.concatenate([cba_ref[...], cbb_ref[...]], axis=1)
    conv = work[6:6 + tm, :] * cw[0:1, :] + work[7:7 + tm, :] * cw[1:2, :] + work[8:8 + tm, :] * cw[2:3, :] + cb
    act_ref[...] = (_silu(conv[:, 0:FFN_TN]) * conv[:, FFN_TN:2 * FFN_TN]).astype(BF16)
    tails[j] = work[tm:tm + 8, :]
    fa_ref[...] = work[tm + 6:tm + 8, 0:FFN_TN]
    fb_ref[...] = work[tm + 6:tm + 8, FFN_TN:2 * FFN_TN]


def _ffn_up_prompt(h2, wup, cw, cb_row, tm):
    bx, tx, _ = h2.shape
    ntiles = tx // tm
    return pl.pallas_call(
        functools.partial(_ffn_up_prompt_kernel, tm=tm, ntiles=ntiles),
        grid=(bx, ntiles, FFN_NT),
        in_specs=[pl.BlockSpec((None, tm, D_MODEL), lambda b, i, j: (b, i, 0)),
                  pl.BlockSpec((D_MODEL, FFN_TN), lambda b, i, j: (0, j)),
                  pl.BlockSpec((D_MODEL, FFN_TN), lambda b, i, j: (0, j + FFN_NT)),
                  pl.BlockSpec((FFN_CONV, FFN_TN), lambda b, i, j: (0, j)),
                  pl.BlockSpec((FFN_CONV, FFN_TN), lambda b, i, j: (0, j + FFN_NT)),
                  pl.BlockSpec((1, FFN_TN), lambda b, i, j: (0, j)),
                  pl.BlockSpec((1, FFN_TN), lambda b, i, j: (0, j + FFN_NT))],
        out_specs=[pl.BlockSpec((None, tm, FFN_TN), lambda b, i, j: (b, i, j)),
                   pl.BlockSpec((None, None, FFN_CONV - 1, FFN_TN), lambda b, i, j: (b, i, 0, j)),
                   pl.BlockSpec((None, None, FFN_CONV - 1, FFN_TN), lambda b, i, j: (b, i, 0, j))],
        out_shape=[jax.ShapeDtypeStruct((bx, tx, D_FF), BF16),
                   jax.ShapeDtypeStruct((bx, ntiles, FFN_CONV - 1, D_FF), F32),
                   jax.ShapeDtypeStruct((bx, ntiles, FFN_CONV - 1, D_FF), F32)],
        scratch_shapes=[pltpu.VMEM((tm + 8, 2 * FFN_TN), F32), pltpu.VMEM((FFN_NT, 8, 2 * FFN_TN), F32)],
        compiler_params=_params(("arbitrary", "arbitrary", "arbitrary")),
        name="ffn_up_prompt",
    )(h2, wup, wup, cw, cw, cb_row, cb_row)


def _ffn_up_sample_kernel(h2_ref, wa_ref, wb_ref, h0a_ref, h0b_ref, h1a_ref, h1b_ref, cwa_ref, cwb_ref,
                          cba_ref, cbb_ref, act_ref, ua_ref, ub_ref):
    h2 = h2_ref[...]
    ua = _dot(h2, wa_ref[...])
    ub = _dot(h2, wb_ref[...])
    cwa = cwa_ref[...]
    cwb = cwb_ref[...]
    ca = h0a_ref[...] * cwa[0:1, :] + h1a_ref[...] * cwa[1:2, :] + ua * cwa[2:3, :] + cba_ref[...]
    cb = h0b_ref[...] * cwb[0:1, :] + h1b_ref[...] * cwb[1:2, :] + ub * cwb[2:3, :] + cbb_ref[...]
    act_ref[...] = (_silu(ca) * cb).astype(BF16)
    ua_ref[...] = ua
    ub_ref[...] = ub


def _ffn_up_sample(h2, wup, hist0, hist1, cw, cb_row):
    nb = h2.shape[0]
    col = lambda rows, off: pl.BlockSpec((rows, FFN_TN), lambda j: (0, j + off))
    return pl.pallas_call(
        _ffn_up_sample_kernel,
        grid=(FFN_NT,),
        in_specs=[pl.BlockSpec((nb, D_MODEL), lambda j: (0, 0)),
                  col(D_MODEL, 0), col(D_MODEL, FFN_NT), col(nb, 0), col(nb, FFN_NT), col(nb, 0), col(nb, FFN_NT),
                  col(FFN_CONV, 0), col(FFN_CONV, FFN_NT), col(1, 0), col(1, FFN_NT)],
        out_specs=[col(nb, 0), col(nb, 0), col(nb, 0)],
        out_shape=[jax.ShapeDtypeStruct((nb, D_FF), BF16), jax.ShapeDtypeStruct((nb, D_FF), F32),
                   jax.ShapeDtypeStruct((nb, D_FF), F32)],
        compiler_params=_params(("arbitrary",)),
        name="ffn_up_sample",
    )(h2, wup, wup, hist0, hist0, hist1, hist1, cw, cw, cb_row, cb_row)


def _ffn_down_kernel(act_ref, x1_ref, g2_ref, nw_ref, w_ref, y_ref):
    x2 = x1_ref[...] + g2_ref[...] * _dot(act_ref[...], w_ref[...])
    y_ref[...] = x2 * lax.rsqrt(jnp.mean(x2 * x2, axis=-1, keepdims=True) + EPS) * nw_ref[...]


def _ffn_down(act, x1, mod3, nw_row, wdown, tm):
    bx, tx, _ = x1.shape
    r = mod3.shape[1]
    mod_b = (lambda b: b) if mod3.shape[0] == bx else (lambda b: 0)
    return pl.pallas_call(
        _ffn_down_kernel,
        grid=(bx, tx // tm),
        in_specs=[pl.BlockSpec((None, tm, D_FF), lambda b, i: (b, i, 0)),
                  pl.BlockSpec((None, tm, D_MODEL), lambda b, i: (b, i, 0)),
                  pl.BlockSpec((None, r, D_MODEL), lambda b, i: (mod_b(b), 0, 5)),
                  pl.BlockSpec((1, D_MODEL), lambda b, i: (0, 0)),
                  pl.BlockSpec((D_FF, D_MODEL), lambda b, i: (0, 0))],
        out_specs=pl.BlockSpec((None, tm, D_MODEL), lambda b, i: (b, i, 0)),
        out_shape=jax.ShapeDtypeStruct((bx, tx, D_MODEL), F32),
        compiler_params=_params(("arbitrary", "arbitrary")),
        name="ffn_down",
    )(act, x1, mod3, nw_row, wdown)


def _rope_tables(pos):
    half = ROPE_DIM // 2
    inv_freq = ROPE_THETA ** (-jnp.arange(half, dtype=F32) / half)
    ang = pos.astype(F32)[:, None] * inv_freq[None, :]
    cos, sin = jnp.cos(ang), jnp.sin(ang)
    n = pos.shape[0]
    ones = jnp.ones((n, ATT_HEAD_DIM - ROPE_DIM), F32)
    zeros8 = jnp.zeros((n, half), F32)
    zeros48 = jnp.zeros((n, ATT_HEAD_DIM - ROPE_DIM), F32)
    c = jnp.concatenate([cos, cos, ones], axis=1)
    s1 = jnp.concatenate([-sin, zeros8, zeros48], axis=1)
    s2 = jnp.concatenate([zeros8, sin, zeros48], axis=1)
    return tuple(jnp.concatenate([t, t], axis=1) for t in (c, s1, s2))


def _overlap(nc, ns):
    cst = np.arange(nc)[:, None] * CMP_STRIDE
    sst = np.arange(ns)[None, :] * SEL_LEN
    ov = np.clip(np.minimum(cst + CMP_LEN, sst + SEL_LEN) - np.maximum(cst, sst), 0, None)
    return (ov / CMP_STRIDE).astype(np.float32)


def _cmp_weights(pe, w1, w2):
    k16 = CMP_STRIDE * ATT_HEAD_DIM
    w1ab = jnp.concatenate([w1[:CMP_STRIDE].reshape(k16, CMP_HIDDEN), w1[CMP_STRIDE:].reshape(k16, CMP_HIDDEN)], axis=1)
    return (w1ab.astype(BF16), w1.reshape(CMP_LEN * ATT_HEAD_DIM, CMP_HIDDEN).astype(BF16), pe.reshape(1, -1),
            w2.astype(BF16))


def kernel(x_prompt, x_sample, c_prompt, c_sample, cache_nsa_kv, page_table, cache_win_kv, state_ssm, state_ssm_conv, state_ffn_conv, ada_w, ada_b, norm1_w, norm2_w, final_norm_w, w_in, ssm_conv_w, ssm_conv_b, ssm_dt_bias, ssm_A_log, ssm_D, ssm_norm_w, cmp_pe_k, cmp_w1_k, cmp_w2_k, cmp_pe_v, cmp_w1_v, cmp_w2_v, w_ssm_out, w_att_out, w_out, ffn_w_up, ffn_conv_w, ffn_conv_b, ffn_w_down):
    bp, tp, _ = x_prompt.shape
    nb = x_sample.shape[0]
    npages = page_table.shape[1]
    past = npages * PAGE_SIZE

    o_z, o_xbc, o_dt = 0, D_INNER, D_INNER + CONV_DIM
    o_q = o_dt + N_SSM_HEADS
    o_kv = o_q + ATT_WIDTH
    o_ag = o_kv + 6 * 256
    o_mg = o_ag + 3 * N_ATT_HEADS
    w_r = jnp.concatenate([
        w_in[:, o_z:o_z + 2048], w_in[:, o_xbc:o_xbc + 2048], w_in[:, o_q:o_q + 1024], w_in[:, o_mg:o_mg + 2048],
        w_in[:, o_xbc + 2048:o_xbc + 3072], w_in[:, o_kv:o_kv + 1536], w_in[:, o_dt:o_dt + 32],
        w_in[:, o_ag:o_ag + 48], jnp.zeros((D_MODEL, 48), F32)], axis=1).astype(BF16)
    ada_w_bf = ada_w.astype(BF16)
    wss = w_ssm_out.astype(BF16)
    wat = w_att_out.astype(BF16)
    wo = w_out.astype(BF16)
    wup = ffn_w_up.astype(BF16)
    wdown = ffn_w_down.astype(BF16)
    row = lambda v: v.reshape(1, -1)
    pad128 = lambda v: jnp.pad(v, (0, 128 - v.shape[0])).reshape(1, 128)
    dvec = jnp.repeat(ssm_D, SSM_HEAD_DIM).reshape(1, D_INNER)
    cmpw_k = _cmp_weights(cmp_pe_k, cmp_w1_k, cmp_w2_k)
    cmpw_v = _cmp_weights(cmp_pe_v, cmp_w1_v, cmp_w2_v)
    wbd, w1f, pe2, w2bd = (jnp.stack([a, b]) for a, b in zip(cmpw_k, cmpw_v))

    npad = (-(bp + nb)) % 8
    c_all = jnp.concatenate([c_prompt, c_sample, jnp.zeros((npad, D_MODEL), F32)], axis=0)
    mod = _ada_mod(c_all, ada_w_bf, row(ada_b))
    mod_p = mod[:bp].reshape(bp, 1, 6 * D_MODEL)
    mod_s = mod[bp:bp + nb].reshape(1, nb, 6 * D_MODEL)

    proj_p = _inproj(x_prompt, mod_p, row(norm1_w), w_r, 1024)
    yssm_p, conv_p, hlast_p = _ssd_prompt(proj_p, ssm_conv_w, row(ssm_conv_b), pad128(ssm_dt_bias),
                                          pad128(ssm_A_log), dvec, row(ssm_norm_w))
    tabs_p = _rope_tables(jnp.arange(tp, dtype=jnp.int32))
    q_p, kvrows_p, ksa_p, khm_p, gates_p, win_p = _rope_prep(proj_p, *tabs_p, 512)
    kc_p, vc_p = _compress_prompt(kvrows_p, wbd, w1f, pe2, w2bd)
    nblk = -(-tp // SEL_LEN)
    ncmp = tp // CMP_STRIDE
    ovt = np.zeros((nblk, ncmp), np.float32)
    ovt[:, :ncmp - 1] = _overlap(ncmp - 1, nblk).T
    yatt_p = _nsa_prompt(q_p, gates_p, kc_p, vc_p, ksa_p, khm_p, jnp.asarray(ovt))
    x1_p, h2_p = _merge_out(x_prompt, yssm_p, yatt_p, proj_p, mod_p, row(norm2_w), wss, wat, wo, 512)
    act_p, fa_p, fb_p = _ffn_up_prompt(h2_p, wup, ffn_conv_w, row(ffn_conv_b), 512)
    y_p = _ffn_down(act_p, x1_p, mod_p, row(final_norm_w), wdown, 512)

    x_s3 = x_sample.reshape(1, nb, D_MODEL)
    proj_s = _inproj(x_s3, mod_s, row(norm1_w), w_r, nb)
    eh = (np.arange(D_INNER)[None, :] // SSM_HEAD_DIM == np.arange(128)[:, None]).astype(np.float32)
    xdt, dec_e, yd, bm_c, cm_c, xdt_t, dec_t = _ssm_pre(
        proj_s, state_ssm_conv[:, 0], state_ssm_conv[:, 1], state_ssm_conv[:, 2], ssm_conv_w, row(ssm_conv_b),
        pad128(ssm_dt_bias), pad128(ssm_A_log), dvec, jnp.asarray(eh))
    h_new, yoff = _ssm_state(state_ssm.reshape(nb, D_INNER, SSM_STATE), xdt_t, dec_t, bm_c, cm_c)
    yssm_s = _ssm_post(yd, yoff[:, 0, :], dec_e, proj_s, row(ssm_norm_w))
    tabs_s = _rope_tables(jnp.full((nb,), past, dtype=jnp.int32))
    q_s, kvrows_s, _, _, _, wnew_s = _rope_prep(proj_s, *tabs_s, nb)
    cache3 = jnp.transpose(cache_nsa_kv, (0, 2, 3, 4, 1))
    win_t = jnp.transpose(cache_win_kv, (0, 2, 3, 4, 1))
    kc_s, vc_s = _compress_sample(page_table, cache3, wbd, w1f, pe2, w2bd)
    ncmp_s = past // CMP_STRIDE
    nsel_s = past // SEL_LEN + 1
    ov_s = np.zeros((ncmp_s, 128), np.float32)
    ov_s[:ncmp_s - 1, :nsel_s] = _overlap(ncmp_s - 1, nsel_s)
    e_s = (np.arange(past)[None, :] // SEL_LEN == np.arange(128)[:, None]).astype(np.float32)
    att_g_s = proj_s[0, :, COL_TAIL + 32:COL_TAIL + 80].reshape(nb, N_ATT_HEADS, 3)
    yatt_s16, win_s = _nsa_sample(
        page_table, cache3, q_s.reshape(nb, N_ATT_HEADS, ATT_HEAD_DIM), att_g_s, kvrows_s.reshape(nb, 1, 1024),
        wnew_s.reshape(nb, 1, 512), kc_s, vc_s, win_t, jnp.asarray(ov_s), jnp.asarray(e_s, dtype=BF16))
    x1_s, h2_s = _merge_out(x_s3, yssm_s.reshape(1, nb, D_INNER), yatt_s16.reshape(1, nb, ATT_WIDTH), proj_s, mod_s,
                            row(norm2_w), wss, wat, wo, nb)
    act_s, ua_s, ub_s = _ffn_up_sample(h2_s.reshape(nb, D_MODEL), wup, state_ffn_conv[:, 0], state_ffn_conv[:, 1],
                                       ffn_conv_w, row(ffn_conv_b))
    y_s = _ffn_down(act_s.reshape(1, nb, D_FF), x1_s, mod_s, row(final_norm_w), wdown, nb)

    xbc_s = jnp.concatenate([proj_s[0, :, COL_XS:COL_XS + 2048], proj_s[0, :, COL_BM:COL_BM + 1024]], axis=1)
    conv_s = jnp.stack([state_ssm_conv[:, 1], state_ssm_conv[:, 2], xbc_s], axis=1)
    ffn_s = jnp.stack([state_ffn_conv[:, 1], jnp.concatenate([ua_s, ub_s], axis=1)], axis=1)
    return (y_p,
            y_s.reshape(nb, 1, D_MODEL),
            kvrows_p.reshape(bp, tp, 4, N_KV_HEADS, ATT_HEAD_DIM),
            kvrows_s.reshape(nb, 1, 4, N_KV_HEADS, ATT_HEAD_DIM),
            win_p.reshape(bp, WINDOW, 2, N_KV_HEADS, ATT_HEAD_DIM),
            jnp.transpose(win_s, (0, 4, 1, 2, 3)),
            hlast_p.reshape(bp, N_SSM_HEADS, SSM_HEAD_DIM, SSM_STATE),
            h_new.reshape(nb, N_SSM_HEADS, SSM_HEAD_DIM, SSM_STATE),
            conv_p,
            conv_s,
            jnp.concatenate([fa_p[:, -1], fb_p[:, -1]], axis=2),
            ffn_s)
```

```python
import functools

import numpy as np
import jax
import jax.numpy as jnp
from jax import lax
from jax.experimental import pallas as pl
from jax.experimental.pallas import tpu as pltpu

F32 = jnp.float32
BF16 = jnp.bfloat16
HIGHEST = lax.Precision.HIGHEST

D_MODEL = 1024
D_INNER = 2048
N_SSM_HEADS = 32
SSM_HEAD_DIM = 64
SSM_STATE = 128
SSM_GROUPS = 4
SSM_CONV = 4
CONV_DIM = 3072
SSD_CHUNK = 128
N_ATT_HEADS = 16
ATT_HEAD_DIM = 64
N_KV_HEADS = 4
Q_PER_KV = 4
ATT_WIDTH = 1024
ATT_SCALE = ATT_HEAD_DIM ** -0.5
LOG2E = 1.4426950408889634
ROPE_DIM = 16
ROPE_THETA = 500000.0
CMP_LEN = 32
CMP_STRIDE = 16
CMP_HIDDEN = 128
SEL_LEN = 64
SEL_TOPK = 16
WINDOW = 512
Q_BLOCK = 128
PAGE_SIZE = 128
D_FF = 2816
FFN_CONV = 3
EPS = 1e-6
NEG = -1e30
BIG = 1e30
TINY = 1e-30

COL_Z = 0
COL_XS = 2048
COL_Q = 4096
COL_GM = 5120
COL_BM = 7168
COL_CM = 7680
COL_KV = 8192
COL_TAIL = 9728
N_PROJ = 9856
PROJ_TN = 1408
VMEM_LIMIT = 56 * 1024 * 1024


def _sig(x):
    return 1.0 / (1.0 + jnp.exp(-x))


def _silu(x):
    return x * _sig(x)


def _softplus(x):
    return jnp.maximum(x, 0.0) + jnp.log(1.0 + jnp.exp(-jnp.abs(x)))


def _dot(a, b):
    return jnp.dot(a, b, preferred_element_type=F32)


def _dot_hi(a, b):
    return jnp.dot(a, b, preferred_element_type=F32, precision=HIGHEST)


def _dot_nt(a, b):
    return lax.dot_general(a, b, (((1,), (1,)), ((), ())), preferred_element_type=F32)


def _dot_nt_hi(a, b):
    return lax.dot_general(a, b, (((1,), (1,)), ((), ())), preferred_element_type=F32, precision=HIGHEST)


def _params(sem):
    return pltpu.CompilerParams(dimension_semantics=sem, vmem_limit_bytes=VMEM_LIMIT)


def _ada_kernel(c_ref, w_ref, b_ref, o_ref):
    c = c_ref[...]
    o_ref[...] = _dot(_silu(c).astype(BF16), w_ref[...]) + b_ref[...]


def _ada_mod(c_all, w_bf, b_row):
    m = c_all.shape[0]
    tn = 512
    return pl.pallas_call(
        _ada_kernel,
        grid=(w_bf.shape[1] // tn,),
        in_specs=[pl.BlockSpec((m, D_MODEL), lambda j: (0, 0)),
                  pl.BlockSpec((D_MODEL, tn), lambda j: (0, j)),
                  pl.BlockSpec((1, tn), lambda j: (0, j))],
        out_specs=pl.BlockSpec((m, tn), lambda j: (0, j)),
        out_shape=jax.ShapeDtypeStruct((m, w_bf.shape[1]), F32),
        compiler_params=_params(("arbitrary",)),
        name="ada_mod",
    )(c_all, w_bf, b_row)


def _inproj_kernel(x_ref, sh_ref, sc_ref, nw_ref, w_ref, o_ref, h_scr):
    @pl.when(pl.program_id(2) == 0)
    def _():
        x = x_ref[...]
        y = x * lax.rsqrt(jnp.mean(x * x, axis=-1, keepdims=True) + EPS)
        h = y * nw_ref[...] * (1.0 + sc_ref[...]) + sh_ref[...]
        h_scr[...] = h.astype(BF16)

    o_ref[...] = _dot(h_scr[...], w_ref[...])


def _inproj(x3, mod3, nw_row, w_bf, tm):
    bx, tx, _ = x3.shape
    r = mod3.shape[1]
    mod_b = (lambda b: b) if mod3.shape[0] == bx else (lambda b: 0)
    return pl.pallas_call(
        _inproj_kernel,
        grid=(bx, tx // tm, N_PROJ // PROJ_TN),
        in_specs=[pl.BlockSpec((None, tm, D_MODEL), lambda b, i, j: (b, i, 0)),
                  pl.BlockSpec((None, r, D_MODEL), lambda b, i, j: (mod_b(b), 0, 0)),
                  pl.BlockSpec((None, r, D_MODEL), lambda b, i, j: (mod_b(b), 0, 1)),
                  pl.BlockSpec((1, D_MODEL), lambda b, i, j: (0, 0)),
                  pl.BlockSpec((D_MODEL, PROJ_TN), lambda b, i, j: (0, j))],
        out_specs=pl.BlockSpec((None, tm, PROJ_TN), lambda b, i, j: (b, i, j)),
        out_shape=jax.ShapeDtypeStruct((bx, tx, N_PROJ), F32),
        scratch_shapes=[pltpu.VMEM((tm, D_MODEL), BF16)],
        compiler_params=_params(("arbitrary", "arbitrary", "arbitrary")),
        name="inproj",
    )(x3, mod3, mod3, nw_row, w_bf)


def _ssd_kernel(z_ref, xs_ref, bm_ref, cm_ref, dt_ref, cw_ref, cb_ref, dtb_ref, alog_ref, dvec_ref, nw_ref,
                y_ref, conv_ref, hout_ref, xpad, h_t, *, nchunks):
    c = pl.program_id(1)
    ln = SSD_CHUNK

    @pl.when(c == 0)
    def _():
        xpad[0:8, :] = jnp.zeros((8, CONV_DIM), F32)
        h_t[...] = jnp.zeros_like(h_t)

    xpad[8:8 + ln, 0:2048] = xs_ref[...]
    xpad[8:8 + ln, 2048:2560] = bm_ref[...]
    xpad[8:8 + ln, 2560:3072] = cm_ref[...]
    cw = cw_ref[...]
    x_cur = xpad[8:8 + ln, :]
    tail = xpad[0:8, :]
    row8 = lax.broadcasted_iota(jnp.int32, (8, CONV_DIM), 0)
    conv = x_cur * cw[3:4, :] + cb_ref[...]
    for k in range(1, SSM_CONV):
        rolled = pltpu.roll(x_cur, k, axis=0)
        head = jnp.where(row8 < k, pltpu.roll(tail, k, axis=0), rolled[0:8])
        conv = conv + jnp.concatenate([head, rolled[8:]], axis=0) * cw[3 - k:4 - k, :]
    xc = _silu(conv)
    xs_c = xc[:, 0:2048]
    bm_c = xc[:, 2048:2560]
    cm_c = xc[:, 2560:3072]

    dt = _softplus(dt_ref[...] + dtb_ref[...])
    a = -jnp.exp(alog_ref[...])
    row = lax.broadcasted_iota(jnp.int32, (ln, ln), 0)
    col = lax.broadcasted_iota(jnp.int32, (ln, ln), 1)
    causal = row >= col
    cs = _dot_hi(causal.astype(F32), dt * a)
    cs_t = cs.T
    dt_t = dt.T

    x_bf = xs_c.astype(BF16)
    bm_bf = bm_c.astype(BF16)
    cm_bf = cm_c.astype(BF16)
    lo = lax.broadcasted_iota(jnp.int32, (1, 128), 1) < 64
    ys = []
    for g in range(SSM_GROUPS):
        bg = bm_bf[:, g * 128:(g + 1) * 128]
        cg = cm_bf[:, g * 128:(g + 1) * 128]
        cb = _dot_nt(cg, bg)
        b_t = bm_c[:, g * 128:(g + 1) * 128].T
        for r2 in range(4):
            pair = g * 4 + r2
            xp = x_bf[:, pair * 128:(pair + 1) * 128]
            yd, st = [], []
            for h in (2 * pair, 2 * pair + 1):
                cs_col = cs[:, h:h + 1]
                cs_row = cs_t[h:h + 1, :]
                dt_row = dt_t[h:h + 1, :]
                lmat = jnp.where(causal, jnp.exp(cs_col - cs_row), 0.0)
                yd.append(_dot((cb * lmat * dt_row).astype(BF16), xp))
                w_row = dt_row * jnp.exp(cs_t[h:h + 1, ln - 1:ln] - cs_row)
                st.append(_dot((b_t * w_row).astype(BF16), xp))
            ha, hb = 2 * pair, 2 * pair + 1
            ecol = jnp.where(lo, jnp.exp(cs[:, ha:ha + 1]), jnp.exp(cs[:, hb:hb + 1]))
            hprev = h_t[pair]
            yoff = _dot(cg, hprev.astype(BF16)) * ecol
            ys.append(jnp.where(lo, yd[0], yd[1]) + yoff)
            edec = jnp.where(lo, jnp.exp(cs_t[ha:ha + 1, ln - 1:ln]), jnp.exp(cs_t[hb:hb + 1, ln - 1:ln]))
            h_t[pair] = hprev * edec + jnp.where(lo, st[0], st[1])
    y = jnp.concatenate(ys, axis=1) + xs_c * dvec_ref[...]
    yz = y * _silu(z_ref[...])
    ms = jnp.mean(yz * yz, axis=-1, keepdims=True)
    y_ref[...] = (yz * lax.rsqrt(ms + EPS) * nw_ref[...]).astype(BF16)

    xpad[0:8, :] = xpad[ln:ln + 8, :]

    @pl.when(c == nchunks - 1)
    def _():
        conv_ref[...] = xpad[ln + 5:ln + 8, :]
        for pair in range(16):
            hout_ref[pair * 128:(pair + 1) * 128, :] = h_t[pair].T


def _ssd_prompt(proj3, cw, cb_row, dtb_row, alog_row, dvec_row, nw_row):
    bx, tx, _ = proj3.shape
    nchunks = tx // SSD_CHUNK
    ln = SSD_CHUNK
    const = lambda shape: pl.BlockSpec(shape, lambda b, c: (0, 0))
    return pl.pallas_call(
        functools.partial(_ssd_kernel, nchunks=nchunks),
        grid=(bx, nchunks),
        in_specs=[pl.BlockSpec((None, ln, 2048), lambda b, c: (b, c, COL_Z // 2048)),
                  pl.BlockSpec((None, ln, 2048), lambda b, c: (b, c, COL_XS // 2048)),
                  pl.BlockSpec((None, ln, 512), lambda b, c: (b, c, COL_BM // 512)),
                  pl.BlockSpec((None, ln, 512), lambda b, c: (b, c, COL_CM // 512)),
                  pl.BlockSpec((None, ln, 128), lambda b, c: (b, c, COL_TAIL // 128)),
                  const((SSM_CONV, CONV_DIM)), const((1, CONV_DIM)), const((1, 128)), const((1, 128)),
                  const((1, D_INNER)), const((1, D_INNER))],
        out_specs=[pl.BlockSpec((None, ln, D_INNER), lambda b, c: (b, c, 0)),
                   pl.BlockSpec((None, SSM_CONV - 1, CONV_DIM), lambda b, c: (b, 0, 0)),
                   pl.BlockSpec((None, N_SSM_HEADS * SSM_HEAD_DIM, SSM_STATE), lambda b, c: (b, 0, 0))],
        out_shape=[jax.ShapeDtypeStruct((bx, tx, D_INNER), BF16),
                   jax.ShapeDtypeStruct((bx, SSM_CONV - 1, CONV_DIM), F32),
                   jax.ShapeDtypeStruct((bx, N_SSM_HEADS * SSM_HEAD_DIM, SSM_STATE), F32)],
        scratch_shapes=[pltpu.VMEM((ln + 8, CONV_DIM), F32), pltpu.VMEM((16, 128, 128), F32)],
        compiler_params=_params(("arbitrary", "arbitrary")),
        name="ssd_prompt",
    )(proj3, proj3, proj3, proj3, proj3, cw, cb_row, dtb_row, alog_row, dvec_row, nw_row)


def _ssm_pre_kernel(xs_ref, bm_ref, cm_ref, dt_ref, s0_ref, s1_ref, s2_ref, cw_ref, cb_ref, dtb_ref, alog_ref,
                    dvec_ref, eh_ref, xdt_ref, dec_ref, yd_ref, bmc_ref, cmc_ref, xdt_t_ref, dec_t_ref):
    cw = cw_ref[...]
    xbc = jnp.concatenate([xs_ref[...], bm_ref[...], cm_ref[...]], axis=1)
    conv = s0_ref[...] * cw[0:1, :] + s1_ref[...] * cw[1:2, :] + s2_ref[...] * cw[2:3, :] + xbc * cw[3:4, :] + cb_ref[...]
    xc = _silu(conv)
    xs_c = xc[:, 0:2048]
    bm_c = xc[:, 2048:2560]
    cm_c = xc[:, 2560:3072]
    dt = _softplus(dt_ref[...] + dtb_ref[...])
    a = -jnp.exp(alog_ref[...])
    dec = jnp.exp(dt * a)
    eh = eh_ref[...]
    dt_e = _dot_hi(dt, eh)
    dec_e = _dot_hi(dec, eh)
    xdt = xs_c * dt_e
    cbs = []
    for g in range(SSM_GROUPS):
        cbg = jnp.sum(cm_c[:, g * 128:(g + 1) * 128] * bm_c[:, g * 128:(g + 1) * 128], axis=-1, keepdims=True)
        cbs.append(jnp.broadcast_to(cbg, (cbg.shape[0], 512)))
    cb_e = jnp.concatenate(cbs, axis=1)
    xdt_ref[...] = xdt
    dec_ref[...] = dec_e
    yd_ref[...] = cb_e * xdt + xs_c * dvec_ref[...]
    bmc_ref[...] = bm_c
    cmc_ref[...] = cm_c
    for k in range(16):
        xdt_t_ref[k * 128:(k + 1) * 128, :] = xdt[:, k * 128:(k + 1) * 128].T
        dec_t_ref[k * 128:(k + 1) * 128, :] = dec_e[:, k * 128:(k + 1) * 128].T


def _ssm_pre(proj3, s0, s1, s2, cw, cb_row, dtb_row, alog_row, dvec_row, eh):
    nb = proj3.shape[1]
    const = lambda shape: pl.BlockSpec(shape, lambda i: (0,) * len(shape))
    return pl.pallas_call(
        _ssm_pre_kernel,
        grid=(1,),
        in_specs=[pl.BlockSpec((None, nb, 2048), lambda i: (0, 0, COL_XS // 2048)),
                  pl.BlockSpec((None, nb, 512), lambda i: (0, 0, COL_BM // 512)),
                  pl.BlockSpec((None, nb, 512), lambda i: (0, 0, COL_CM // 512)),
                  pl.BlockSpec((None, nb, 128), lambda i: (0, 0, COL_TAIL // 128)),
                  const((nb, CONV_DIM)), const((nb, CONV_DIM)), const((nb, CONV_DIM)),
                  const((SSM_CONV, CONV_DIM)), const((1, CONV_DIM)), const((1, 128)), const((1, 128)),
                  const((1, D_INNER)), const((128, D_INNER))],
        out_specs=[const((nb, D_INNER)), const((nb, D_INNER)), const((nb, D_INNER)), const((nb, 512)),
                   const((nb, 512)), const((D_INNER, nb)), const((D_INNER, nb))],
        out_shape=[jax.ShapeDtypeStruct((nb, D_INNER), F32)] * 3 + [jax.ShapeDtypeStruct((nb, 512), F32)] * 2
        + [jax.ShapeDtypeStruct((D_INNER, nb), F32)] * 2,
        compiler_params=_params(("arbitrary",)),
        name="ssm_pre",
    )(proj3, proj3, proj3, proj3, s0, s1, s2, cw, cb_row, dtb_row, alog_row, dvec_row, eh)


SSM_STATE_ROWS = 4


def _ssm_state_kernel(h0_ref, xdt_t_ref, dec_t_ref, bm_ref, cm_ref, hn_ref, yoff_ref):
    shift = (128 - pl.program_id(0) * SSM_STATE_ROWS) & 127
    xdt_r = pltpu.roll(xdt_t_ref[...], shift, axis=1)
    dec_r = pltpu.roll(dec_t_ref[...], shift, axis=1)
    for bb in range(SSM_STATE_ROWS):
        h0 = h0_ref[bb]
        xcol = xdt_r[:, bb:bb + 1]
        outs, yoffs = [], []
        for g in range(SSM_GROUPS):
            rows = slice(g * 512, (g + 1) * 512)
            outs.append(xcol[rows] * bm_ref[bb, :, g * 128:(g + 1) * 128])
            cm8 = jnp.broadcast_to(cm_ref[bb, :, g * 128:(g + 1) * 128], (8, 128)).astype(BF16)
            yoffs.append(_dot_nt(cm8, h0[rows].astype(BF16)))
        hn_ref[bb] = h0 * dec_r[:, bb:bb + 1] + jnp.concatenate(outs, axis=0)
        yoff_ref[bb] = jnp.concatenate(yoffs, axis=1)


def _ssm_state(h0, xdt_t, dec_t, bm_c, cm_c):
    nb = h0.shape[0]
    nr = SSM_STATE_ROWS
    assert nb == 128 and nb % nr == 0
    const = lambda shape: pl.BlockSpec(shape, lambda b: (0, 0))
    return pl.pallas_call(
        _ssm_state_kernel,
        grid=(nb // nr,),
        in_specs=[pl.BlockSpec((nr, D_INNER, SSM_STATE), lambda b: (b, 0, 0)),
                  const((D_INNER, nb)), const((D_INNER, nb)),
                  pl.BlockSpec((nr, 1, 512), lambda b: (b, 0, 0)), pl.BlockSpec((nr, 1, 512), lambda b: (b, 0, 0))],
        out_specs=[pl.BlockSpec((nr, D_INNER, SSM_STATE), lambda b: (b, 0, 0)),
                   pl.BlockSpec((nr, 8, D_INNER), lambda b: (b, 0, 0))],
        out_shape=[jax.ShapeDtypeStruct((nb, D_INNER, SSM_STATE), F32),
                   jax.ShapeDtypeStruct((nb, 8, D_INNER), F32)],
        compiler_params=_params(("arbitrary",)),
        name="ssm_state",
    )(h0, xdt_t, dec_t, bm_c.reshape(nb, 1, 512), cm_c.reshape(nb, 1, 512))


def _ssm_post_kernel(yd_ref, yoff_ref, dec_ref, z_ref, nw_ref, y_ref):
    y = yd_ref[...] + yoff_ref[...] * dec_ref[...]
    yz = y * _silu(z_ref[...])
    ms = jnp.mean(yz * yz, axis=-1, keepdims=True)
    y_ref[...] = (yz * lax.rsqrt(ms + EPS) * nw_ref[...]).astype(BF16)


def _ssm_post(yd, yoff, dec_e, proj3, nw_row):
    nb = yd.shape[0]
    const = lambda shape: pl.BlockSpec(shape, lambda i: (0, 0))
    return pl.pallas_call(
        _ssm_post_kernel,
        grid=(1,),
        in_specs=[const((nb, D_INNER)), const((nb, D_INNER)), const((nb, D_INNER)),
                  pl.BlockSpec((None, nb, 2048), lambda i: (0, 0, COL_Z // 2048)), const((1, D_INNER))],
        out_specs=const((nb, D_INNER)),
        out_shape=jax.ShapeDtypeStruct((nb, D_INNER), BF16),
        compiler_params=_params(("arbitrary",)),
        name="ssm_post",
    )(yd, yoff, dec_e, proj3, nw_row)


def _rope128(x, c, s1, s2):
    return x * c + pltpu.roll(x, 120, axis=1) * s1 + pltpu.roll(x, 8, axis=1) * s2


def _rope_kernel(q_ref, k01_ref, k23_ref, k45_ref, tail_ref, c_ref, s1_ref, s2_ref,
                 qo_ref, kvrows_ref, ksa_ref, khm_ref, gates_ref, win_ref, *, ntiles, tr):
    c = c_ref[...]
    s1 = s1_ref[...]
    s2 = s2_ref[...]
    q = q_ref[...]
    qo_ref[...] = (jnp.concatenate(
        [_rope128(q[:, k * 128:(k + 1) * 128], c, s1, s2) for k in range(8)], axis=1
    ) * (ATT_SCALE * LOG2E)).astype(BF16)
    streams = []
    for pref in (k01_ref, k23_ref, k45_ref):
        blk = pref[...]
        kk = jnp.concatenate([_rope128(blk[:, k * 128:(k + 1) * 128], c, s1, s2) for k in range(2)], axis=1)
        streams.append(kk)
        streams.append(blk[:, 256:512])
    kvrows_ref[...] = jnp.concatenate(streams[0:4], axis=1)
    pos = pl.program_id(1) * tr + lax.broadcasted_iota(jnp.int32, (tr, 64), 0)
    own = lax.broadcasted_iota(jnp.int32, (tr, 64), 1) == lax.shift_right_logical(pos, 6)
    extra = jnp.where(own, NEG, 0.0).astype(BF16)
    ksel = streams[2].astype(BF16)
    vsel = streams[3].astype(BF16)
    ones_col = (lax.broadcasted_iota(jnp.int32, (tr, 64), 1) == 0).astype(BF16)
    for h in range(N_KV_HEADS):
        ksa_ref[h] = jnp.concatenate([ksel[:, h * 64:(h + 1) * 64], extra], axis=1)
        ksa_ref[N_KV_HEADS + h] = jnp.concatenate([vsel[:, h * 64:(h + 1) * 64], ones_col], axis=1)
    for i, s in enumerate((3, 4, 5)):
        sb = streams[s].astype(BF16)
        for h in range(N_KV_HEADS):
            khm_ref[i * 4 + h] = sb[:, h * 64:(h + 1) * 64]
    g = _sig(tail_ref[...])
    for hk in range(N_KV_HEADS):
        gates_ref[hk] = g[:, 32 + hk * 12:32 + (hk + 1) * 12]

    @pl.when(pl.program_id(1) == ntiles - 1)
    def _():
        win_ref[...] = jnp.concatenate(streams[4:6], axis=1)


def _rope_prep(proj3, ctab, s1tab, s2tab, tr):
    bx, tx, _ = proj3.shape
    ntiles = tx // tr
    tab = pl.BlockSpec((tr, 128), lambda b, i: (i, 0))
    return pl.pallas_call(
        functools.partial(_rope_kernel, ntiles=ntiles, tr=tr),
        grid=(bx, ntiles),
        in_specs=[pl.BlockSpec((None, tr, 1024), lambda b, i: (b, i, COL_Q // 1024)),
                  pl.BlockSpec((None, tr, 512), lambda b, i: (b, i, COL_KV // 512)),
                  pl.BlockSpec((None, tr, 512), lambda b, i: (b, i, COL_KV // 512 + 1)),
                  pl.BlockSpec((None, tr, 512), lambda b, i: (b, i, COL_KV // 512 + 2)),
                  pl.BlockSpec((None, tr, 128), lambda b, i: (b, i, COL_TAIL // 128)),
                  tab, tab, tab],
        out_specs=[pl.BlockSpec((None, tr, 1024), lambda b, i: (b, i, 0)),
                   pl.BlockSpec((None, tr, 1024), lambda b, i: (b, i, 0)),
                   pl.BlockSpec((None, 8, tr, 128), lambda b, i: (b, 0, i, 0)),
                   pl.BlockSpec((None, 12, tr, 64), lambda b, i: (b, 0, i, 0)),
                   pl.BlockSpec((None, 4, tr, 12), lambda b, i: (b, 0, i, 0)),
                   pl.BlockSpec((None, tr, 512), lambda b, i: (b, 0, 0))],
        out_shape=[jax.ShapeDtypeStruct((bx, tx, 1024), BF16),
                   jax.ShapeDtypeStruct((bx, tx, 1024), F32),
                   jax.ShapeDtypeStruct((bx, 8, tx, 128), BF16),
                   jax.ShapeDtypeStruct((bx, 12, tx, 64), BF16),
                   jax.ShapeDtypeStruct((bx, 4, tx, 12), F32),
                   jax.ShapeDtypeStruct((bx, tr, 512), F32)],
        compiler_params=_params(("arbitrary", "arbitrary")),
        name="rope_prep",
    )(proj3, proj3, proj3, proj3, proj3, ctab, s1tab, s2tab)


def _compress_kernel(*refs, nrefs, npages, head_major):
    if nrefs > 1:
        refs = refs[1:]
    page_refs = refs[:nrefs]
    w1ab_ref, w1f_ref, pe_ref, w2_ref, kc_ref, vc_ref, shift = refs[nrefs:]
    nsub = PAGE_SIZE // CMP_STRIDE
    nj = npages * nsub
    nr = N_KV_HEADS * nj
    shift[nr:nr + 8, :] = jnp.zeros((8, CMP_HIDDEN), F32)
    ri = lax.broadcasted_iota(jnp.int32, (PAGE_SIZE, PAGE_SIZE), 0)
    ci = lax.broadcasted_iota(jnp.int32, (PAGE_SIZE, PAGE_SIZE), 1)
    perm = (ci == (ri & (nsub - 1)) * CMP_STRIDE + lax.shift_right_logical(ri, 3)).astype(BF16)
    if nrefs == 1:
        pages = [page_refs[0][p * PAGE_SIZE:(p + 1) * PAGE_SIZE, :] for p in range(npages)]
        grouped = [_dot(perm, pg.astype(BF16)) for pg in pages]
    else:
        grouped = [_dot_nt(perm, pr[...].reshape(2 * N_KV_HEADS * ATT_HEAD_DIM, PAGE_SIZE).astype(BF16))
                   for pr in page_refs]
    for st, o_ref in ((0, kc_ref), (1, vc_ref)):
        heads = []
        for h in range(N_KV_HEADS):
            c0 = st * 256 + h * ATT_HEAD_DIM
            rows = [jnp.concatenate([y[s * nsub:(s + 1) * nsub, c0:c0 + ATT_HEAD_DIM] for s in range(CMP_STRIDE)],
                                    axis=1) for y in grouped]
            heads.append(rows[0] if npages == 1 else jnp.concatenate(rows, axis=0))
        a = jnp.concatenate(heads, axis=0).astype(BF16)
        p2 = _dot(a, w1ab_ref[st])
        shift[0:nr, :] = p2[:, CMP_HIDDEN:2 * CMP_HIDDEN]
        pe8 = jnp.broadcast_to(pe_ref[st], (8, CMP_LEN * ATT_HEAD_DIM)).astype(BF16)
        pe_t = _dot(pe8, w1f_ref[st])[0:1, :]
        hid = p2[:, 0:CMP_HIDDEN] + shift[pl.ds(1, nr), :] + pe_t
        out = _dot(_silu(hid).astype(BF16), w2_ref[st]).astype(BF16)
        if head_major:
            for h in range(N_KV_HEADS):
                o_ref[h] = out[h * nj:(h + 1) * nj, :]
        else:
            o_ref[...] = jnp.concatenate([out[h * nj:(h + 1) * nj, :] for h in range(N_KV_HEADS)], axis=1)


def _compress_prompt(kv_rows, wbd, w1f, pe, w2bd):
    bx, tx, _ = kv_rows.shape
    nj = tx // CMP_STRIDE
    c4 = lambda shape: pl.BlockSpec(shape, lambda b: (0,) * len(shape))
    return pl.pallas_call(
        functools.partial(_compress_kernel, nrefs=1, npages=tx // PAGE_SIZE, head_major=True),
        grid=(bx,),
        in_specs=[pl.BlockSpec((None, tx, 512), lambda b: (b, 0, 0)),
                  c4(wbd.shape), c4(w1f.shape), c4(pe.shape), c4(w2bd.shape)],
        out_specs=[pl.BlockSpec((None, 4, nj, 64), lambda b: (b, 0, 0, 0))] * 2,
        out_shape=[jax.ShapeDtypeStruct((bx, 4, nj, 64), BF16)] * 2,
        scratch_shapes=[pltpu.VMEM((N_KV_HEADS * nj + 8, CMP_HIDDEN), F32)],
        compiler_params=_params(("arbitrary",)),
        name="compress_prompt",
    )(kv_rows, wbd, w1f, pe, w2bd)


def _compress_sample(page_table, cache3, wbd, w1f, pe, w2bd):
    nb, npages = page_table.shape
    nsub = PAGE_SIZE // CMP_STRIDE
    nj = npages * nsub
    c4 = lambda shape: pl.BlockSpec(shape, lambda b, pt: (0,) * len(shape))
    page_specs = [pl.BlockSpec((None, 2, N_KV_HEADS, ATT_HEAD_DIM, PAGE_SIZE),
                               functools.partial(lambda b, pt, p: (pt[b, p], 0, 0, 0, 0), p=p))
                  for p in range(npages)]
    grid_spec = pltpu.PrefetchScalarGridSpec(
        num_scalar_prefetch=1,
        grid=(nb,),
        in_specs=page_specs + [c4(wbd.shape), c4(w1f.shape), c4(pe.shape), c4(w2bd.shape)],
        out_specs=[pl.BlockSpec((None, nj, 256), lambda b, pt: (b, 0, 0))] * 2,
        scratch_shapes=[pltpu.VMEM((N_KV_HEADS * nj + 8, CMP_HIDDEN), F32)],
    )
    return pl.pallas_call(
        functools.partial(_compress_kernel, nrefs=npages, npages=npages, head_major=False),
        grid_spec=grid_spec,
        out_shape=[jax.ShapeDtypeStruct((nb, nj, 256), BF16)] * 2,
        compiler_params=_params(("arbitrary",)),
        name="compress_sample",
    )(page_table, *([cache3] * npages), wbd, w1f, pe, w2bd)


KEY_BLOCK = 1024
NSA_QUERY_BLOCK = 256


def _nsa_prompt_kernel(q_ref, g_ref, kc_ref, vc_ref, ksa_ref, vs_ref, kw_ref, vw_ref, ovt_ref,
                       o_ref, s_scr, m_scr, acc_scr, *, qblk):
    qb = pl.program_id(2)
    q0 = qb * qblk
    q4 = q_ref[...]
    nq = Q_PER_KV * qblk
    qs = jnp.concatenate([q4[:, r * 64:(r + 1) * 64] for r in range(Q_PER_KV)], axis=0)
    ncmp = kc_ref.shape[0]
    trow = q0 + (lax.broadcasted_iota(jnp.int32, (nq, 1), 0) & (qblk - 1))

    jl = lax.broadcasted_iota(jnp.int32, (nq, ncmp), 1)
    mask_c = jl <= lax.shift_right_arithmetic(trow - (CMP_LEN - 1), CMP_STRIDE.bit_length() - 1)
    s = jnp.where(mask_c, _dot_nt(qs, kc_ref[...]), NEG)
    e = jnp.where(mask_c, jnp.exp2(s - jnp.max(s, axis=-1, keepdims=True)), 0.0)
    p = e / jnp.maximum(jnp.sum(e, axis=-1, keepdims=True), TINY)
    o_c = _dot(p.astype(BF16), vc_ref[...])
    psum = p[0:qblk] + p[qblk:2 * qblk] + p[2 * qblk:3 * qblk] + p[3 * qblk:4 * qblk]
    imp_t = _dot_nt_hi(ovt_ref[...], psum)
    nblk = imp_t.shape[0]
    tq = q0 + lax.broadcasted_iota(jnp.int32, (nblk, qblk), 1)
    blk = lax.broadcasted_iota(jnp.int32, (nblk, qblk), 0)
    cur = lax.shift_right_logical(tq, 6)
    valid = blk * SEL_LEN <= tq
    forced = (blk == 0) | (blk == cur) | (blk == cur - 1)
    imp = jnp.where(valid, jnp.where(forced, BIG, imp_t), -BIG)
    rank = jnp.zeros((nblk, qblk), F32)
    for j in range(nblk):
        rj = imp[j:j + 1, :]
        beats = (rj > imp) | ((rj == imp) & (blk > j))
        rank = rank + jnp.where(beats, 1.0, 0.0)
    nsel_t = jnp.where(rank < float(SEL_TOPK), 0.0, 1.0)
    nsel_pad = jnp.concatenate([nsel_t, jnp.zeros((128 - nblk, qblk), F32)], axis=0) if nblk < 128 else nsel_t
    nsel = jnp.concatenate([nsel_pad[:, i * 128:(i + 1) * 128].T for i in range(qblk // 128)],
                           axis=0)[:, 0:64].astype(BF16)
    qaug = jnp.concatenate([qs, jnp.concatenate([nsel] * Q_PER_KV, axis=0)], axis=1)

    wlen = WINDOW + qblk
    wstart = pl.multiple_of(jnp.maximum(q0 - WINDOW, 0), 128)
    kw = kw_ref[pl.ds(wstart, wlen), :]
    vw = vw_ref[pl.ds(wstart, wlen), :]
    d = trow[0:qblk] - (wstart + lax.broadcasted_iota(jnp.int32, (qblk, wlen), 1))
    ok = (d >= 0) & (d < WINDOW)
    o_w = []
    for r in range(Q_PER_KV):
        sw = jnp.where(ok, _dot_nt(qs[r * qblk:(r + 1) * qblk], kw), NEG)
        ew = jnp.exp2(sw - jnp.max(sw, axis=-1, keepdims=True))
        o_w.append(_dot(ew.astype(BF16), vw) / jnp.maximum(jnp.sum(ew, axis=-1, keepdims=True), TINY))

    cw = 256
    nck = KEY_BLOCK // cw

    def block_scores(jb, causal):
        base = pl.multiple_of(jb * KEY_BLOCK, KEY_BLOCK)
        mx = None
        for c in range(nck):
            sc = _dot_nt(qaug, ksa_ref[pl.ds(base + c * cw, cw), :])
            if causal:
                kpos = base + c * cw + lax.broadcasted_iota(jnp.int32, (nq, cw), 1)
                sc = jnp.where(kpos <= trow, sc, NEG)
            s_scr[jb, :, c * cw:(c + 1) * cw] = sc
            for i in range(cw // 128):
                part = sc[:, i * 128:(i + 1) * 128]
                mx = part if mx is None else jnp.maximum(mx, part)
        m_scr[...] = jnp.maximum(m_scr[...], mx)

    nfull = lax.shift_right_logical(qb, (KEY_BLOCK // qblk).bit_length() - 1)
    m_scr[...] = jnp.full((nq, 128), -jnp.inf, F32)

    def pass_a(jb, carry):
        block_scores(jb, False)
        return carry

    lax.fori_loop(0, nfull, pass_a, 0)
    block_scores(nfull, True)
    mb = jnp.broadcast_to(jnp.max(m_scr[...], axis=-1, keepdims=True), (nq, 128))

    acc_scr[...] = jnp.zeros((nq, 128), F32)

    def pass_b(jb, carry):
        base = pl.multiple_of(jb * KEY_BLOCK, KEY_BLOCK)
        ps = [jnp.exp2(s_scr[jb, :, c * 128:(c + 1) * 128] - mb).astype(BF16) for c in range(KEY_BLOCK // 128)]
        acc_scr[...] = acc_scr[...] + _dot(jnp.concatenate(ps, axis=1), vs_ref[pl.ds(base, KEY_BLOCK), :])
        return carry

    lax.fori_loop(0, nfull + 1, pass_b, 0)
    acc = acc_scr[...]
    o_s = acc[:, 0:64] / jnp.maximum(acc[:, 64:65], TINY)

    g = g_ref[...]
    outs = []
    for r in range(Q_PER_KV):
        rows = slice(r * qblk, (r + 1) * qblk)
        outs.append(g[:, 3 * r:3 * r + 1] * o_c[rows] + g[:, 3 * r + 1:3 * r + 2] * o_s[rows]
                    + g[:, 3 * r + 2:3 * r + 3] * o_w[r])
    o_ref[...] = jnp.concatenate(outs, axis=1).astype(BF16)


def _nsa_prompt(q_r, gates, kc, vc, ksa, khm, ovt):
    bx, tx, _ = q_r.shape
    qblk = NSA_QUERY_BLOCK
    ntiles = tx // qblk
    ncmp = kc.shape[2]
    nq = Q_PER_KV * qblk
    assert tx % KEY_BLOCK == 0 and tx >= WINDOW + qblk and tx <= 64 * SEL_LEN
    kv_spec = lambda s: pl.BlockSpec((None, None, tx, 64), lambda b, h, i: (b, s * 4 + h, 0, 0))
    return pl.pallas_call(
        functools.partial(_nsa_prompt_kernel, qblk=qblk),
        grid=(bx, N_KV_HEADS, ntiles),
        in_specs=[pl.BlockSpec((None, qblk, 256), lambda b, h, i: (b, i, h)),
                  pl.BlockSpec((None, None, qblk, 12), lambda b, h, i: (b, h, i, 0)),
                  pl.BlockSpec((None, None, ncmp, 64), lambda b, h, i: (b, h, 0, 0)),
                  pl.BlockSpec((None, None, ncmp, 64), lambda b, h, i: (b, h, 0, 0)),
                  pl.BlockSpec((None, None, tx, 128), lambda b, h, i: (b, h, 0, 0)),
                  pl.BlockSpec((None, None, tx, 128), lambda b, h, i: (b, N_KV_HEADS + h, 0, 0)),
                  kv_spec(1), kv_spec(2),
                  pl.BlockSpec(ovt.shape, lambda b, h, i: (0, 0))],
        out_specs=pl.BlockSpec((None, qblk, 256), lambda b, h, i: (b, i, h)),
        out_shape=jax.ShapeDtypeStruct((bx, tx, ATT_WIDTH), BF16),
        scratch_shapes=[pltpu.VMEM((tx // KEY_BLOCK, nq, KEY_BLOCK), F32), pltpu.VMEM((nq, 128), F32),
                        pltpu.VMEM((nq, 128), F32)],
        compiler_params=_params(("arbitrary", "arbitrary", "arbitrary")),
        name="nsa_prompt",
    )(q_r, gates, kc, vc, ksa, ksa, khm, khm, ovt)


NSA_SAMPLE_ROWS = 2


def _fold_heads(o256, hmask):
    o = o256 * hmask
    return o[:, 0:64] + o[:, 64:128] + o[:, 128:192] + o[:, 192:256]


def _nsa_sample_kernel(*refs, npages, nbb):
    refs = refs[1:]
    pages = refs[:npages * nbb]
    per_row = refs[npages * nbb:npages * nbb + 7]
    ov_ref, e_ref, o_ref, wout_ref = refs[npages * nbb + 7:]
    for bb in range(nbb):
        _nsa_sample_row(pages[bb * npages:(bb + 1) * npages], *[r.at[bb] for r in per_row], ov_ref, e_ref,
                        o_ref.at[bb], wout_ref.at[bb])


def _nsa_sample_row(pages, q_ref, g_ref, knew_ref, wnew_ref, kc_ref, vc_ref, win_ref, ov_ref, e_ref, o_ref, wout_ref):
    npages = len(pages)
    past = npages * PAGE_SIZE
    ncmp = kc_ref.shape[0]
    ntok = (past - CMP_LEN) // CMP_STRIDE + 1
    nsel = past // SEL_LEN + 1

    q16 = q_ref[...].astype(F32)
    rowh = lax.broadcasted_iota(jnp.int32, (16, 256), 0)
    laneh = lax.broadcasted_iota(jnp.int32, (16, 256), 1)
    hmask = (lax.shift_right_logical(rowh, 2) == lax.shift_right_logical(laneh, 6)).astype(F32)
    qbd_f = jnp.concatenate([q16] * 4, axis=1) * hmask
    qbd = qbd_f.astype(BF16)

    jl = lax.broadcasted_iota(jnp.int32, (16, ncmp), 1)
    mask_c = jl < ntok
    s = jnp.where(mask_c, _dot_nt(qbd, kc_ref[...]), NEG)
    e = jnp.where(mask_c, jnp.exp2(s - jnp.max(s, axis=-1, keepdims=True)), 0.0)
    p_c = e / jnp.maximum(jnp.sum(e, axis=-1, keepdims=True), TINY)
    o_c = _fold_heads(_dot(p_c.astype(BF16), vc_ref[...]), hmask)

    gm = (lax.shift_right_logical(lax.broadcasted_iota(jnp.int32, (8, 16), 1), 2)
          == lax.broadcasted_iota(jnp.int32, (8, 16), 0)).astype(F32)
    imp = _dot_hi(_dot_hi(gm, p_c), ov_ref[...])
    blk = lax.broadcasted_iota(jnp.int32, (8, 128), 1)
    exists = blk < nsel
    forced = (blk == 0) | (blk == nsel - 1) | (blk == nsel - 2)
    imp = jnp.where(exists, jnp.where(forced, BIG, imp), -BIG)
    rank = jnp.zeros((8, 128), F32)
    for j in range(nsel):
        cj = imp[:, j:j + 1]
        beats = (cj > imp) | ((cj == imp) & (blk > j))
        rank = rank + jnp.where(beats, 1.0, 0.0)
    sel8 = jnp.where((rank < float(min(SEL_TOPK, nsel))) & exists, 1.0, 0.0)
    gm_t = (lax.shift_right_logical(lax.broadcasted_iota(jnp.int32, (16, 8), 0), 2)
            == lax.broadcasted_iota(jnp.int32, (16, 8), 1)).astype(F32)
    sel16 = _dot(gm_t, sel8)
    bias_past = (_dot(sel16.astype(BF16), e_ref[...]) - 1.0) * BIG

    knew = knew_ref[...]
    s_new = jnp.sum(qbd_f * knew[:, 512:768], axis=-1, keepdims=True)
    hd = N_KV_HEADS * ATT_HEAD_DIM
    s_past = jnp.concatenate(
        [_dot(qbd, pg[0].reshape(hd, PAGE_SIZE).astype(BF16)) for pg in pages], axis=1) + bias_past
    m = jnp.maximum(jnp.max(s_past, axis=-1, keepdims=True), s_new)
    e_past = jnp.exp2(s_past - m)
    e_new = jnp.exp2(s_new - m)
    acc = e_new * knew[:, 768:1024]
    for p, pg in enumerate(pages):
        acc = acc + _dot_nt(e_past[:, p * 128:(p + 1) * 128].astype(BF16), pg[1].reshape(hd, PAGE_SIZE).astype(BF16))
    lsum = jnp.sum(e_past, axis=-1, keepdims=True) + e_new
    o_s = _fold_heads(acc / jnp.maximum(lsum, TINY), hmask)

    kw_t = win_ref[0].reshape(hd, WINDOW)
    vw_t = win_ref[1].reshape(hd, WINDOW)
    wnew = wnew_ref[...]
    wl = lax.broadcasted_iota(jnp.int32, (16, WINDOW), 1)
    s_w = jnp.where(wl >= 1, _dot(qbd, kw_t.astype(BF16)), NEG)
    s_wn = jnp.sum(qbd_f * wnew[:, 0:256], axis=-1, keepdims=True)
    mw = jnp.maximum(jnp.max(s_w, axis=-1, keepdims=True), s_wn)
    e_w = jnp.where(wl >= 1, jnp.exp2(s_w - mw), 0.0)
    e_wn = jnp.exp2(s_wn - mw)
    acc_w = _dot_nt(e_w.astype(BF16), vw_t.astype(BF16)) + e_wn * wnew[:, 256:512]
    o_w = _fold_heads(acc_w / jnp.maximum(jnp.sum(e_w, axis=-1, keepdims=True) + e_wn, TINY), hmask)

    g = _sig(g_ref[...])
    o_ref[...] = (g[:, 0:1] * o_c + g[:, 1:2] * o_s + g[:, 2:3] * o_w).astype(BF16)

    last = lax.broadcasted_iota(jnp.int32, (hd, 128), 1) == 127
    for i, src in enumerate((kw_t, vw_t)):
        col = jnp.broadcast_to(wnew[:, i * hd:(i + 1) * hd], (128, hd)).T
        rolled = pltpu.roll(src, WINDOW - 1, axis=1)
        out = jnp.concatenate([rolled[:, 0:WINDOW - 128], jnp.where(last, col, rolled[:, WINDOW - 128:])], axis=1)
        wout_ref[i] = out.reshape(N_KV_HEADS, ATT_HEAD_DIM, WINDOW)


def _nsa_sample(page_table, cache3, q16, g16, knew, wnew, kc, vc, win, ov, emat):
    nb, npages = page_table.shape
    ncmp = kc.shape[1]
    c2 = lambda shape: pl.BlockSpec(shape, lambda b, pt: (0,) * len(shape))
    nbb = NSA_SAMPLE_ROWS
    assert nb % nbb == 0
    page_specs = [pl.BlockSpec((None, 2, N_KV_HEADS, ATT_HEAD_DIM, PAGE_SIZE),
                               functools.partial(lambda b, pt, bb, p: (pt[b * nbb + bb, p], 1, 0, 0, 0), bb=bb, p=p))
                  for bb in range(nbb) for p in range(npages)]
    win_spec = pl.BlockSpec((nbb, 2, N_KV_HEADS, ATT_HEAD_DIM, WINDOW), lambda b, pt: (b, 0, 0, 0, 0))
    rows = lambda r, c: pl.BlockSpec((nbb, r, c), lambda b, pt: (b, 0, 0))
    grid_spec = pltpu.PrefetchScalarGridSpec(
        num_scalar_prefetch=1,
        grid=(nb // nbb,),
        in_specs=page_specs + [rows(16, 64), rows(16, 3), rows(1, 1024), rows(1, 512), rows(ncmp, 256),
                               rows(ncmp, 256), win_spec, c2(ov.shape), c2(emat.shape)],
        out_specs=[rows(16, 64), win_spec],
    )
    return pl.pallas_call(
        functools.partial(_nsa_sample_kernel, npages=npages, nbb=nbb),
        grid_spec=grid_spec,
        out_shape=[jax.ShapeDtypeStruct((nb, 16, 64), BF16),
                   jax.ShapeDtypeStruct((nb, 2, N_KV_HEADS, ATT_HEAD_DIM, WINDOW), F32)],
        compiler_params=_params(("arbitrary",)),
        name="nsa_sample",
    )(page_table, *([cache3] * (npages * nbb)), q16, g16, knew, wnew, kc, vc, win, ov, emat)


def _merge_kernel(x_ref, yssm_ref, yatt_ref, gs_ref, ga_ref, g1_ref, sh2_ref, sc2_ref, nw_ref,
                  wss_ref, wat_ref, wo_ref, x1_ref, h2_ref):
    ms = _dot(yssm_ref[...], wss_ref[...])
    ma = _dot(yatt_ref[...], wat_ref[...])
    merged = _sig(gs_ref[...]) * ms + _sig(ga_ref[...]) * ma
    x1 = x_ref[...] + g1_ref[...] * _dot(merged.astype(BF16), wo_ref[...])
    x1_ref[...] = x1
    y = x1 * lax.rsqrt(jnp.mean(x1 * x1, axis=-1, keepdims=True) + EPS)
    h2_ref[...] = (y * nw_ref[...] * (1.0 + sc2_ref[...]) + sh2_ref[...]).astype(BF16)


def _merge_out(x3, yssm, yatt, proj3, mod3, nw_row, wss, wat, wo, tm):
    bx, tx, _ = x3.shape
    r = mod3.shape[1]
    mod_b = (lambda b: b) if mod3.shape[0] == bx else (lambda b: 0)
    row = lambda w, cb: pl.BlockSpec((None, tm, w), lambda b, i: (b, i, cb))
    modc = lambda cb: pl.BlockSpec((None, r, D_MODEL), lambda b, i: (mod_b(b), 0, cb))
    const = lambda shape: pl.BlockSpec(shape, lambda b, i: (0, 0))
    return pl.pallas_call(
        _merge_kernel,
        grid=(bx, tx // tm),
        in_specs=[row(D_MODEL, 0), row(D_INNER, 0), row(ATT_WIDTH, 0),
                  row(1024, COL_GM // 1024), row(1024, COL_GM // 1024 + 1),
                  modc(2), modc(3), modc(4), const((1, D_MODEL)),
                  const(wss.shape), const(wat.shape), const(wo.shape)],
        out_specs=[row(D_MODEL, 0), row(D_MODEL, 0)],
        out_shape=[jax.ShapeDtypeStruct((bx, tx, D_MODEL), F32), jax.ShapeDtypeStruct((bx, tx, D_MODEL), BF16)],
        compiler_params=_params(("arbitrary", "arbitrary")),
        name="merge_out",
    )(x3, yssm, yatt, proj3, proj3, mod3, mod3, mod3, nw_row, wss, wat, wo)


FFN_TN = 1408
FFN_NT = D_FF // FFN_TN


def _ffn_up_prompt_kernel(h2_ref, wa_ref, wb_ref, cwa_ref, cwb_ref, cba_ref, cbb_ref,
                          act_ref, fa_ref, fb_ref, work, tails, *, tm, ntiles):
    i = pl.program_id(1)
    j = pl.program_id(2)
    h2 = h2_ref[...]
    u = jnp.concatenate([_dot(h2, wa_ref[...]), _dot(h2, wb_ref[...])], axis=1)
    work[8:8 + tm, :] = u

    @pl.when(i == 0)
    def _():
        work[0:8, :] = jnp.zeros((8, 2 * FFN_TN), F32)

    @pl.when(i > 0)
    def _():
        work[0:8, :] = tails[j]

    cw = jnp.concatenate([cwa_ref[...], cwb_ref[...]], axis=1)
    cb = jnp.concatenate([cba_ref[...], cbb_ref[...]], axis=1)
    conv = work[6:6 + tm, :] * cw[0:1, :] + work[7:7 + tm, :] * cw[1:2, :] + work[8:8 + tm, :] * cw[2:3, :] + cb
    act_ref[...] = (_silu(conv[:, 0:FFN_TN]) * conv[:, FFN_TN:2 * FFN_TN]).astype(BF16)
    tails[j] = work[tm:tm + 8, :]
    fa_ref[...] = work[tm + 6:tm + 8, 0:FFN_TN]
    fb_ref[...] = work[tm + 6:tm + 8, FFN_TN:2 * FFN_TN]


def _ffn_up_prompt(h2, wup, cw, cb_row, tm):
    bx, tx, _ = h2.shape
    ntiles = tx // tm
    return pl.pallas_call(
        functools.partial(_ffn_up_prompt_kernel, tm=tm, ntiles=ntiles),
        grid=(bx, ntiles, FFN_NT),
        in_specs=[pl.BlockSpec((None, tm, D_MODEL), lambda b, i, j: (b, i, 0)),
                  pl.BlockSpec((D_MODEL, FFN_TN), lambda b, i, j: (0, j)),
                  pl.BlockSpec((D_MODEL, FFN_TN), lambda b, i, j: (0, j + FFN_NT)),
                  pl.BlockSpec((FFN_CONV, FFN_TN), lambda b, i, j: (0, j)),
                  pl.BlockSpec((FFN_CONV, FFN_TN), lambda b, i, j: (0, j + FFN_NT)),
                  pl.BlockSpec((1, FFN_TN), lambda b, i, j: (0, j)),
                  pl.BlockSpec((1, FFN_TN), lambda b, i, j: (0, j + FFN_NT))],
        out_specs=[pl.BlockSpec((None, tm, FFN_TN), lambda b, i, j: (b, i, j)),
                   pl.BlockSpec((None, None, FFN_CONV - 1, FFN_TN), lambda b, i, j: (b, i, 0, j)),
                   pl.BlockSpec((None, None, FFN_CONV - 1, FFN_TN), lambda b, i, j: (b, i, 0, j))],
        out_shape=[jax.ShapeDtypeStruct((bx, tx, D_FF), BF16),
                   jax.ShapeDtypeStruct((bx, ntiles, FFN_CONV - 1, D_FF), F32),
                   jax.ShapeDtypeStruct((bx, ntiles, FFN_CONV - 1, D_FF), F32)],
        scratch_shapes=[pltpu.VMEM((tm + 8, 2 * FFN_TN), F32), pltpu.VMEM((FFN_NT, 8, 2 * FFN_TN), F32)],
        compiler_params=_params(("arbitrary", "arbitrary", "arbitrary")),
        name="ffn_up_prompt",
    )(h2, wup, wup, cw, cw, cb_row, cb_row)


def _ffn_up_sample_kernel(h2_ref, wa_ref, wb_ref, h0a_ref, h0b_ref, h1a_ref, h1b_ref, cwa_ref, cwb_ref,
                          cba_ref, cbb_ref, act_ref, ua_ref, ub_ref):
    h2 = h2_ref[...]
    ua = _dot(h2, wa_ref[...])
    ub = _dot(h2, wb_ref[...])
    cwa = cwa_ref[...]
    cwb = cwb_ref[...]
    ca = h0a_ref[...] * cwa[0:1, :] + h1a_ref[...] * cwa[1:2, :] + ua * cwa[2:3, :] + cba_ref[...]
    cb = h0b_ref[...] * cwb[0:1, :] + h1b_ref[...] * cwb[1:2, :] + ub * cwb[2:3, :] + cbb_ref[...]
    act_ref[...] = (_silu(ca) * cb).astype(BF16)
    ua_ref[...] = ua
    ub_ref[...] = ub


def _ffn_up_sample(h2, wup, hist0, hist1, cw, cb_row):
    nb = h2.shape[0]
    col = lambda rows, off: pl.BlockSpec((rows, FFN_TN), lambda j: (0, j + off))
    return pl.pallas_call(
        _ffn_up_sample_kernel,
        grid=(FFN_NT,),
        in_specs=[pl.BlockSpec((nb, D_MODEL), lambda j: (0, 0)),
                  col(D_MODEL, 0), col(D_MODEL, FFN_NT), col(nb, 0), col(nb, FFN_NT), col(nb, 0), col(nb, FFN_NT),
                  col(FFN_CONV, 0), col(FFN_CONV, FFN_NT), col(1, 0), col(1, FFN_NT)],
        out_specs=[col(nb, 0), col(nb, 0), col(nb, 0)],
        out_shape=[jax.ShapeDtypeStruct((nb, D_FF), BF16), jax.ShapeDtypeStruct((nb, D_FF), F32),
                   jax.ShapeDtypeStruct((nb, D_FF), F32)],
        compiler_params=_params(("arbitrary",)),
        name="ffn_up_sample",
    )(h2, wup, wup, hist0, hist0, hist1, hist1, cw, cw, cb_row, cb_row)


def _ffn_down_kernel(act_ref, x1_ref, g2_ref, nw_ref, w_ref, y_ref):
    x2 = x1_ref[...] + g2_ref[...] * _dot(act_ref[...], w_ref[...])
    y_ref[...] = x2 * lax.rsqrt(jnp.mean(x2 * x2, axis=-1, keepdims=True) + EPS) * nw_ref[...]


def _ffn_down(act, x1, mod3, nw_row, wdown, tm):
    bx, tx, _ = x1.shape
    r = mod3.shape[1]
    mod_b = (lambda b: b) if mod3.shape[0] == bx else (lambda b: 0)
    return pl.pallas_call(
        _ffn_down_kernel,
        grid=(bx, tx // tm),
        in_specs=[pl.BlockSpec((None, tm, D_FF), lambda b, i: (b, i, 0)),
                  pl.BlockSpec((None, tm, D_MODEL), lambda b, i: (b, i, 0)),
                  pl.BlockSpec((None, r, D_MODEL), lambda b, i: (mod_b(b), 0, 5)),
                  pl.BlockSpec((1, D_MODEL), lambda b, i: (0, 0)),
                  pl.BlockSpec((D_FF, D_MODEL), lambda b, i: (0, 0))],
        out_specs=pl.BlockSpec((None, tm, D_MODEL), lambda b, i: (b, i, 0)),
        out_shape=jax.ShapeDtypeStruct((bx, tx, D_MODEL), F32),
        compiler_params=_params(("arbitrary", "arbitrary")),
        name="ffn_down",
    )(act, x1, mod3, nw_row, wdown)


def _rope_tables(pos):
    half = ROPE_DIM // 2
    inv_freq = ROPE_THETA ** (-np.arange(half, dtype=np.float64) / half)
    ang = np.asarray(pos, np.float64)[:, None] * inv_freq[None, :]
    cos, sin = np.cos(ang).astype(np.float32), np.sin(ang).astype(np.float32)
    n = ang.shape[0]
    ones = np.ones((n, ATT_HEAD_DIM - ROPE_DIM), np.float32)
    zeros8 = np.zeros((n, half), np.float32)
    zeros48 = np.zeros((n, ATT_HEAD_DIM - ROPE_DIM), np.float32)
    c = np.concatenate([cos, cos, ones], axis=1)
    s1 = np.concatenate([-sin, zeros8, zeros48], axis=1)
    s2 = np.concatenate([zeros8, sin, zeros48], axis=1)
    return tuple(jnp.asarray(np.concatenate([t, t], axis=1)) for t in (c, s1, s2))


def _overlap(nc, ns):
    cst = np.arange(nc)[:, None] * CMP_STRIDE
    sst = np.arange(ns)[None, :] * SEL_LEN
    ov = np.clip(np.minimum(cst + CMP_LEN, sst + SEL_LEN) - np.maximum(cst, sst), 0, None)
    return (ov / CMP_STRIDE).astype(np.float32)


def _cmp_weights(pe, w1, w2):
    k16 = CMP_STRIDE * ATT_HEAD_DIM
    w1ab = jnp.concatenate([w1[:CMP_STRIDE].reshape(k16, CMP_HIDDEN), w1[CMP_STRIDE:].reshape(k16, CMP_HIDDEN)], axis=1)
    return (w1ab.astype(BF16), w1.reshape(CMP_LEN * ATT_HEAD_DIM, CMP_HIDDEN).astype(BF16), pe.reshape(1, -1),
            w2.astype(BF16))


def kernel(x_prompt, x_sample, c_prompt, c_sample, cache_nsa_kv, page_table, cache_win_kv, state_ssm, state_ssm_conv, state_ffn_conv, ada_w, ada_b, norm1_w, norm2_w, final_norm_w, w_in, ssm_conv_w, ssm_conv_b, ssm_dt_bias, ssm_A_log, ssm_D, ssm_norm_w, cmp_pe_k, cmp_w1_k, cmp_w2_k, cmp_pe_v, cmp_w1_v, cmp_w2_v, w_ssm_out, w_att_out, w_out, ffn_w_up, ffn_conv_w, ffn_conv_b, ffn_w_down):
    bp, tp, _ = x_prompt.shape
    nb = x_sample.shape[0]
    npages = page_table.shape[1]
    past = npages * PAGE_SIZE

    o_z, o_xbc, o_dt = 0, D_INNER, D_INNER + CONV_DIM
    o_q = o_dt + N_SSM_HEADS
    o_kv = o_q + ATT_WIDTH
    o_ag = o_kv + 6 * 256
    o_mg = o_ag + 3 * N_ATT_HEADS
    w_r = jnp.concatenate([
        w_in[:, o_z:o_z + 2048], w_in[:, o_xbc:o_xbc + 2048], w_in[:, o_q:o_q + 1024], w_in[:, o_mg:o_mg + 2048],
        w_in[:, o_xbc + 2048:o_xbc + 3072], w_in[:, o_kv:o_kv + 1536], w_in[:, o_dt:o_dt + 32],
        w_in[:, o_ag:o_ag + 48], jnp.zeros((D_MODEL, 48), F32)], axis=1).astype(BF16)
    ada_w_bf = ada_w.astype(BF16)
    wss = w_ssm_out.astype(BF16)
    wat = w_att_out.astype(BF16)
    wo = w_out.astype(BF16)
    wup = ffn_w_up.astype(BF16)
    wdown = ffn_w_down.astype(BF16)
    row = lambda v: v.reshape(1, -1)
    pad128 = lambda v: jnp.pad(v, (0, 128 - v.shape[0])).reshape(1, 128)
    dvec = jnp.repeat(ssm_D, SSM_HEAD_DIM).reshape(1, D_INNER)
    cmpw_k = _cmp_weights(cmp_pe_k, cmp_w1_k, cmp_w2_k)
    cmpw_v = _cmp_weights(cmp_pe_v, cmp_w1_v, cmp_w2_v)
    wbd, w1f, pe2, w2bd = (jnp.stack([a, b]) for a, b in zip(cmpw_k, cmpw_v))

    npad = (-(bp + nb)) % 8
    c_all = jnp.concatenate([c_prompt, c_sample, jnp.zeros((npad, D_MODEL), F32)], axis=0)
    mod = _ada_mod(c_all, ada_w_bf, row(ada_b))
    mod_p = mod[:bp].reshape(bp, 1, 6 * D_MODEL)
    mod_s = mod[bp:bp + nb].reshape(1, nb, 6 * D_MODEL)

    proj_p = _inproj(x_prompt, mod_p, row(norm1_w), w_r, 1024)
    yssm_p, conv_p, hlast_p = _ssd_prompt(proj_p, ssm_conv_w, row(ssm_conv_b), pad128(ssm_dt_bias),
                                          pad128(ssm_A_log), dvec, row(ssm_norm_w))
    tabs_p = _rope_tables(np.arange(tp))
    q_p, kvrows_p, ksa_p, khm_p, gates_p, win_p = _rope_prep(proj_p, *tabs_p, 512)
    kc_p, vc_p = _compress_prompt(kvrows_p, wbd, w1f, pe2, w2bd)
    nblk = -(-tp // SEL_LEN)
    ncmp = tp // CMP_STRIDE
    ovt = np.zeros((nblk, ncmp), np.float32)
    ovt[:, :ncmp - 1] = _overlap(ncmp - 1, nblk).T
    yatt_p = _nsa_prompt(q_p, gates_p, kc_p, vc_p, ksa_p, khm_p, jnp.asarray(ovt))
    x1_p, h2_p = _merge_out(x_prompt, yssm_p, yatt_p, proj_p, mod_p, row(norm2_w), wss, wat, wo, 512)
    act_p, fa_p, fb_p = _ffn_up_prompt(h2_p, wup, ffn_conv_w, row(ffn_conv_b), 512)
    y_p = _ffn_down(act_p, x1_p, mod_p, row(final_norm_w), wdown, 512)

    x_s3 = x_sample.reshape(1, nb, D_MODEL)
    proj_s = _inproj(x_s3, mod_s, row(norm1_w), w_r, nb)
    eh = (np.arange(D_INNER)[None, :] // SSM_HEAD_DIM == np.arange(128)[:, None]).astype(np.float32)
    xdt, dec_e, yd, bm_c, cm_c, xdt_t, dec_t = _ssm_pre(
        proj_s, state_ssm_conv[:, 0], state_ssm_conv[:, 1], state_ssm_conv[:, 2], ssm_conv_w, row(ssm_conv_b),
        pad128(ssm_dt_bias), pad128(ssm_A_log), dvec, jnp.asarray(eh))
    h_new, yoff = _ssm_state(state_ssm.reshape(nb, D_INNER, SSM_STATE), xdt_t, dec_t, bm_c, cm_c)
    yssm_s = _ssm_post(yd, yoff[:, 0, :], dec_e, proj_s, row(ssm_norm_w))
    tabs_s = _rope_tables(np.full((nb,), past))
    q_s, kvrows_s, _, _, _, wnew_s = _rope_prep(proj_s, *tabs_s, nb)
    cache3 = jnp.transpose(cache_nsa_kv, (0, 2, 3, 4, 1))
    win_t = jnp.transpose(cache_win_kv, (0, 2, 3, 4, 1))
    kc_s, vc_s = _compress_sample(page_table, cache3, wbd, w1f, pe2, w2bd)
    ncmp_s = past // CMP_STRIDE
    nsel_s = past // SEL_LEN + 1
    ov_s = np.zeros((ncmp_s, 128), np.float32)
    ov_s[:ncmp_s - 1, :nsel_s] = _overlap(ncmp_s - 1, nsel_s)
    e_s = (np.arange(past)[None, :] // SEL_LEN == np.arange(128)[:, None]).astype(np.float32)
    att_g_s = proj_s[0, :, COL_TAIL + 32:COL_TAIL + 80].reshape(nb, N_ATT_HEADS, 3)
    yatt_s16, win_s = _nsa_sample(
        page_table, cache3, q_s.reshape(nb, N_ATT_HEADS, ATT_HEAD_DIM), att_g_s, kvrows_s.reshape(nb, 1, 1024),
        wnew_s.reshape(nb, 1, 512), kc_s, vc_s, win_t, jnp.asarray(ov_s), jnp.asarray(e_s, dtype=BF16))
    x1_s, h2_s = _merge_out(x_s3, yssm_s.reshape(1, nb, D_INNER), yatt_s16.reshape(1, nb, ATT_WIDTH), proj_s, mod_s,
                            row(norm2_w), wss, wat, wo, nb)
    act_s, ua_s, ub_s = _ffn_up_sample(h2_s.reshape(nb, D_MODEL), wup, state_ffn_conv[:, 0], state_ffn_conv[:, 1],
                                       ffn_conv_w, row(ffn_conv_b))
    y_s = _ffn_down(act_s.reshape(1, nb, D_FF), x1_s, mod_s, row(final_norm_w), wdown, nb)

    xbc_s = jnp.concatenate([proj_s[0, :, COL_XS:COL_XS + 2048], proj_s[0, :, COL_BM:COL_BM + 1024]], axis=1)
    conv_s = jnp.stack([state_ssm_conv[:, 1], state_ssm_conv[:, 2], xbc_s], axis=1)
    ffn_s = jnp.stack([state_ffn_conv[:, 1], jnp.concatenate([ua_s, ub_s], axis=1)], axis=1)
    return (y_p,
            y_s.reshape(nb, 1, D_MODEL),
            kvrows_p.reshape(bp, tp, 4, N_KV_HEADS, ATT_HEAD_DIM),
            kvrows_s.reshape(nb, 1, 4, N_KV_HEADS, ATT_HEAD_DIM),
            win_p.reshape(bp, WINDOW, 2, N_KV_HEADS, ATT_HEAD_DIM),
            jnp.transpose(win_s, (0, 4, 1, 2, 3)),
            hlast_p.reshape(bp, N_SSM_HEADS, SSM_HEAD_DIM, SSM_STATE),
            h_new.reshape(nb, N_SSM_HEADS, SSM_HEAD_DIM, SSM_STATE),
            conv_p,
            conv_s,
            jnp.concatenate([fa_p[:, -1], fb_p[:, -1]], axis=2),
            ffn_s)
```

```python
import functools

import numpy as np
import jax
import jax.numpy as jnp
from jax import lax
from jax.experimental import pallas as pl
from jax.experimental.pallas import tpu as pltpu

F32 = jnp.float32
BF16 = jnp.bfloat16
HIGHEST = lax.Precision.HIGHEST

D_MODEL = 1024
D_INNER = 2048
N_SSM_HEADS = 32
SSM_HEAD_DIM = 64
SSM_STATE = 128
SSM_GROUPS = 4
SSM_CONV = 4
CONV_DIM = 3072
SSD_CHUNK = 128
N_ATT_HEADS = 16
ATT_HEAD_DIM = 64
N_KV_HEADS = 4
Q_PER_KV = 4
ATT_WIDTH = 1024
ATT_SCALE = ATT_HEAD_DIM ** -0.5
LOG2E = 1.4426950408889634
ROPE_DIM = 16
ROPE_THETA = 500000.0
CMP_LEN = 32
CMP_STRIDE = 16
CMP_HIDDEN = 128
SEL_LEN = 64
SEL_TOPK = 16
WINDOW = 512
Q_BLOCK = 128
PAGE_SIZE = 128
D_FF = 2816
FFN_CONV = 3
EPS = 1e-6
NEG = -1e30
BIG = 1e30
TINY = 1e-30

COL_Z = 0
COL_XS = 2048
COL_Q = 4096
COL_GM = 5120
COL_BM = 7168
COL_CM = 7680
COL_KV = 8192
COL_TAIL = 9728
N_PROJ = 9856
PROJ_TN = 1408
VMEM_LIMIT = 56 * 1024 * 1024


def _sig(x):
    return 1.0 / (1.0 + jnp.exp(-x))


def _silu(x):
    return x * _sig(x)


def _softplus(x):
    return jnp.maximum(x, 0.0) + jnp.log(1.0 + jnp.exp(-jnp.abs(x)))


def _dot(a, b):
    return jnp.dot(a, b, preferred_element_type=F32)


def _dot_hi(a, b):
    return jnp.dot(a, b, preferred_element_type=F32, precision=HIGHEST)


def _dot_nt(a, b):
    return lax.dot_general(a, b, (((1,), (1,)), ((), ())), preferred_element_type=F32)


def _dot_nt_hi(a, b):
    return lax.dot_general(a, b, (((1,), (1,)), ((), ())), preferred_element_type=F32, precision=HIGHEST)


def _params(sem):
    return pltpu.CompilerParams(dimension_semantics=sem, vmem_limit_bytes=VMEM_LIMIT)


def _ada_kernel(c_ref, w_ref, b_ref, o_ref):
    c = c_ref[...]
    o_ref[...] = _dot(_silu(c).astype(BF16), w_ref[...]) + b_ref[...]


def _ada_mod(c_all, w_bf, b_row):
    m = c_all.shape[0]
    tn = 512
    return pl.pallas_call(
        _ada_kernel,
        grid=(w_bf.shape[1] // tn,),
        in_specs=[pl.BlockSpec((m, D_MODEL), lambda j: (0, 0)),
                  pl.BlockSpec((D_MODEL, tn), lambda j: (0, j)),
                  pl.BlockSpec((1, tn), lambda j: (0, j))],
        out_specs=pl.BlockSpec((m, tn), lambda j: (0, j)),
        out_shape=jax.ShapeDtypeStruct((m, w_bf.shape[1]), F32),
        compiler_params=_params(("arbitrary",)),
        name="ada_mod",
    )(c_all, w_bf, b_row)


def _inproj_kernel(x_ref, sh_ref, sc_ref, nw_ref, w_ref, o_ref, h_scr):
    @pl.when(pl.program_id(2) == 0)
    def _():
        x = x_ref[...]
        y = x * lax.rsqrt(jnp.mean(x * x, axis=-1, keepdims=True) + EPS)
        h = y * nw_ref[...] * (1.0 + sc_ref[...]) + sh_ref[...]
        h_scr[...] = h.astype(BF16)

    o_ref[...] = _dot(h_scr[...], w_ref[...])


def _inproj(x3, mod3, nw_row, w_bf, tm):
    bx, tx, _ = x3.shape
    r = mod3.shape[1]
    mod_b = (lambda b: b) if mod3.shape[0] == bx else (lambda b: 0)
    return pl.pallas_call(
        _inproj_kernel,
        grid=(bx, tx // tm, N_PROJ // PROJ_TN),
        in_specs=[pl.BlockSpec((None, tm, D_MODEL), lambda b, i, j: (b, i, 0)),
                  pl.BlockSpec((None, r, D_MODEL), lambda b, i, j: (mod_b(b), 0, 0)),
                  pl.BlockSpec((None, r, D_MODEL), lambda b, i, j: (mod_b(b), 0, 1)),
                  pl.BlockSpec((1, D_MODEL), lambda b, i, j: (0, 0)),
                  pl.BlockSpec((D_MODEL, PROJ_TN), lambda b, i, j: (0, j))],
        out_specs=pl.BlockSpec((None, tm, PROJ_TN), lambda b, i, j: (b, i, j)),
        out_shape=jax.ShapeDtypeStruct((bx, tx, N_PROJ), F32),
        scratch_shapes=[pltpu.VMEM((tm, D_MODEL), BF16)],
        compiler_params=_params(("arbitrary", "arbitrary", "arbitrary")),
        name="inproj",
    )(x3, mod3, mod3, nw_row, w_bf)


def _ssd_kernel(z_ref, xs_ref, bm_ref, cm_ref, dt_ref, cw_ref, cb_ref, dtb_ref, alog_ref, dvec_ref, nw_ref,
                y_ref, conv_ref, hout_ref, xpad, h_t, *, nchunks):
    c = pl.program_id(1)
    ln = SSD_CHUNK

    @pl.when(c == 0)
    def _():
        xpad[...] = jnp.zeros((8, CONV_DIM), F32)
        h_t[...] = jnp.zeros_like(h_t)

    cw = cw_ref[...]
    x_cur = jnp.concatenate([xs_ref[...], bm_ref[...], cm_ref[...]], axis=1)
    tail = xpad[...]
    row8 = lax.broadcasted_iota(jnp.int32, (8, CONV_DIM), 0)
    conv = x_cur * cw[3:4, :] + cb_ref[...]
    for k in range(1, SSM_CONV):
        rolled = pltpu.roll(x_cur, k, axis=0)
        head = jnp.where(row8 < k, pltpu.roll(tail, k, axis=0), rolled[0:8])
        conv = conv + jnp.concatenate([head, rolled[8:]], axis=0) * cw[3 - k:4 - k, :]
    xc = _silu(conv)
    xs_c = xc[:, 0:2048]
    bm_c = xc[:, 2048:2560]
    cm_c = xc[:, 2560:3072]

    dt = _softplus(dt_ref[...] + dtb_ref[...])
    a = -jnp.exp(alog_ref[...])
    row = lax.broadcasted_iota(jnp.int32, (ln, ln), 0)
    col = lax.broadcasted_iota(jnp.int32, (ln, ln), 1)
    causal = row >= col
    cs = _dot_hi(causal.astype(F32), dt * a)
    cs_t = cs.T
    dt_t = dt.T

    x_bf = xs_c.astype(BF16)
    bm_bf = bm_c.astype(BF16)
    cm_bf = cm_c.astype(BF16)
    lo = lax.broadcasted_iota(jnp.int32, (1, 128), 1) < 64
    ys = []
    for g in range(SSM_GROUPS):
        bg = bm_bf[:, g * 128:(g + 1) * 128]
        cg = cm_bf[:, g * 128:(g + 1) * 128]
        cb = _dot_nt(cg, bg)
        b_t = bm_c[:, g * 128:(g + 1) * 128].T
        for r2 in range(4):
            pair = g * 4 + r2
            xp = x_bf[:, pair * 128:(pair + 1) * 128]
            yd, st = [], []
            for h in (2 * pair, 2 * pair + 1):
                cs_col = cs[:, h:h + 1]
                cs_row = cs_t[h:h + 1, :]
                dt_row = dt_t[h:h + 1, :]
                lmat = jnp.where(causal, jnp.exp(cs_col - cs_row), 0.0)
                yd.append(_dot((cb * lmat * dt_row).astype(BF16), xp))
                w_row = dt_row * jnp.exp(cs_t[h:h + 1, ln - 1:ln] - cs_row)
                st.append(_dot((b_t * w_row).astype(BF16), xp))
            ha, hb = 2 * pair, 2 * pair + 1
            ecol = jnp.where(lo, jnp.exp(cs[:, ha:ha + 1]), jnp.exp(cs[:, hb:hb + 1]))
            hprev = h_t[pair]
            yoff = _dot(cg, hprev.astype(BF16)) * ecol
            ys.append(jnp.where(lo, yd[0], yd[1]) + yoff)
            edec = jnp.where(lo, jnp.exp(cs_t[ha:ha + 1, ln - 1:ln]), jnp.exp(cs_t[hb:hb + 1, ln - 1:ln]))
            h_t[pair] = hprev * edec + jnp.where(lo, st[0], st[1])
    y = jnp.concatenate(ys, axis=1) + xs_c * dvec_ref[...]
    yz = y * _silu(z_ref[...])
    ms = jnp.mean(yz * yz, axis=-1, keepdims=True)
    y_ref[...] = (yz * lax.rsqrt(ms + EPS) * nw_ref[...]).astype(BF16)

    last = x_cur[ln - 8:ln]
    xpad[...] = last

    @pl.when(c == nchunks - 1)
    def _():
        conv_ref[...] = last[8 - (SSM_CONV - 1):8]
        for pair in range(16):
            hout_ref[pair * 128:(pair + 1) * 128, :] = h_t[pair].T


def _ssd_prompt(proj3, cw, cb_row, dtb_row, alog_row, dvec_row, nw_row):
    bx, tx, _ = proj3.shape
    nchunks = tx // SSD_CHUNK
    ln = SSD_CHUNK
    const = lambda shape: pl.BlockSpec(shape, lambda b, c: (0, 0))
    return pl.pallas_call(
        functools.partial(_ssd_kernel, nchunks=nchunks),
        grid=(bx, nchunks),
        in_specs=[pl.BlockSpec((None, ln, 2048), lambda b, c: (b, c, COL_Z // 2048)),
                  pl.BlockSpec((None, ln, 2048), lambda b, c: (b, c, COL_XS // 2048)),
                  pl.BlockSpec((None, ln, 512), lambda b, c: (b, c, COL_BM // 512)),
                  pl.BlockSpec((None, ln, 512), lambda b, c: (b, c, COL_CM // 512)),
                  pl.BlockSpec((None, ln, 128), lambda b, c: (b, c, COL_TAIL // 128)),
                  const((SSM_CONV, CONV_DIM)), const((1, CONV_DIM)), const((1, 128)), const((1, 128)),
                  const((1, D_INNER)), const((1, D_INNER))],
        out_specs=[pl.BlockSpec((None, ln, D_INNER), lambda b, c: (b, c, 0)),
                   pl.BlockSpec((None, SSM_CONV - 1, CONV_DIM), lambda b, c: (b, 0, 0)),
                   pl.BlockSpec((None, N_SSM_HEADS * SSM_HEAD_DIM, SSM_STATE), lambda b, c: (b, 0, 0))],
        out_shape=[jax.ShapeDtypeStruct((bx, tx, D_INNER), BF16),
                   jax.ShapeDtypeStruct((bx, SSM_CONV - 1, CONV_DIM), F32),
                   jax.ShapeDtypeStruct((bx, N_SSM_HEADS * SSM_HEAD_DIM, SSM_STATE), F32)],
        scratch_shapes=[pltpu.VMEM((8, CONV_DIM), F32), pltpu.VMEM((16, 128, 128), F32)],
        compiler_params=_params(("arbitrary", "arbitrary")),
        name="ssd_prompt",
    )(proj3, proj3, proj3, proj3, proj3, cw, cb_row, dtb_row, alog_row, dvec_row, nw_row)


def _ssm_pre_kernel(xs_ref, bm_ref, cm_ref, dt_ref, s0_ref, s1_ref, s2_ref, cw_ref, cb_ref, dtb_ref, alog_ref,
                    dvec_ref, eh_ref, xdt_ref, dec_ref, yd_ref, bmc_ref, cmc_ref, xdt_t_ref, dec_t_ref):
    cw = cw_ref[...]
    xbc = jnp.concatenate([xs_ref[...], bm_ref[...], cm_ref[...]], axis=1)
    conv = s0_ref[...] * cw[0:1, :] + s1_ref[...] * cw[1:2, :] + s2_ref[...] * cw[2:3, :] + xbc * cw[3:4, :] + cb_ref[...]
    xc = _silu(conv)
    xs_c = xc[:, 0:2048]
    bm_c = xc[:, 2048:2560]
    cm_c = xc[:, 2560:3072]
    dt = _softplus(dt_ref[...] + dtb_ref[...])
    a = -jnp.exp(alog_ref[...])
    dec = jnp.exp(dt * a)
    eh = eh_ref[...]
    dt_e = _dot_hi(dt, eh)
    dec_e = _dot_hi(dec, eh)
    xdt = xs_c * dt_e
    cbs = []
    for g in range(SSM_GROUPS):
        cbg = jnp.sum(cm_c[:, g * 128:(g + 1) * 128] * bm_c[:, g * 128:(g + 1) * 128], axis=-1, keepdims=True)
        cbs.append(jnp.broadcast_to(cbg, (cbg.shape[0], 512)))
    cb_e = jnp.concatenate(cbs, axis=1)
    xdt_ref[...] = xdt
    dec_ref[...] = dec_e
    yd_ref[...] = cb_e * xdt + xs_c * dvec_ref[...]
    bmc_ref[...] = bm_c
    cmc_ref[...] = cm_c
    for k in range(16):
        xdt_t_ref[k * 128:(k + 1) * 128, :] = xdt[:, k * 128:(k + 1) * 128].T
        dec_t_ref[k * 128:(k + 1) * 128, :] = dec_e[:, k * 128:(k + 1) * 128].T


def _ssm_pre(proj3, s0, s1, s2, cw, cb_row, dtb_row, alog_row, dvec_row, eh):
    nb = proj3.shape[1]
    const = lambda shape: pl.BlockSpec(shape, lambda i: (0,) * len(shape))
    return pl.pallas_call(
        _ssm_pre_kernel,
        grid=(1,),
        in_specs=[pl.BlockSpec((None, nb, 2048), lambda i: (0, 0, COL_XS // 2048)),
                  pl.BlockSpec((None, nb, 512), lambda i: (0, 0, COL_BM // 512)),
                  pl.BlockSpec((None, nb, 512), lambda i: (0, 0, COL_CM // 512)),
                  pl.BlockSpec((None, nb, 128), lambda i: (0, 0, COL_TAIL // 128)),
                  const((nb, CONV_DIM)), const((nb, CONV_DIM)), const((nb, CONV_DIM)),
                  const((SSM_CONV, CONV_DIM)), const((1, CONV_DIM)), const((1, 128)), const((1, 128)),
                  const((1, D_INNER)), const((128, D_INNER))],
        out_specs=[const((nb, D_INNER)), const((nb, D_INNER)), const((nb, D_INNER)), const((nb, 512)),
                   const((nb, 512)), const((D_INNER, nb)), const((D_INNER, nb))],
        out_shape=[jax.ShapeDtypeStruct((nb, D_INNER), F32)] * 3 + [jax.ShapeDtypeStruct((nb, 512), F32)] * 2
        + [jax.ShapeDtypeStruct((D_INNER, nb), F32)] * 2,
        compiler_params=_params(("arbitrary",)),
        name="ssm_pre",
    )(proj3, proj3, proj3, proj3, s0, s1, s2, cw, cb_row, dtb_row, alog_row, dvec_row, eh)


SSM_STATE_ROWS = 4


def _ssm_state_kernel(h0_ref, xdt_t_ref, dec_t_ref, bm_ref, cm_ref, hn_ref, yoff_ref):
    shift = (128 - pl.program_id(0) * SSM_STATE_ROWS) & 127
    xdt_r = pltpu.roll(xdt_t_ref[...], shift, axis=1)
    dec_r = pltpu.roll(dec_t_ref[...], shift, axis=1)
    for bb in range(SSM_STATE_ROWS):
        h0 = h0_ref[bb]
        xcol = xdt_r[:, bb:bb + 1]
        outs, yoffs = [], []
        for g in range(SSM_GROUPS):
            rows = slice(g * 512, (g + 1) * 512)
            outs.append(xcol[rows] * bm_ref[bb, :, g * 128:(g + 1) * 128])
            cm8 = jnp.broadcast_to(cm_ref[bb, :, g * 128:(g + 1) * 128], (8, 128)).astype(BF16)
            yoffs.append(_dot_nt(cm8, h0[rows].astype(BF16)))
        hn_ref[bb] = h0 * dec_r[:, bb:bb + 1] + jnp.concatenate(outs, axis=0)
        yoff_ref[bb] = jnp.concatenate(yoffs, axis=1)


def _ssm_state(h0, xdt_t, dec_t, bm_c, cm_c):
    nb = h0.shape[0]
    nr = SSM_STATE_ROWS
    assert nb == 128 and nb % nr == 0
    const = lambda shape: pl.BlockSpec(shape, lambda b: (0, 0))
    return pl.pallas_call(
        _ssm_state_kernel,
        grid=(nb // nr,),
        in_specs=[pl.BlockSpec((nr, D_INNER, SSM_STATE), lambda b: (b, 0, 0)),
                  const((D_INNER, nb)), const((D_INNER, nb)),
                  pl.BlockSpec((nr, 1, 512), lambda b: (b, 0, 0)), pl.BlockSpec((nr, 1, 512), lambda b: (b, 0, 0))],
        out_specs=[pl.BlockSpec((nr, D_INNER, SSM_STATE), lambda b: (b, 0, 0)),
                   pl.BlockSpec((nr, 8, D_INNER), lambda b: (b, 0, 0))],
        out_shape=[jax.ShapeDtypeStruct((nb, D_INNER, SSM_STATE), F32),
                   jax.ShapeDtypeStruct((nb, 8, D_INNER), F32)],
        compiler_params=_params(("arbitrary",)),
        name="ssm_state",
    )(h0, xdt_t, dec_t, bm_c.reshape(nb, 1, 512), cm_c.reshape(nb, 1, 512))


def _ssm_post_kernel(yd_ref, yoff_ref, dec_ref, z_ref, nw_ref, y_ref):
    y = yd_ref[...] + yoff_ref[...] * dec_ref[...]
    yz = y * _silu(z_ref[...])
    ms = jnp.mean(yz * yz, axis=-1, keepdims=True)
    y_ref[...] = (yz * lax.rsqrt(ms + EPS) * nw_ref[...]).astype(BF16)


def _ssm_post(yd, yoff, dec_e, proj3, nw_row):
    nb = yd.shape[0]
    const = lambda shape: pl.BlockSpec(shape, lambda i: (0, 0))
    return pl.pallas_call(
        _ssm_post_kernel,
        grid=(1,),
        in_specs=[const((nb, D_INNER)), const((nb, D_INNER)), const((nb, D_INNER)),
                  pl.BlockSpec((None, nb, 2048), lambda i: (0, 0, COL_Z // 2048)), const((1, D_INNER))],
        out_specs=const((nb, D_INNER)),
        out_shape=jax.ShapeDtypeStruct((nb, D_INNER), BF16),
        compiler_params=_params(("arbitrary",)),
        name="ssm_post",
    )(yd, yoff, dec_e, proj3, nw_row)


def _rope128(x, c, s1, s2):
    return x * c + pltpu.roll(x, 120, axis=1) * s1 + pltpu.roll(x, 8, axis=1) * s2


def _rope_kernel(q_ref, k01_ref, k23_ref, k45_ref, tail_ref, c_ref, s1_ref, s2_ref,
                 qo_ref, kvrows_ref, ksa_ref, khm_ref, gates_ref, win_ref, *, ntiles, tr):
    c = c_ref[...]
    s1 = s1_ref[...]
    s2 = s2_ref[...]
    q = q_ref[...]
    qo_ref[...] = (jnp.concatenate(
        [_rope128(q[:, k * 128:(k + 1) * 128], c, s1, s2) for k in range(8)], axis=1
    ) * (ATT_SCALE * LOG2E)).astype(BF16)
    streams = []
    for pref in (k01_ref, k23_ref, k45_ref):
        blk = pref[...]
        kk = jnp.concatenate([_rope128(blk[:, k * 128:(k + 1) * 128], c, s1, s2) for k in range(2)], axis=1)
        streams.append(kk)
        streams.append(blk[:, 256:512])
    kvrows_ref[...] = jnp.concatenate(streams[0:4], axis=1)
    pos = pl.program_id(1) * tr + lax.broadcasted_iota(jnp.int32, (tr, 64), 0)
    own = lax.broadcasted_iota(jnp.int32, (tr, 64), 1) == lax.shift_right_logical(pos, 6)
    extra = jnp.where(own, NEG, 0.0).astype(BF16)
    ksel = streams[2].astype(BF16)
    vsel = streams[3].astype(BF16)
    ones_col = (lax.broadcasted_iota(jnp.int32, (tr, 64), 1) == 0).astype(BF16)
    for h in range(N_KV_HEADS):
        ksa_ref[h] = jnp.concatenate([ksel[:, h * 64:(h + 1) * 64], extra], axis=1)
        ksa_ref[N_KV_HEADS + h] = jnp.concatenate([vsel[:, h * 64:(h + 1) * 64], ones_col], axis=1)
    for i, s in enumerate((3, 4, 5)):
        sb = streams[s].astype(BF16)
        for h in range(N_KV_HEADS):
            khm_ref[i * 4 + h] = sb[:, h * 64:(h + 1) * 64]
    g = _sig(tail_ref[...])
    for hk in range(N_KV_HEADS):
        gates_ref[hk] = g[:, 32 + hk * 12:32 + (hk + 1) * 12]

    @pl.when(pl.program_id(1) == ntiles - 1)
    def _():
        win_ref[...] = jnp.concatenate(streams[4:6], axis=1)


def _rope_prep(proj3, ctab, s1tab, s2tab, tr):
    bx, tx, _ = proj3.shape
    ntiles = tx // tr
    tab = pl.BlockSpec((tr, 128), lambda b, i: (i, 0))
    return pl.pallas_call(
        functools.partial(_rope_kernel, ntiles=ntiles, tr=tr),
        grid=(bx, ntiles),
        in_specs=[pl.BlockSpec((None, tr, 1024), lambda b, i: (b, i, COL_Q // 1024)),
                  pl.BlockSpec((None, tr, 512), lambda b, i: (b, i, COL_KV // 512)),
                  pl.BlockSpec((None, tr, 512), lambda b, i: (b, i, COL_KV // 512 + 1)),
                  pl.BlockSpec((None, tr, 512), lambda b, i: (b, i, COL_KV // 512 + 2)),
                  pl.BlockSpec((None, tr, 128), lambda b, i: (b, i, COL_TAIL // 128)),
                  tab, tab, tab],
        out_specs=[pl.BlockSpec((None, tr, 1024), lambda b, i: (b, i, 0)),
                   pl.BlockSpec((None, tr, 1024), lambda b, i: (b, i, 0)),
                   pl.BlockSpec((None, 8, tr, 128), lambda b, i: (b, 0, i, 0)),
                   pl.BlockSpec((None, 12, tr, 64), lambda b, i: (b, 0, i, 0)),
                   pl.BlockSpec((None, 4, tr, 12), lambda b, i: (b, 0, i, 0)),
                   pl.BlockSpec((None, tr, 512), lambda b, i: (b, 0, 0))],
        out_shape=[jax.ShapeDtypeStruct((bx, tx, 1024), BF16),
                   jax.ShapeDtypeStruct((bx, tx, 1024), F32),
                   jax.ShapeDtypeStruct((bx, 8, tx, 128), BF16),
                   jax.ShapeDtypeStruct((bx, 12, tx, 64), BF16),
                   jax.ShapeDtypeStruct((bx, 4, tx, 12), F32),
                   jax.ShapeDtypeStruct((bx, tr, 512), F32)],
        compiler_params=_params(("arbitrary", "arbitrary")),
        name="rope_prep",
    )(proj3, proj3, proj3, proj3, proj3, ctab, s1tab, s2tab)


def _compress_kernel(*refs, nrefs, npages, head_major):
    if nrefs > 1:
        refs = refs[1:]
    page_refs = refs[:nrefs]
    w1ab_ref, w1f_ref, pe_ref, w2_ref, kc_ref, vc_ref, shift = refs[nrefs:]
    nsub = PAGE_SIZE // CMP_STRIDE
    nj = npages * nsub
    nr = N_KV_HEADS * nj
    shift[nr:nr + 8, :] = jnp.zeros((8, CMP_HIDDEN), F32)
    ri = lax.broadcasted_iota(jnp.int32, (PAGE_SIZE, PAGE_SIZE), 0)
    ci = lax.broadcasted_iota(jnp.int32, (PAGE_SIZE, PAGE_SIZE), 1)
    perm = (ci == (ri & (nsub - 1)) * CMP_STRIDE + lax.shift_right_logical(ri, 3)).astype(BF16)
    if nrefs == 1:
        pages = [page_refs[0][p * PAGE_SIZE:(p + 1) * PAGE_SIZE, :] for p in range(npages)]
        grouped = [_dot(perm, pg.astype(BF16)) for pg in pages]
    else:
        grouped = [_dot_nt(perm, pr[...].reshape(2 * N_KV_HEADS * ATT_HEAD_DIM, PAGE_SIZE).astype(BF16))
                   for pr in page_refs]
    for st, o_ref in ((0, kc_ref), (1, vc_ref)):
        heads = []
        for h in range(N_KV_HEADS):
            c0 = st * 256 + h * ATT_HEAD_DIM
            rows = [jnp.concatenate([y[s * nsub:(s + 1) * nsub, c0:c0 + ATT_HEAD_DIM] for s in range(CMP_STRIDE)],
                                    axis=1) for y in grouped]
            heads.append(rows[0] if npages == 1 else jnp.concatenate(rows, axis=0))
        a = jnp.concatenate(heads, axis=0).astype(BF16)
        p2 = _dot(a, w1ab_ref[st])
        shift[0:nr, :] = p2[:, CMP_HIDDEN:2 * CMP_HIDDEN]
        pe8 = jnp.broadcast_to(pe_ref[st], (8, CMP_LEN * ATT_HEAD_DIM)).astype(BF16)
        pe_t = _dot(pe8, w1f_ref[st])[0:1, :]
        hid = p2[:, 0:CMP_HIDDEN] + shift[pl.ds(1, nr), :] + pe_t
        out = _dot(_silu(hid).astype(BF16), w2_ref[st]).astype(BF16)
        if head_major:
            for h in range(N_KV_HEADS):
                o_ref[h] = out[h * nj:(h + 1) * nj, :]
        else:
            o_ref[...] = jnp.concatenate([out[h * nj:(h + 1) * nj, :] for h in range(N_KV_HEADS)], axis=1)


def _compress_prompt(kv_rows, wbd, w1f, pe, w2bd):
    bx, tx, _ = kv_rows.shape
    nj = tx // CMP_STRIDE
    c4 = lambda shape: pl.BlockSpec(shape, lambda b: (0,) * len(shape))
    return pl.pallas_call(
        functools.partial(_compress_kernel, nrefs=1, npages=tx // PAGE_SIZE, head_major=True),
        grid=(bx,),
        in_specs=[pl.BlockSpec((None, tx, 512), lambda b: (b, 0, 0)),
                  c4(wbd.shape), c4(w1f.shape), c4(pe.shape), c4(w2bd.shape)],
        out_specs=[pl.BlockSpec((None, 4, nj, 64), lambda b: (b, 0, 0, 0))] * 2,
        out_shape=[jax.ShapeDtypeStruct((bx, 4, nj, 64), BF16)] * 2,
        scratch_shapes=[pltpu.VMEM((N_KV_HEADS * nj + 8, CMP_HIDDEN), F32)],
        compiler_params=_params(("arbitrary",)),
        name="compress_prompt",
    )(kv_rows, wbd, w1f, pe, w2bd)


def _compress_sample(page_table, cache3, wbd, w1f, pe, w2bd):
    nb, npages = page_table.shape
    nsub = PAGE_SIZE // CMP_STRIDE
    nj = npages * nsub
    c4 = lambda shape: pl.BlockSpec(shape, lambda b, pt: (0,) * len(shape))
    page_specs = [pl.BlockSpec((None, 2, N_KV_HEADS, ATT_HEAD_DIM, PAGE_SIZE),
                               functools.partial(lambda b, pt, p: (pt[b, p], 0, 0, 0, 0), p=p))
                  for p in range(npages)]
    grid_spec = pltpu.PrefetchScalarGridSpec(
        num_scalar_prefetch=1,
        grid=(nb,),
        in_specs=page_specs + [c4(wbd.shape), c4(w1f.shape), c4(pe.shape), c4(w2bd.shape)],
        out_specs=[pl.BlockSpec((None, nj, 256), lambda b, pt: (b, 0, 0))] * 2,
        scratch_shapes=[pltpu.VMEM((N_KV_HEADS * nj + 8, CMP_HIDDEN), F32)],
    )
    return pl.pallas_call(
        functools.partial(_compress_kernel, nrefs=npages, npages=npages, head_major=False),
        grid_spec=grid_spec,
        out_shape=[jax.ShapeDtypeStruct((nb, nj, 256), BF16)] * 2,
        compiler_params=_params(("arbitrary",)),
        name="compress_sample",
    )(page_table, *([cache3] * npages), wbd, w1f, pe, w2bd)


KEY_BLOCK = 1024
NSA_QUERY_BLOCK = 256


def _nsa_prompt_kernel(q_ref, g_ref, kc_ref, vc_ref, ksa_ref, vs_ref, kw_ref, vw_ref, ovt_ref,
                       o_ref, s_scr, m_scr, acc_scr, *, qblk):
    qb = pl.program_id(2)
    q0 = qb * qblk
    q4 = q_ref[...]
    nq = Q_PER_KV * qblk
    qs = jnp.concatenate([q4[:, r * 64:(r + 1) * 64] for r in range(Q_PER_KV)], axis=0)
    ncmp = kc_ref.shape[0]
    trow = q0 + (lax.broadcasted_iota(jnp.int32, (nq, 1), 0) & (qblk - 1))

    jl = lax.broadcasted_iota(jnp.int32, (nq, ncmp), 1)
    mask_c = jl <= lax.shift_right_arithmetic(trow - (CMP_LEN - 1), CMP_STRIDE.bit_length() - 1)
    s = jnp.where(mask_c, _dot_nt(qs, kc_ref[...]), NEG)
    e = jnp.where(mask_c, jnp.exp2(s - jnp.max(s, axis=-1, keepdims=True)), 0.0)
    p = e / jnp.maximum(jnp.sum(e, axis=-1, keepdims=True), TINY)
    o_c = _dot(p.astype(BF16), vc_ref[...])
    psum = p[0:qblk] + p[qblk:2 * qblk] + p[2 * qblk:3 * qblk] + p[3 * qblk:4 * qblk]
    imp_t = _dot_nt_hi(ovt_ref[...], psum)
    nblk = imp_t.shape[0]
    tq = q0 + lax.broadcasted_iota(jnp.int32, (nblk, qblk), 1)
    blk = lax.broadcasted_iota(jnp.int32, (nblk, qblk), 0)
    cur = lax.shift_right_logical(tq, 6)
    valid = blk * SEL_LEN <= tq
    forced = (blk == 0) | (blk == cur) | (blk == cur - 1)
    imp = jnp.where(valid, jnp.where(forced, BIG, imp_t), -BIG)
    rank = jnp.zeros((nblk, qblk), F32)
    for j in range(nblk):
        rj = imp[j:j + 1, :]
        beats = (rj > imp) | ((rj == imp) & (blk > j))
        rank = rank + jnp.where(beats, 1.0, 0.0)
    nsel_t = jnp.where(rank < float(SEL_TOPK), 0.0, 1.0)
    nsel_pad = jnp.concatenate([nsel_t, jnp.zeros((128 - nblk, qblk), F32)], axis=0) if nblk < 128 else nsel_t
    nsel = jnp.concatenate([nsel_pad[:, i * 128:(i + 1) * 128].T for i in range(qblk // 128)],
                           axis=0)[:, 0:64].astype(BF16)
    qaug = jnp.concatenate([qs, jnp.concatenate([nsel] * Q_PER_KV, axis=0)], axis=1)

    wlen = WINDOW + qblk
    wstart = pl.multiple_of(jnp.maximum(q0 - WINDOW, 0), 128)
    kw = kw_ref[pl.ds(wstart, wlen), :]
    vw = vw_ref[pl.ds(wstart, wlen), :]
    d = trow[0:qblk] - (wstart + lax.broadcasted_iota(jnp.int32, (qblk, wlen), 1))
    ok = (d >= 0) & (d < WINDOW)
    o_w = []
    for r in range(Q_PER_KV):
        sw = jnp.where(ok, _dot_nt(qs[r * qblk:(r + 1) * qblk], kw), NEG)
        ew = jnp.exp2(sw - jnp.max(sw, axis=-1, keepdims=True))
        o_w.append(_dot(ew.astype(BF16), vw) / jnp.maximum(jnp.sum(ew, axis=-1, keepdims=True), TINY))

    cw = 256
    nck = KEY_BLOCK // cw

    def block_scores(jb, chunks, causal):
        base = pl.multiple_of(jb * KEY_BLOCK, KEY_BLOCK)
        mx = None
        for c in chunks:
            sc = _dot_nt(qaug, ksa_ref[pl.ds(base + c * cw, cw), :])
            if causal:
                kpos = base + c * cw + lax.broadcasted_iota(jnp.int32, (nq, cw), 1)
                sc = jnp.where(kpos <= trow, sc, NEG)
            s_scr[jb, :, c * cw:(c + 1) * cw] = sc
            for i in range(cw // 128):
                part = sc[:, i * 128:(i + 1) * 128]
                mx = part if mx is None else jnp.maximum(mx, part)
        m_scr[...] = jnp.maximum(m_scr[...], mx)

    def block_values(jb, chunks):
        base = pl.multiple_of(jb * KEY_BLOCK, KEY_BLOCK)
        lo, hi = chunks[0] * cw, (chunks[-1] + 1) * cw
        ps = [jnp.exp2(s_scr[jb, :, c * 128:(c + 1) * 128] - mb).astype(BF16) for c in range(lo // 128, hi // 128)]
        acc_scr[...] = acc_scr[...] + _dot(jnp.concatenate(ps, axis=1), vs_ref[pl.ds(base + lo, hi - lo), :])

    every = tuple(range(nck))
    nfull = lax.shift_right_logical(qb, (KEY_BLOCK // qblk).bit_length() - 1)
    m_scr[...] = jnp.full((nq, 128), -jnp.inf, F32)

    def pass_a(jb, carry):
        block_scores(jb, every, False)
        return carry

    lax.fori_loop(0, nfull, pass_a, 0)
    block_scores(nfull, every, True)
    mb =jnp.broadcast_to(jnp.max(m_scr[...], axis=-1, keepdims=True), (nq, 128))

    acc_scr[...] = jnp.zeros((nq, 128), F32)

    def pass_b(jb, carry):
        block_values(jb, every)
        return carry

    lax.fori_loop(0, nfull, pass_b, 0)
    block_values(nfull, every)
    acc = acc_scr[...]
    o_s = acc[:, 0:64] / jnp.maximum(acc[:, 64:65], TINY)

    g = g_ref[...]
    outs = []
    for r in range(Q_PER_KV):
        rows = slice(r * qblk, (r + 1) * qblk)
        outs.append(g[:, 3 * r:3 * r + 1] * o_c[rows] + g[:, 3 * r + 1:3 * r + 2] * o_s[rows]
                    + g[:, 3 * r + 2:3 * r + 3] * o_w[r])
    o_ref[...] = jnp.concatenate(outs, axis=1).astype(BF16)


def _nsa_prompt(q_r, gates, kc, vc, ksa, khm, ovt):
    bx, tx, _ = q_r.shape
    qblk = NSA_QUERY_BLOCK
    ntiles = tx // qblk
    ncmp = kc.shape[2]
    nq = Q_PER_KV * qblk
    assert tx % KEY_BLOCK == 0 and tx >= WINDOW + qblk and tx <= 64 * SEL_LEN
    kv_spec = lambda s: pl.BlockSpec((None, None, tx, 64), lambda b, h, i: (b, s * 4 + h, 0, 0))
    return pl.pallas_call(
        functools.partial(_nsa_prompt_kernel, qblk=qblk),
        grid=(bx, N_KV_HEADS, ntiles),
        in_specs=[pl.BlockSpec((None, qblk, 256), lambda b, h, i: (b, i, h)),
                  pl.BlockSpec((None, None, qblk, 12), lambda b, h, i: (b, h, i, 0)),
                  pl.BlockSpec((None, None, ncmp, 64), lambda b, h, i: (b, h, 0, 0)),
                  pl.BlockSpec((None, None, ncmp, 64), lambda b, h, i: (b, h, 0, 0)),
                  pl.BlockSpec((None, None, tx, 128), lambda b, h, i: (b, h, 0, 0)),
                  pl.BlockSpec((None, None, tx, 128), lambda b, h, i: (b, N_KV_HEADS + h, 0, 0)),
                  kv_spec(1), kv_spec(2),
                  pl.BlockSpec(ovt.shape, lambda b, h, i: (0, 0))],
        out_specs=pl.BlockSpec((None, qblk, 256), lambda b, h, i: (b, i, h)),
        out_shape=jax.ShapeDtypeStruct((bx, tx, ATT_WIDTH), BF16),
        scratch_shapes=[pltpu.VMEM((tx // KEY_BLOCK, nq, KEY_BLOCK), F32), pltpu.VMEM((nq, 128), F32),
                        pltpu.VMEM((nq, 128), F32)],
        compiler_params=_params(("arbitrary", "arbitrary", "arbitrary")),
        name="nsa_prompt",
    )(q_r, gates, kc, vc, ksa, ksa, khm, khm, ovt)


NSA_SAMPLE_ROWS = 2


def _fold_heads(o256, hmask):
    o = o256 * hmask
    return o[:, 0:64] + o[:, 64:128] + o[:, 128:192] + o[:, 192:256]


def _nsa_sample_kernel(*refs, npages, nbb):
    refs = refs[1:]
    pages = refs[:npages * nbb]
    per_row = refs[npages * nbb:npages * nbb + 7]
    ov_ref, e_ref, o_ref, wout_ref = refs[npages * nbb + 7:]
    for bb in range(nbb):
        _nsa_sample_row(pages[bb * npages:(bb + 1) * npages], *[r.at[bb] for r in per_row], ov_ref, e_ref,
                        o_ref.at[bb], wout_ref.at[bb])


def _nsa_sample_row(pages, q_ref, g_ref, knew_ref, wnew_ref, kc_ref, vc_ref, win_ref, ov_ref, e_ref, o_ref, wout_ref):
    npages = len(pages)
    past = npages * PAGE_SIZE
    ncmp = kc_ref.shape[0]
    ntok = (past - CMP_LEN) // CMP_STRIDE + 1
    nsel = past // SEL_LEN + 1

    q16 = q_ref[...].astype(F32)
    rowh = lax.broadcasted_iota(jnp.int32, (16, 256), 0)
    laneh = lax.broadcasted_iota(jnp.int32, (16, 256), 1)
    hmask = (lax.shift_right_logical(rowh, 2) == lax.shift_right_logical(laneh, 6)).astype(F32)
    qbd_f = jnp.concatenate([q16] * 4, axis=1) * hmask
    qbd = qbd_f.astype(BF16)

    jl = lax.broadcasted_iota(jnp.int32, (16, ncmp), 1)
    mask_c = jl < ntok
    s = jnp.where(mask_c, _dot_nt(qbd, kc_ref[...]), NEG)
    e = jnp.where(mask_c, jnp.exp2(s - jnp.max(s, axis=-1, keepdims=True)), 0.0)
    p_c = e / jnp.maximum(jnp.sum(e, axis=-1, keepdims=True), TINY)
    o_c = _fold_heads(_dot(p_c.astype(BF16), vc_ref[...]), hmask)

    imp = _dot(p_c.astype(BF16), ov_ref[...])
    imp = imp + pltpu.roll(imp, 1, axis=0)
    imp = imp + pltpu.roll(imp, 2, axis=0)
    blk = lax.broadcasted_iota(jnp.int32, (16, 128), 1)
    exists = blk < nsel
    forced = (blk == 0) | (blk == nsel - 1) | (blk == nsel - 2)
    imp = jnp.where(exists, jnp.where(forced, BIG, imp), -BIG)
    rank = jnp.zeros((16, 128), F32)
    for j in range(nsel):
        cj = imp[:, j:j + 1]
        beats = (cj > imp) | ((cj == imp) & (blk > j))
        rank = rank + jnp.where(beats, 1.0, 0.0)
    group_row = (lax.broadcasted_iota(jnp.int32, (16, 128), 0) & 3) == 3
    sel16 = jnp.where((rank < float(min(SEL_TOPK, nsel))) & exists & group_row, 1.0, 0.0)
    sel16 = sel16 + pltpu.roll(sel16, 15, axis=0)
    sel16 = sel16 + pltpu.roll(sel16, 14, axis=0)
    bias_past = (_dot(sel16.astype(BF16), e_ref[...]) - 1.0) * BIG

    knew = knew_ref[...]
    s_new = jnp.sum(qbd_f * knew[:, 512:768], axis=-1, keepdims=True)
    hd = N_KV_HEADS * ATT_HEAD_DIM
    s_past = jnp.concatenate(
        [_dot(qbd, pg[0].reshape(hd, PAGE_SIZE).astype(BF16)) for pg in pages], axis=1) + bias_past
    m = jnp.maximum(jnp.max(s_past, axis=-1, keepdims=True), s_new)
    e_past = jnp.exp2(s_past - m)
    e_new = jnp.exp2(s_new - m)
    acc = e_new * knew[:, 768:1024]
    for p, pg in enumerate(pages):
        acc = acc + _dot_nt(e_past[:, p * 128:(p + 1) * 128].astype(BF16), pg[1].reshape(hd, PAGE_SIZE).astype(BF16))
    lsum = jnp.sum(e_past, axis=-1, keepdims=True) + e_new
    o_s = _fold_heads(acc / jnp.maximum(lsum, TINY), hmask)

    kw_t = win_ref[0].reshape(hd, WINDOW)
    vw_t = win_ref[1].reshape(hd, WINDOW)
    wnew = wnew_ref[...]
    wl = lax.broadcasted_iota(jnp.int32, (16, WINDOW), 1)
    s_w = jnp.where(wl >= 1, _dot(qbd, kw_t.astype(BF16)), NEG)
    s_wn = jnp.sum(qbd_f * wnew[:, 0:256], axis=-1, keepdims=True)
    mw = jnp.maximum(jnp.max(s_w, axis=-1, keepdims=True), s_wn)
    e_w = jnp.where(wl >= 1, jnp.exp2(s_w - mw), 0.0)
    e_wn = jnp.exp2(s_wn - mw)
    acc_w = _dot_nt(e_w.astype(BF16), vw_t.astype(BF16)) + e_wn * wnew[:, 256:512]
    o_w = _fold_heads(acc_w / jnp.maximum(jnp.sum(e_w, axis=-1, keepdims=True) + e_wn, TINY), hmask)

    g = _sig(g_ref[...])
    o_ref[...] = (g[:, 0:1] * o_c + g[:, 1:2] * o_s + g[:, 2:3] * o_w).astype(BF16)

    last = lax.broadcasted_iota(jnp.int32, (hd, 128), 1) == 127
    for i, src in enumerate((kw_t, vw_t)):
        col = jnp.broadcast_to(wnew[:, i * hd:(i + 1) * hd], (128, hd)).T
        rolled = pltpu.roll(src, WINDOW - 1, axis=1)
        out = jnp.concatenate([rolled[:, 0:WINDOW - 128], jnp.where(last, col, rolled[:, WINDOW - 128:])], axis=1)
        wout_ref[i] = out.reshape(N_KV_HEADS, ATT_HEAD_DIM, WINDOW)


def _nsa_sample(page_table, cache3, q16, g16, knew, wnew, kc, vc, win, ov, emat):
    nb, npages = page_table.shape
    ncmp = kc.shape[1]
    c2 = lambda shape: pl.BlockSpec(shape, lambda b, pt: (0,) * len(shape))
    nbb = NSA_SAMPLE_ROWS
    assert nb % nbb == 0
    page_specs = [pl.BlockSpec((None, 2, N_KV_HEADS, ATT_HEAD_DIM, PAGE_SIZE),
                               functools.partial(lambda b, pt, bb, p: (pt[b * nbb + bb, p], 1, 0, 0, 0), bb=bb, p=p))
                  for bb in range(nbb) for p in range(npages)]
    win_spec = pl.BlockSpec((nbb, 2, N_KV_HEADS, ATT_HEAD_DIM, WINDOW), lambda b, pt: (b, 0, 0, 0, 0))
    rows = lambda r, c: pl.BlockSpec((nbb, r, c), lambda b, pt: (b, 0, 0))
    grid_spec = pltpu.PrefetchScalarGridSpec(
        num_scalar_prefetch=1,
        grid=(nb // nbb,),
        in_specs=page_specs + [rows(16, 64), rows(16, 3), rows(1, 1024), rows(1, 512), rows(ncmp, 256),
                               rows(ncmp, 256), win_spec, c2(ov.shape), c2(emat.shape)],
        out_specs=[rows(16, 64), win_spec],
    )
    return pl.pallas_call(
        functools.partial(_nsa_sample_kernel, npages=npages, nbb=nbb),
        grid_spec=grid_spec,
        out_shape=[jax.ShapeDtypeStruct((nb, 16, 64), BF16),
                   jax.ShapeDtypeStruct((nb, 2, N_KV_HEADS, ATT_HEAD_DIM, WINDOW), F32)],
        compiler_params=_params(("arbitrary",)),
        name="nsa_sample",
    )(page_table, *([cache3] * (npages * nbb)), q16, g16, knew, wnew, kc, vc, win, ov, emat)


def _merge_kernel(x_ref, yssm_ref, yatt_ref, gs_ref, ga_ref, g1_ref, sh2_ref, sc2_ref, nw_ref,
                  wss_ref, wat_ref, wo_ref, x1_ref, h2_ref):
    ms = _dot(yssm_ref[...], wss_ref[...])
    ma = _dot(yatt_ref[...], wat_ref[...])
    merged = _sig(gs_ref[...]) * ms + _sig(ga_ref[...]) * ma
    x1 = x_ref[...] + g1_ref[...] * _dot(merged.astype(BF16), wo_ref[...])
    x1_ref[...] = x1
    y = x1 * lax.rsqrt(jnp.mean(x1 * x1, axis=-1, keepdims=True) + EPS)
    h2_ref[...] = (y * nw_ref[...] * (1.0 + sc2_ref[...]) + sh2_ref[...]).astype(BF16)


def _merge_out(x3, yssm, yatt, proj3, mod3, nw_row, wss, wat, wo, tm):
    bx, tx, _ = x3.shape
    r = mod3.shape[1]
    mod_b = (lambda b: b) if mod3.shape[0] == bx else (lambda b: 0)
    row = lambda w, cb: pl.BlockSpec((None, tm, w), lambda b, i: (b, i, cb))
    modc = lambda cb: pl.BlockSpec((None, r, D_MODEL), lambda b, i: (mod_b(b), 0, cb))
    const = lambda shape: pl.BlockSpec(shape, lambda b, i: (0, 0))
    return pl.pallas_call(
        _merge_kernel,
        grid=(bx, tx // tm),
        in_specs=[row(D_MODEL, 0), row(D_INNER, 0), row(ATT_WIDTH, 0),
                  row(1024, COL_GM // 1024), row(1024, COL_GM // 1024 + 1),
                  modc(2), modc(3), modc(4), const((1, D_MODEL)),
                  const(wss.shape), const(wat.shape), const(wo.shape)],
        out_specs=[row(D_MODEL, 0), row(D_MODEL, 0)],
        out_shape=[jax.ShapeDtypeStruct((bx, tx, D_MODEL), F32), jax.ShapeDtypeStruct((bx, tx, D_MODEL), BF16)],
        compiler_params=_params(("arbitrary", "arbitrary")),
        name="merge_out",
    )(x3, yssm, yatt, proj3, proj3, mod3, mod3, mod3, nw_row, wss, wat, wo)


FFN_TN = 1408
FFN_NT = D_FF // FFN_TN


def _ffn_up_prompt_kernel(h2_ref, wa_ref, wb_ref, cwa_ref, cwb_ref, cba_ref, cbb_ref,
                          act_ref, fa_ref, fb_ref, tails, *, tm, ntiles):
    i = pl.program_id(1)
    j = pl.program_id(2)
    h2 = h2_ref[...]
    row8 = lax.broadcasted_iota(jnp.int32, (8, FFN_TN), 0)

    @pl.when(i == 0)
    def _():
        tails[j] = jnp.zeros((2, 8, FFN_TN), F32)

    halves = []
    for half, (w_ref, cw_ref, cb_ref, f_ref) in enumerate(((wa_ref, cwa_ref, cba_ref, fa_ref),
                                                            (wb_ref, cwb_ref, cbb_ref, fb_ref))):
        u = _dot(h2, w_ref[...])
        cw = cw_ref[...]
        tail = tails[j, half]
        conv = u * cw[FFN_CONV - 1:FFN_CONV, :] + cb_ref[...]
        for k in range(1, FFN_CONV):
            rolled = pltpu.roll(u, k, axis=0)
            head = jnp.where(row8 < k, pltpu.roll(tail, k, axis=0), rolled[0:8])
            conv = conv + jnp.concatenate([head, rolled[8:]], axis=0) * cw[FFN_CONV - 1 - k:FFN_CONV - k, :]
        halves.append(conv)
        last = u[tm - 8:tm]
        tails[j, half] = last
        f_ref[...] = last[8 - (FFN_CONV - 1):8]
    act_ref[...] = (_silu(halves[0]) * halves[1]).astype(BF16)


def _ffn_up_prompt(h2, wup, cw, cb_row, tm):
    bx, tx, _ = h2.shape
    ntiles = tx // tm
    return pl.pallas_call(
        functools.partial(_ffn_up_prompt_kernel, tm=tm, ntiles=ntiles),
        grid=(bx, ntiles, FFN_NT),
        in_specs=[pl.BlockSpec((None, tm, D_MODEL), lambda b, i, j: (b, i, 0)),
                  pl.BlockSpec((D_MODEL, FFN_TN), lambda b, i, j: (0, j)),
                  pl.BlockSpec((D_MODEL, FFN_TN), lambda b, i, j: (0, j + FFN_NT)),
                  pl.BlockSpec((FFN_CONV, FFN_TN), lambda b, i, j: (0, j)),
                  pl.BlockSpec((FFN_CONV, FFN_TN), lambda b, i, j: (0, j + FFN_NT)),
                  pl.BlockSpec((1, FFN_TN), lambda b, i, j: (0, j)),
                  pl.BlockSpec((1, FFN_TN), lambda b, i, j: (0, j + FFN_NT))],
        out_specs=[pl.BlockSpec((None, tm, FFN_TN), lambda b, i, j: (b, i, j)),
                   pl.BlockSpec((None, None, FFN_CONV - 1, FFN_TN), lambda b, i, j: (b, i, 0, j)),
                   pl.BlockSpec((None, None, FFN_CONV - 1, FFN_TN), lambda b, i, j: (b, i, 0, j))],
        out_shape=[jax.ShapeDtypeStruct((bx, tx, D_FF), BF16),
                   jax.ShapeDtypeStruct((bx, ntiles, FFN_CONV - 1, D_FF), F32),
                   jax.ShapeDtypeStruct((bx, ntiles, FFN_CONV - 1, D_FF), F32)],
        scratch_shapes=[pltpu.VMEM((FFN_NT, 2, 8, FFN_TN), F32)],
        compiler_params=_params(("arbitrary", "arbitrary", "arbitrary")),
        name="ffn_up_prompt",
    )(h2, wup, wup, cw, cw, cb_row, cb_row)


def _ffn_up_sample_kernel(h2_ref, wa_ref, wb_ref, h0a_ref, h0b_ref, h1a_ref, h1b_ref, cwa_ref, cwb_ref,
                          cba_ref, cbb_ref, act_ref, ua_ref, ub_ref):
    h2 = h2_ref[...]
    ua = _dot(h2, wa_ref[...])
    ub = _dot(h2, wb_ref[...])
    cwa = cwa_ref[...]
    cwb = cwb_ref[...]
    ca = h0a_ref[...] * cwa[0:1, :] + h1a_ref[...] * cwa[1:2, :] + ua * cwa[2:3, :] + cba_ref[...]
    cb = h0b_ref[...] * cwb[0:1, :] + h1b_ref[...] * cwb[1:2, :] + ub * cwb[2:3, :] + cbb_ref[...]
    act_ref[...] = (_silu(ca) * cb).astype(BF16)
    ua_ref[...] = ua
    ub_ref[...] = ub


def _ffn_up_sample(h2, wup, hist0, hist1, cw, cb_row):
    nb = h2.shape[0]
    col = lambda rows, off: pl.BlockSpec((rows, FFN_TN), lambda j: (0, j + off))
    return pl.pallas_call(
        _ffn_up_sample_kernel,
        grid=(FFN_NT,),
        in_specs=[pl.BlockSpec((nb, D_MODEL), lambda j: (0, 0)),
                  col(D_MODEL, 0), col(D_MODEL, FFN_NT), col(nb, 0), col(nb, FFN_NT), col(nb, 0), col(nb, FFN_NT),
                  col(FFN_CONV, 0), col(FFN_CONV, FFN_NT), col(1, 0), col(1, FFN_NT)],
        out_specs=[col(nb, 0), col(nb, 0), col(nb, 0)],
        out_shape=[jax.ShapeDtypeStruct((nb, D_FF), BF16), jax.ShapeDtypeStruct((nb, D_FF), F32),
                   jax.ShapeDtypeStruct((nb, D_FF), F32)],
        compiler_params=_params(("arbitrary",)),
        name="ffn_up_sample",
    )(h2, wup, wup, hist0, hist0, hist1, hist1, cw, cw, cb_row, cb_row)


def _ffn_down_kernel(act_ref, x1_ref, g2_ref, nw_ref, w_ref, y_ref):
    x2 = x1_ref[...] + g2_ref[...] * _dot(act_ref[...], w_ref[...])
    y_ref[...] = x2 * lax.rsqrt(jnp.mean(x2 * x2, axis=-1, keepdims=True) + EPS) * nw_ref[...]


def _ffn_down(act, x1, mod3, nw_row, wdown, tm):
    bx, tx, _ = x1.shape
    r = mod3.shape[1]
    mod_b = (lambda b: b) if mod3.shape[0] == bx else (lambda b: 0)
    return pl.pallas_call(
        _ffn_down_kernel,
        grid=(bx, tx // tm),
        in_specs=[pl.BlockSpec((None, tm, D_FF), lambda b, i: (b, i, 0)),
                  pl.BlockSpec((None, tm, D_MODEL), lambda b, i: (b, i, 0)),
                  pl.BlockSpec((None, r, D_MODEL), lambda b, i: (mod_b(b), 0, 5)),
                  pl.BlockSpec((1, D_MODEL), lambda b, i: (0, 0)),
                  pl.BlockSpec((D_FF, D_MODEL), lambda b, i: (0, 0))],
        out_specs=pl.BlockSpec((None, tm, D_MODEL), lambda b, i: (b, i, 0)),
        out_shape=jax.ShapeDtypeStruct((bx, tx, D_MODEL), F32),
        compiler_params=_params(("arbitrary", "arbitrary")),
        name="ffn_down",
    )(act, x1, mod3, nw_row, wdown)


def _rope_tables(pos):
    half = ROPE_DIM // 2
    inv_freq = ROPE_THETA ** (-np.arange(half, dtype=np.float64) / half)
    ang = np.asarray(pos, np.float64)[:, None] * inv_freq[None, :]
    cos, sin = np.cos(ang).astype(np.float32), np.sin(ang).astype(np.float32)
    n = ang.shape[0]
    ones = np.ones((n, ATT_HEAD_DIM - ROPE_DIM), np.float32)
    zeros8 = np.zeros((n, half), np.float32)
    zeros48 = np.zeros((n, ATT_HEAD_DIM - ROPE_DIM), np.float32)
    c = np.concatenate([cos, cos, ones], axis=1)
    s1 = np.concatenate([-sin, zeros8, zeros48], axis=1)
    s2 = np.concatenate([zeros8, sin, zeros48], axis=1)
    return tuple(jnp.asarray(np.concatenate([t, t], axis=1)) for t in (c, s1, s2))


def _overlap(nc, ns):
    cst = np.arange(nc)[:, None] * CMP_STRIDE
    sst = np.arange(ns)[None, :] * SEL_LEN
    ov = np.clip(np.minimum(cst + CMP_LEN, sst + SEL_LEN) - np.maximum(cst, sst), 0, None)
    return (ov / CMP_STRIDE).astype(np.float32)


def _cmp_weights(pe, w1, w2):
    k16 = CMP_STRIDE * ATT_HEAD_DIM
    w1ab = jnp.concatenate([w1[:CMP_STRIDE].reshape(k16, CMP_HIDDEN), w1[CMP_STRIDE:].reshape(k16, CMP_HIDDEN)], axis=1)
    return (w1ab.astype(BF16), w1.reshape(CMP_LEN * ATT_HEAD_DIM, CMP_HIDDEN).astype(BF16), pe.reshape(1, -1),
            w2.astype(BF16))


def kernel(x_prompt, x_sample, c_prompt, c_sample, cache_nsa_kv, page_table, cache_win_kv, state_ssm, state_ssm_conv, state_ffn_conv, ada_w, ada_b, norm1_w, norm2_w, final_norm_w, w_in, ssm_conv_w, ssm_conv_b, ssm_dt_bias, ssm_A_log, ssm_D, ssm_norm_w, cmp_pe_k, cmp_w1_k, cmp_w2_k, cmp_pe_v, cmp_w1_v, cmp_w2_v, w_ssm_out, w_att_out, w_out, ffn_w_up, ffn_conv_w, ffn_conv_b, ffn_w_down):
    bp, tp, _ = x_prompt.shape
    nb = x_sample.shape[0]
    npages = page_table.shape[1]
    past = npages * PAGE_SIZE

    o_z, o_xbc, o_dt = 0, D_INNER, D_INNER + CONV_DIM
    o_q = o_dt + N_SSM_HEADS
    o_kv = o_q + ATT_WIDTH
    o_ag = o_kv + 6 * 256
    o_mg = o_ag + 3 * N_ATT_HEADS
    w_r = jnp.concatenate([
        w_in[:, o_z:o_z + 2048], w_in[:, o_xbc:o_xbc + 2048], w_in[:, o_q:o_q + 1024], w_in[:, o_mg:o_mg + 2048],
        w_in[:, o_xbc + 2048:o_xbc + 3072], w_in[:, o_kv:o_kv + 1536], w_in[:, o_dt:o_dt + 32],
        w_in[:, o_ag:o_ag + 48], jnp.zeros((D_MODEL, 48), F32)], axis=1).astype(BF16)
    ada_w_bf = ada_w.astype(BF16)
    wss = w_ssm_out.astype(BF16)
    wat = w_att_out.astype(BF16)
    wo = w_out.astype(BF16)
    wup = ffn_w_up.astype(BF16)
    wdown = ffn_w_down.astype(BF16)
    row = lambda v: v.reshape(1, -1)
    pad128 = lambda v: jnp.pad(v, (0, 128 - v.shape[0])).reshape(1, 128)
    dvec = jnp.repeat(ssm_D, SSM_HEAD_DIM).reshape(1, D_INNER)
    cmpw_k = _cmp_weights(cmp_pe_k, cmp_w1_k, cmp_w2_k)
    cmpw_v = _cmp_weights(cmp_pe_v, cmp_w1_v, cmp_w2_v)
    wbd, w1f, pe2, w2bd = (jnp.stack([a, b]) for a, b in zip(cmpw_k, cmpw_v))

    npad = (-(bp + nb)) % 8
    c_all = jnp.concatenate([c_prompt, c_sample, jnp.zeros((npad, D_MODEL), F32)], axis=0)
    mod = _ada_mod(c_all, ada_w_bf, row(ada_b))
    mod_p = mod[:bp].reshape(bp, 1, 6 * D_MODEL)
    mod_s = mod[bp:bp + nb].reshape(1, nb, 6 * D_MODEL)

    proj_p = _inproj(x_prompt, mod_p, row(norm1_w), w_r, 1024)
    yssm_p, conv_p, hlast_p = _ssd_prompt(proj_p, ssm_conv_w, row(ssm_conv_b), pad128(ssm_dt_bias),
                                          pad128(ssm_A_log), dvec, row(ssm_norm_w))
    tabs_p = _rope_tables(np.arange(tp))
    q_p, kvrows_p, ksa_p, khm_p, gates_p, win_p = _rope_prep(proj_p, *tabs_p, 512)
    kc_p, vc_p = _compress_prompt(kvrows_p, wbd, w1f, pe2, w2bd)
    nblk = -(-tp // SEL_LEN)
    ncmp = tp // CMP_STRIDE
    ovt = np.zeros((nblk, ncmp), np.float32)
    ovt[:, :ncmp - 1] = _overlap(ncmp - 1, nblk).T
    yatt_p = _nsa_prompt(q_p, gates_p, kc_p, vc_p, ksa_p, khm_p, jnp.asarray(ovt))
    x1_p, h2_p = _merge_out(x_prompt, yssm_p, yatt_p, proj_p, mod_p, row(norm2_w), wss, wat, wo, 512)
    act_p, fa_p, fb_p = _ffn_up_prompt(h2_p, wup, ffn_conv_w, row(ffn_conv_b), 512)
    y_p = _ffn_down(act_p, x1_p, mod_p, row(final_norm_w), wdown, 512)

    x_s3 = x_sample.reshape(1, nb, D_MODEL)
    proj_s = _inproj(x_s3, mod_s, row(norm1_w), w_r, nb)
    eh = (np.arange(D_INNER)[None, :] // SSM_HEAD_DIM == np.arange(128)[:, None]).astype(np.float32)
    xdt, dec_e, yd, bm_c, cm_c, xdt_t, dec_t = _ssm_pre(
        proj_s, state_ssm_conv[:, 0], state_ssm_conv[:, 1], state_ssm_conv[:, 2], ssm_conv_w, row(ssm_conv_b),
        pad128(ssm_dt_bias), pad128(ssm_A_log), dvec, jnp.asarray(eh))
    h_new, yoff = _ssm_state(state_ssm.reshape(nb, D_INNER, SSM_STATE), xdt_t, dec_t, bm_c, cm_c)
    yssm_s = _ssm_post(yd, yoff[:, 0, :], dec_e, proj_s, row(ssm_norm_w))
    tabs_s = _rope_tables(np.full((nb,), past))
    q_s, kvrows_s, _, _, _, wnew_s = _rope_prep(proj_s, *tabs_s, nb)
    cache3 = jnp.transpose(cache_nsa_kv, (0, 2, 3, 4, 1))
    win_t = jnp.transpose(cache_win_kv, (0, 2, 3, 4, 1))
    kc_s, vc_s = _compress_sample(page_table, cache3, wbd, w1f, pe2, w2bd)
    ncmp_s = past // CMP_STRIDE
    nsel_s = past // SEL_LEN + 1
    ov_s = np.zeros((ncmp_s, 128), np.float32)
    ov_s[:ncmp_s - 1, :nsel_s] = _overlap(ncmp_s - 1, nsel_s)
    e_s = (np.arange(past)[None, :] // SEL_LEN == np.arange(128)[:, None]).astype(np.float32)
    att_g_s = proj_s[0, :, COL_TAIL + 32:COL_TAIL + 80].reshape(nb, N_ATT_HEADS, 3)
    yatt_s16, win_s = _nsa_sample(
        page_table, cache3, q_s.reshape(nb, N_ATT_HEADS, ATT_HEAD_DIM), att_g_s, kvrows_s.reshape(nb, 1, 1024),
        wnew_s.reshape(nb, 1, 512), kc_s, vc_s, win_t, jnp.asarray(ov_s, dtype=BF16), jnp.asarray(e_s, dtype=BF16))
    x1_s, h2_s = _merge_out(x_s3, yssm_s.reshape(1, nb, D_INNER), yatt_s16.reshape(1, nb, ATT_WIDTH), proj_s, mod_s,
                            row(norm2_w), wss, wat, wo, nb)
    act_s, ua_s, ub_s = _ffn_up_sample(h2_s.reshape(nb, D_MODEL), wup, state_ffn_conv[:, 0], state_ffn_conv[:, 1],
                                       ffn_conv_w, row(ffn_conv_b))
    y_s = _ffn_down(act_s.reshape(1, nb, D_FF), x1_s, mod_s, row(final_norm_w), wdown, nb)

    xbc_s = jnp.concatenate([proj_s[0, :, COL_XS:COL_XS + 2048], proj_s[0, :, COL_BM:COL_BM + 1024]], axis=1)
    conv_s = jnp.stack([state_ssm_conv[:, 1], state_ssm_conv[:, 2], xbc_s], axis=1)
    ffn_s = jnp.stack([state_ffn_conv[:, 1], jnp.concatenate([ua_s, ub_s], axis=1)], axis=1)
    return (y_p,
            y_s.reshape(nb, 1, D_MODEL),
            kvrows_p.reshape(bp, tp, 4, N_KV_HEADS, ATT_HEAD_DIM),
            kvrows_s.reshape(nb, 1, 4, N_KV_HEADS, ATT_HEAD_DIM),
            win_p.reshape(bp, WINDOW, 2, N_KV_HEADS, ATT_HEAD_DIM),
            jnp.transpose(win_s, (0, 4, 1, 2, 3)),
            hlast_p.reshape(bp, N_SSM_HEADS, SSM_HEAD_DIM, SSM_STATE),
            h_new.reshape(nb, N_SSM_HEADS, SSM_HEAD_DIM, SSM_STATE),
            conv_p,
            conv_s,
            jnp.concatenate([fa_p[:, -1], fb_p[:, -1]], axis=2),
            ffn_s)
```

```python
import functools

import numpy as np
import jax
import jax.numpy as jnp
from jax import lax
from jax.experimental import pallas as pl
from jax.experimental.pallas import tpu as pltpu

F32 = jnp.float32
BF16 = jnp.bfloat16
HIGHEST = lax.Precision.HIGHEST

D_MODEL = 1024
D_INNER = 2048
N_SSM_HEADS = 32
SSM_HEAD_DIM = 64
SSM_STATE = 128
SSM_GROUPS = 4
SSM_CONV = 4
CONV_DIM = 3072
SSD_CHUNK = 128
N_ATT_HEADS = 16
ATT_HEAD_DIM = 64
N_KV_HEADS = 4
Q_PER_KV = 4
ATT_WIDTH = 1024
ATT_SCALE = ATT_HEAD_DIM ** -0.5
LOG2E = 1.4426950408889634
ROPE_DIM = 16
ROPE_THETA = 500000.0
CMP_LEN = 32
CMP_STRIDE = 16
CMP_HIDDEN = 128
SEL_LEN = 64
SEL_TOPK = 16
WINDOW = 512
Q_BLOCK = 128
PAGE_SIZE = 128
D_FF = 2816
FFN_CONV = 3
EPS = 1e-6
NEG = -1e30
BIG = 1e30
TINY = 1e-30

COL_Z = 0
COL_XS = 2048
COL_Q = 4096
COL_GM = 5120
COL_BM = 7168
COL_CM = 7680
COL_KV = 8192
COL_TAIL = 9728
N_PROJ = 9856
PROJ_TN = 1408
VMEM_LIMIT = 56 * 1024 * 1024


def _sig(x):
    return 1.0 / (1.0 + jnp.exp(-x))


def _silu(x):
    return x * _sig(x)


def _softplus(x):
    return jnp.maximum(x, 0.0) + jnp.log(1.0 + jnp.exp(-jnp.abs(x)))


def _dot(a, b):
    return jnp.dot(a, b, preferred_element_type=F32)


def _dot_hi(a, b):
    return jnp.dot(a, b, preferred_element_type=F32, precision=HIGHEST)


def _dot_nt(a, b):
    return lax.dot_general(a, b, (((1,), (1,)), ((), ())), preferred_element_type=F32)


def _dot_nt_hi(a, b):
    return lax.dot_general(a, b, (((1,), (1,)), ((), ())), preferred_element_type=F32, precision=HIGHEST)


def _params(sem):
    return pltpu.CompilerParams(dimension_semantics=sem, vmem_limit_bytes=VMEM_LIMIT)


def _ada_kernel(c_ref, w_ref, b_ref, o_ref):
    c = c_ref[...]
    o_ref[...] = _dot(_silu(c).astype(BF16), w_ref[...]) + b_ref[...]


def _ada_mod(c_all, w_bf, b_row):
    m = c_all.shape[0]
    tn = 512
    return pl.pallas_call(
        _ada_kernel,
        grid=(w_bf.shape[1] // tn,),
        in_specs=[pl.BlockSpec((m, D_MODEL), lambda j: (0, 0)),
                  pl.BlockSpec((D_MODEL, tn), lambda j: (0, j)),
                  pl.BlockSpec((1, tn), lambda j: (0, j))],
        out_specs=pl.BlockSpec((m, tn), lambda j: (0, j)),
        out_shape=jax.ShapeDtypeStruct((m, w_bf.shape[1]), F32),
        compiler_params=_params(("arbitrary",)),
        name="ada_mod",
    )(c_all, w_bf, b_row)


def _inproj_kernel(x_ref, sh_ref, sc_ref, nw_ref, w_ref, o_ref, h_scr):
    @pl.when(pl.program_id(2) == 0)
    def _():
        x = x_ref[...]
        y = x * lax.rsqrt(jnp.mean(x * x, axis=-1, keepdims=True) + EPS)
        h = y * nw_ref[...] * (1.0 + sc_ref[...]) + sh_ref[...]
        h_scr[...] = h.astype(BF16)

    o_ref[...] = _dot(h_scr[...], w_ref[...])


def _inproj(x3, mod3, nw_row, w_bf, tm):
    bx, tx, _ = x3.shape
    r = mod3.shape[1]
    mod_b = (lambda b: b) if mod3.shape[0] == bx else (lambda b: 0)
    return pl.pallas_call(
        _inproj_kernel,
        grid=(bx, tx // tm, N_PROJ // PROJ_TN),
        in_specs=[pl.BlockSpec((None, tm, D_MODEL), lambda b, i, j: (b, i, 0)),
                  pl.BlockSpec((None, r, D_MODEL), lambda b, i, j: (mod_b(b), 0, 0)),
                  pl.BlockSpec((None, r, D_MODEL), lambda b, i, j: (mod_b(b), 0, 1)),
                  pl.BlockSpec((1, D_MODEL), lambda b, i, j: (0, 0)),
                  pl.BlockSpec((D_MODEL, PROJ_TN), lambda b, i, j: (0, j))],
        out_specs=pl.BlockSpec((None, tm, PROJ_TN), lambda b, i, j: (b, i, j)),
        out_shape=jax.ShapeDtypeStruct((bx, tx, N_PROJ), F32),
        scratch_shapes=[pltpu.VMEM((tm, D_MODEL), BF16)],
        compiler_params=_params(("arbitrary", "arbitrary", "arbitrary")),
        name="inproj",
    )(x3, mod3, mod3, nw_row, w_bf)


def _ssd_kernel(z_ref, xs_ref, bm_ref, cm_ref, dt_ref, cw_ref, cb_ref, dtb_ref, alog_ref, dvec_ref, nw_ref,
                y_ref, conv_ref, hout_ref, xpad, h_t, *, nchunks):
    c = pl.program_id(1)
    ln = SSD_CHUNK

    @pl.when(c == 0)
    def _():
        xpad[...] = jnp.zeros((8, CONV_DIM), F32)
        h_t[...] = jnp.zeros_like(h_t)

    cw = cw_ref[...]
    x_cur = jnp.concatenate([xs_ref[...], bm_ref[...], cm_ref[...]], axis=1)
    tail = xpad[...]
    row8 = lax.broadcasted_iota(jnp.int32, (8, CONV_DIM), 0)
    conv = x_cur * cw[3:4, :] + cb_ref[...]
    for k in range(1, SSM_CONV):
        rolled = pltpu.roll(x_cur, k, axis=0)
        head = jnp.where(row8 < k, pltpu.roll(tail, k, axis=0), rolled[0:8])
        conv = conv + jnp.concatenate([head, rolled[8:]], axis=0) * cw[3 - k:4 - k, :]
    xc = _silu(conv)
    xs_c = xc[:, 0:2048]
    bm_c = xc[:, 2048:2560]
    cm_c = xc[:, 2560:3072]

    dt = _softplus(dt_ref[...] + dtb_ref[...])
    a = -jnp.exp(alog_ref[...])
    row = lax.broadcasted_iota(jnp.int32, (ln, ln), 0)
    col = lax.broadcasted_iota(jnp.int32, (ln, ln), 1)
    causal = row >= col
    cs = _dot_hi(causal.astype(F32), dt * a)
    cs_t = cs.T
    dt_t = dt.T

    x_bf = xs_c.astype(BF16)
    bm_bf = bm_c.astype(BF16)
    cm_bf = cm_c.astype(BF16)
    lo = lax.broadcasted_iota(jnp.int32, (1, 128), 1) < 64
    ys = []
    for g in range(SSM_GROUPS):
        bg = bm_bf[:, g * 128:(g + 1) * 128]
        cg = cm_bf[:, g * 128:(g + 1) * 128]
        cb = _dot_nt(cg, bg)
        b_t = bm_c[:, g * 128:(g + 1) * 128].T
        for r2 in range(4):
            pair = g * 4 + r2
            xp = x_bf[:, pair * 128:(pair + 1) * 128]
            yd, st = [], []
            for h in (2 * pair, 2 * pair + 1):
                cs_col = cs[:, h:h + 1]
                cs_row = cs_t[h:h + 1, :]
                dt_row = dt_t[h:h + 1, :]
                lmat = jnp.where(causal, jnp.exp(cs_col - cs_row), 0.0)
                yd.append(_dot((cb * lmat * dt_row).astype(BF16), xp))
                w_row = dt_row * jnp.exp(cs_t[h:h + 1, ln - 1:ln] - cs_row)
                st.append(_dot((b_t * w_row).astype(BF16), xp))
            ha, hb = 2 * pair, 2 * pair + 1
            ecol = jnp.where(lo, jnp.exp(cs[:, ha:ha + 1]), jnp.exp(cs[:, hb:hb + 1]))
            hprev = h_t[pair]
            yoff = _dot(cg, hprev.astype(BF16)) * ecol
            ys.append(jnp.where(lo, yd[0], yd[1]) + yoff)
            edec = jnp.where(lo, jnp.exp(cs_t[ha:ha + 1, ln - 1:ln]), jnp.exp(cs_t[hb:hb + 1, ln - 1:ln]))
            h_t[pair] = hprev * edec + jnp.where(lo, st[0], st[1])
    y = jnp.concatenate(ys, axis=1) + xs_c * dvec_ref[...]
    yz = y * _silu(z_ref[...])
    ms = jnp.mean(yz * yz, axis=-1, keepdims=True)
    y_ref[...] = (yz * lax.rsqrt(ms + EPS) * nw_ref[...]).astype(BF16)

    last = x_cur[ln - 8:ln]
    xpad[...] = last

    @pl.when(c == nchunks - 1)
    def _():
        conv_ref[...] = last[8 - (SSM_CONV - 1):8]
        for pair in range(16):
            hout_ref[pair * 128:(pair + 1) * 128, :] = h_t[pair].T


def _ssd_prompt(proj3, cw, cb_row, dtb_row, alog_row, dvec_row, nw_row):
    bx, tx, _ = proj3.shape
    nchunks = tx // SSD_CHUNK
    ln = SSD_CHUNK
    const = lambda shape: pl.BlockSpec(shape, lambda b, c: (0, 0))
    return pl.pallas_call(
        functools.partial(_ssd_kernel, nchunks=nchunks),
        grid=(bx, nchunks),
        in_specs=[pl.BlockSpec((None, ln, 2048), lambda b, c: (b, c, COL_Z // 2048)),
                  pl.BlockSpec((None, ln, 2048), lambda b, c: (b, c, COL_XS // 2048)),
                  pl.BlockSpec((None, ln, 512), lambda b, c: (b, c, COL_BM // 512)),
                  pl.BlockSpec((None, ln, 512), lambda b, c: (b, c, COL_CM // 512)),
                  pl.BlockSpec((None, ln, 128), lambda b, c: (b, c, COL_TAIL // 128)),
                  const((SSM_CONV, CONV_DIM)), const((1, CONV_DIM)), const((1, 128)), const((1, 128)),
                  const((1, D_INNER)), const((1, D_INNER))],
        out_specs=[pl.BlockSpec((None, ln, D_INNER), lambda b, c: (b, c, 0)),
                   pl.BlockSpec((None, SSM_CONV - 1, CONV_DIM), lambda b, c: (b, 0, 0)),
                   pl.BlockSpec((None, N_SSM_HEADS * SSM_HEAD_DIM, SSM_STATE), lambda b, c: (b, 0, 0))],
        out_shape=[jax.ShapeDtypeStruct((bx, tx, D_INNER), BF16),
                   jax.ShapeDtypeStruct((bx, SSM_CONV - 1, CONV_DIM), F32),
                   jax.ShapeDtypeStruct((bx, N_SSM_HEADS * SSM_HEAD_DIM, SSM_STATE), F32)],
        scratch_shapes=[pltpu.VMEM((8, CONV_DIM), F32), pltpu.VMEM((16, 128, 128), F32)],
        compiler_params=_params(("arbitrary", "arbitrary")),
        name="ssd_prompt",
    )(proj3, proj3, proj3, proj3, proj3, cw, cb_row, dtb_row, alog_row, dvec_row, nw_row)


def _ssm_pre_kernel(xs_ref, bm_ref, cm_ref, dt_ref, s0_ref, s1_ref, s2_ref, cw_ref, cb_ref, dtb_ref, alog_ref,
                    dvec_ref, eh_ref, xdt_ref, dec_ref, yd_ref, bmc_ref, cmc_ref, xdt_t_ref, dec_t_ref):
    cw = cw_ref[...]
    xbc = jnp.concatenate([xs_ref[...], bm_ref[...], cm_ref[...]], axis=1)
    conv = s0_ref[...] * cw[0:1, :] + s1_ref[...] * cw[1:2, :] + s2_ref[...] * cw[2:3, :] + xbc * cw[3:4, :] + cb_ref[...]
    xc = _silu(conv)
    xs_c = xc[:, 0:2048]
    bm_c = xc[:, 2048:2560]
    cm_c = xc[:, 2560:3072]
    dt = _softplus(dt_ref[...] + dtb_ref[...])
    a = -jnp.exp(alog_ref[...])
    dec = jnp.exp(dt * a)
    eh = eh_ref[...]
    dt_e = _dot_hi(dt, eh)
    dec_e = _dot_hi(dec, eh)
    xdt = xs_c * dt_e
    cbs = []
    for g in range(SSM_GROUPS):
        cbg = jnp.sum(cm_c[:, g * 128:(g + 1) * 128] * bm_c[:, g * 128:(g + 1) * 128], axis=-1, keepdims=True)
        cbs.append(jnp.broadcast_to(cbg, (cbg.shape[0], 512)))
    cb_e = jnp.concatenate(cbs, axis=1)
    xdt_ref[...] = xdt
    dec_ref[...] = dec_e
    yd_ref[...] = cb_e * xdt + xs_c * dvec_ref[...]
    bmc_ref[...] = bm_c
    cmc_ref[...] = cm_c
    for k in range(16):
        xdt_t_ref[k * 128:(k + 1) * 128, :] = xdt[:, k * 128:(k + 1) * 128].T
        dec_t_ref[k * 128:(k + 1) * 128, :] = dec_e[:, k * 128:(k + 1) * 128].T


def _ssm_pre(proj3, s0, s1, s2, cw, cb_row, dtb_row, alog_row, dvec_row, eh):
    nb = proj3.shape[1]
    const = lambda shape: pl.BlockSpec(shape, lambda i: (0,) * len(shape))
    return pl.pallas_call(
        _ssm_pre_kernel,
        grid=(1,),
        in_specs=[pl.BlockSpec((None, nb, 2048), lambda i: (0, 0, COL_XS // 2048)),
                  pl.BlockSpec((None, nb, 512), lambda i: (0, 0, COL_BM // 512)),
                  pl.BlockSpec((None, nb, 512), lambda i: (0, 0, COL_CM // 512)),
                  pl.BlockSpec((None, nb, 128), lambda i: (0, 0, COL_TAIL // 128)),
                  const((nb, CONV_DIM)), const((nb, CONV_DIM)), const((nb, CONV_DIM)),
                  const((SSM_CONV, CONV_DIM)), const((1, CONV_DIM)), const((1, 128)), const((1, 128)),
                  const((1, D_INNER)), const((128, D_INNER))],
        out_specs=[const((nb, D_INNER)), const((nb, D_INNER)), const((nb, D_INNER)), const((nb, 512)),
                   const((nb, 512)), const((D_INNER, nb)), const((D_INNER, nb))],
        out_shape=[jax.ShapeDtypeStruct((nb, D_INNER), F32)] * 3 + [jax.ShapeDtypeStruct((nb, 512), F32)] * 2
        + [jax.ShapeDtypeStruct((D_INNER, nb), F32)] * 2,
        compiler_params=_params(("arbitrary",)),
        name="ssm_pre",
    )(proj3, proj3, proj3, proj3, s0, s1, s2, cw, cb_row, dtb_row, alog_row, dvec_row, eh)


SSM_STATE_ROWS = 4


def _ssm_state_kernel(h0_ref, xdt_t_ref, dec_t_ref, bm_ref, cm_ref, hn_ref, yoff_ref):
    shift = (128 - pl.program_id(0) * SSM_STATE_ROWS) & 127
    xdt_r = pltpu.roll(xdt_t_ref[...], shift, axis=1)
    dec_r = pltpu.roll(dec_t_ref[...], shift, axis=1)
    for bb in range(SSM_STATE_ROWS):
        h0 = h0_ref[bb]
        xcol = xdt_r[:, bb:bb + 1]
        outs, yoffs = [], []
        for g in range(SSM_GROUPS):
            rows = slice(g * 512, (g + 1) * 512)
            outs.append(xcol[rows] * bm_ref[bb, :, g * 128:(g + 1) * 128])
            cm8 = jnp.broadcast_to(cm_ref[bb, :, g * 128:(g + 1) * 128], (8, 128)).astype(BF16)
            yoffs.append(_dot_nt(cm8, h0[rows].astype(BF16)))
        hn_ref[bb] = h0 * dec_r[:, bb:bb + 1] + jnp.concatenate(outs, axis=0)
        yoff_ref[bb] = jnp.concatenate(yoffs, axis=1)


def _ssm_state(h0, xdt_t, dec_t, bm_c, cm_c):
    nb = h0.shape[0]
    nr = SSM_STATE_ROWS
    assert nb == 128 and nb % nr == 0
    const = lambda shape: pl.BlockSpec(shape, lambda b: (0, 0))
    return pl.pallas_call(
        _ssm_state_kernel,
        grid=(nb // nr,),
        in_specs=[pl.BlockSpec((nr, D_INNER, SSM_STATE), lambda b: (b, 0, 0)),
                  const((D_INNER, nb)), const((D_INNER, nb)),
                  pl.BlockSpec((nr, 1, 512), lambda b: (b, 0, 0)), pl.BlockSpec((nr, 1, 512), lambda b: (b, 0, 0))],
        out_specs=[pl.BlockSpec((nr, D_INNER, SSM_STATE), lambda b: (b, 0, 0)),
                   pl.BlockSpec((nr, 8, D_INNER), lambda b: (b, 0, 0))],
        out_shape=[jax.ShapeDtypeStruct((nb, D_INNER, SSM_STATE), F32),
                   jax.ShapeDtypeStruct((nb, 8, D_INNER), F32)],
        compiler_params=_params(("arbitrary",)),
        name="ssm_state",
    )(h0, xdt_t, dec_t, bm_c.reshape(nb, 1, 512), cm_c.reshape(nb, 1, 512))


def _ssm_post_kernel(yd_ref, yoff_ref, dec_ref, z_ref, nw_ref, y_ref):
    y = yd_ref[...] + yoff_ref[...] * dec_ref[...]
    yz = y * _silu(z_ref[...])
    ms = jnp.mean(yz * yz, axis=-1, keepdims=True)
    y_ref[...] = (yz * lax.rsqrt(ms + EPS) * nw_ref[...]).astype(BF16)


def _ssm_post(yd, yoff, dec_e, proj3, nw_row):
    nb = yd.shape[0]
    const = lambda shape: pl.BlockSpec(shape, lambda i: (0, 0))
    return pl.pallas_call(
        _ssm_post_kernel,
        grid=(1,),
        in_specs=[const((nb, D_INNER)), const((nb, D_INNER)), const((nb, D_INNER)),
                  pl.BlockSpec((None, nb, 2048), lambda i: (0, 0, COL_Z // 2048)), const((1, D_INNER))],
        out_specs=const((nb, D_INNER)),
        out_shape=jax.ShapeDtypeStruct((nb, D_INNER), BF16),
        compiler_params=_params(("arbitrary",)),
        name="ssm_post",
    )(yd, yoff, dec_e, proj3, nw_row)


def _rope128(x, c, s1, s2):
    return x * c + pltpu.roll(x, 120, axis=1) * s1 + pltpu.roll(x, 8, axis=1) * s2


def _rope_kernel(q_ref, k01_ref, k23_ref, k45_ref, tail_ref, c_ref, s1_ref, s2_ref,
                 qo_ref, kvrows_ref, ksa_ref, khm_ref, gates_ref, win_ref, *, ntiles, tr):
    c = c_ref[...]
    s1 = s1_ref[...]
    s2 = s2_ref[...]
    q = q_ref[...]
    qo_ref[...] = (jnp.concatenate(
        [_rope128(q[:, k * 128:(k + 1) * 128], c, s1, s2) for k in range(8)], axis=1
    ) * (ATT_SCALE * LOG2E)).astype(BF16)
    streams = []
    for pref in (k01_ref, k23_ref, k45_ref):
        blk = pref[...]
        kk = jnp.concatenate([_rope128(blk[:, k * 128:(k + 1) * 128], c, s1, s2) for k in range(2)], axis=1)
        streams.append(kk)
        streams.append(blk[:, 256:512])
    kvrows_ref[...] = jnp.concatenate(streams[0:4], axis=1)
    pos = pl.program_id(1) * tr + lax.broadcasted_iota(jnp.int32, (tr, 64), 0)
    own = lax.broadcasted_iota(jnp.int32, (tr, 64), 1) == lax.shift_right_logical(pos, 6)
    extra = jnp.where(own, NEG, 0.0).astype(BF16)
    ksel = streams[2].astype(BF16)
    vsel = streams[3].astype(BF16)
    ones_col = (lax.broadcasted_iota(jnp.int32, (tr, 64), 1) == 0).astype(BF16)
    for h in range(N_KV_HEADS):
        ksa_ref[h] = jnp.concatenate([ksel[:, h * 64:(h + 1) * 64], extra], axis=1)
        ksa_ref[N_KV_HEADS + h] = jnp.concatenate([vsel[:, h * 64:(h + 1) * 64], ones_col], axis=1)
    for i, s in enumerate((3, 4, 5)):
        sb = streams[s].astype(BF16)
        for h in range(N_KV_HEADS):
            khm_ref[i * 4 + h] = sb[:, h * 64:(h + 1) * 64]
    g = _sig(tail_ref[...])
    for hk in range(N_KV_HEADS):
        gates_ref[hk] = g[:, 32 + hk * 12:32 + (hk + 1) * 12]

    @pl.when(pl.program_id(1) == ntiles - 1)
    def _():
        win_ref[...] = jnp.concatenate(streams[4:6], axis=1)


def _rope_prep(proj3, ctab, s1tab, s2tab, tr):
    bx, tx, _ = proj3.shape
    ntiles = tx // tr
    tab = pl.BlockSpec((tr, 128), lambda b, i: (i, 0))
    return pl.pallas_call(
        functools.partial(_rope_kernel, ntiles=ntiles, tr=tr),
        grid=(bx, ntiles),
        in_specs=[pl.BlockSpec((None, tr, 1024), lambda b, i: (b, i, COL_Q // 1024)),
                  pl.BlockSpec((None, tr, 512), lambda b, i: (b, i, COL_KV // 512)),
                  pl.BlockSpec((None, tr, 512), lambda b, i: (b, i, COL_KV // 512 + 1)),
                  pl.BlockSpec((None, tr, 512), lambda b, i: (b, i, COL_KV // 512 + 2)),
                  pl.BlockSpec((None, tr, 128), lambda b, i: (b, i, COL_TAIL // 128)),
                  tab, tab, tab],
        out_specs=[pl.BlockSpec((None, tr, 1024), lambda b, i: (b, i, 0)),
                   pl.BlockSpec((None, tr, 1024), lambda b, i: (b, i, 0)),
                   pl.BlockSpec((None, 8, tr, 128), lambda b, i: (b, 0, i, 0)),
                   pl.BlockSpec((None, 12, tr, 64), lambda b, i: (b, 0, i, 0)),
                   pl.BlockSpec((None, 4, tr, 12), lambda b, i: (b, 0, i, 0)),
                   pl.BlockSpec((None, tr, 512), lambda b, i: (b, 0, 0))],
        out_shape=[jax.ShapeDtypeStruct((bx, tx, 1024), BF16),
                   jax.ShapeDtypeStruct((bx, tx, 1024), F32),
                   jax.ShapeDtypeStruct((bx, 8, tx, 128), BF16),
                   jax.ShapeDtypeStruct((bx, 12, tx, 64), BF16),
                   jax.ShapeDtypeStruct((bx, 4, tx, 12), F32),
                   jax.ShapeDtypeStruct((bx, tr, 512), F32)],
        compiler_params=_params(("arbitrary", "arbitrary")),
        name="rope_prep",
    )(proj3, proj3, proj3, proj3, proj3, ctab, s1tab, s2tab)


def _compress_kernel(*refs, nrefs, npages, head_major):
    if nrefs > 1:
        refs = refs[1:]
    page_refs = refs[:nrefs]
    w1ab_ref, w1f_ref, pe_ref, w2_ref, kc_ref, vc_ref = refs[nrefs:]
    nsub = PAGE_SIZE // CMP_STRIDE
    nj = npages * nsub
    nr = N_KV_HEADS * nj
    ri = lax.broadcasted_iota(jnp.int32, (PAGE_SIZE, PAGE_SIZE), 0)
    ci = lax.broadcasted_iota(jnp.int32, (PAGE_SIZE, PAGE_SIZE), 1)
    perm = (ci == (ri & (nsub - 1)) * CMP_STRIDE + lax.shift_right_logical(ri, 3)).astype(BF16)
    if nrefs == 1:
        pages = [page_refs[0][p * PAGE_SIZE:(p + 1) * PAGE_SIZE, :] for p in range(npages)]
        grouped = [_dot(perm, pg.astype(BF16)) for pg in pages]
    else:
        grouped = [_dot_nt(perm, pr[...].reshape(2 * N_KV_HEADS * ATT_HEAD_DIM, PAGE_SIZE).astype(BF16))
                   for pr in page_refs]
    for st, o_ref in ((0, kc_ref), (1, vc_ref)):
        heads = []
        for h in range(N_KV_HEADS):
            c0 = st * 256 + h * ATT_HEAD_DIM
            rows = [jnp.concatenate([y[s * nsub:(s + 1) * nsub, c0:c0 + ATT_HEAD_DIM] for s in range(CMP_STRIDE)],
                                    axis=1) for y in grouped]
            heads.append(rows[0] if npages == 1 else jnp.concatenate(rows, axis=0))
        a = jnp.concatenate(heads, axis=0).astype(BF16)
        p2 = _dot(a, w1ab_ref[st])
        pe8 = jnp.broadcast_to(pe_ref[st], (8, CMP_LEN * ATT_HEAD_DIM)).astype(BF16)
        pe_t = _dot(pe8, w1f_ref[st])[0:1, :]
        hid = p2[:, 0:CMP_HIDDEN] + pltpu.roll(p2[:, CMP_HIDDEN:2 * CMP_HIDDEN], nr - 1, axis=0) + pe_t
        out = _dot(_silu(hid).astype(BF16), w2_ref[st]).astype(BF16)
        if head_major:
            for h in range(N_KV_HEADS):
                o_ref[h] = out[h * nj:(h + 1) * nj, :]
        else:
            o_ref[...] = jnp.concatenate([out[h * nj:(h + 1) * nj, :] for h in range(N_KV_HEADS)], axis=1)


def _compress_prompt(kv_rows, wbd, w1f, pe, w2bd):
    bx, tx, _ = kv_rows.shape
    nj = tx // CMP_STRIDE
    c4 = lambda shape: pl.BlockSpec(shape, lambda b: (0,) * len(shape))
    return pl.pallas_call(
        functools.partial(_compress_kernel, nrefs=1, npages=tx // PAGE_SIZE, head_major=True),
        grid=(bx,),
        in_specs=[pl.BlockSpec((None, tx, 512), lambda b: (b, 0, 0)),
                  c4(wbd.shape), c4(w1f.shape), c4(pe.shape), c4(w2bd.shape)],
        out_specs=[pl.BlockSpec((None, 4, nj, 64), lambda b: (b, 0, 0, 0))] * 2,
        out_shape=[jax.ShapeDtypeStruct((bx, 4, nj, 64), BF16)] * 2,
        compiler_params=_params(("arbitrary",)),
        name="compress_prompt",
    )(kv_rows, wbd, w1f, pe, w2bd)


def _compress_sample(page_table, cache3, wbd, w1f, pe, w2bd):
    nb, npages = page_table.shape
    nsub = PAGE_SIZE // CMP_STRIDE
    nj = npages * nsub
    c4 = lambda shape: pl.BlockSpec(shape, lambda b, pt: (0,) * len(shape))
    page_specs = [pl.BlockSpec((None, 2, N_KV_HEADS, ATT_HEAD_DIM, PAGE_SIZE),
                               functools.partial(lambda b, pt, p: (pt[b, p], 0, 0, 0, 0), p=p))
                  for p in range(npages)]
    grid_spec = pltpu.PrefetchScalarGridSpec(
        num_scalar_prefetch=1,
        grid=(nb,),
        in_specs=page_specs + [c4(wbd.shape), c4(w1f.shape), c4(pe.shape), c4(w2bd.shape)],
        out_specs=[pl.BlockSpec((None, nj, 256), lambda b, pt: (b, 0, 0))] * 2,
    )
    return pl.pallas_call(
        functools.partial(_compress_kernel, nrefs=npages, npages=npages, head_major=False),
        grid_spec=grid_spec,
        out_shape=[jax.ShapeDtypeStruct((nb, nj, 256), BF16)] * 2,
        compiler_params=_params(("arbitrary",)),
        name="compress_sample",
    )(page_table, *([cache3] * npages), wbd, w1f, pe, w2bd)


KEY_BLOCK = 1024
NSA_QUERY_BLOCK = 256


def _nsa_prompt_kernel(q_ref, g_ref, kc_ref, vc_ref, ksa_ref, vs_ref, kw_ref, vw_ref, ovt_ref,
                       o_ref, s_scr, m_scr, acc_scr, *, qblk):
    qb = pl.program_id(2)
    q0 = qb * qblk
    q4 = q_ref[...]
    nq = Q_PER_KV * qblk
    qs = jnp.concatenate([q4[:, r * 64:(r + 1) * 64] for r in range(Q_PER_KV)], axis=0)
    ncmp = kc_ref.shape[0]
    trow = q0 + (lax.broadcasted_iota(jnp.int32, (nq, 1), 0) & (qblk - 1))

    jl = lax.broadcasted_iota(jnp.int32, (nq, ncmp), 1)
    mask_c = jl <= lax.shift_right_arithmetic(trow - (CMP_LEN - 1), CMP_STRIDE.bit_length() - 1)
    s = jnp.where(mask_c, _dot_nt(qs, kc_ref[...]), NEG)
    e = jnp.where(mask_c, jnp.exp2(s - jnp.max(s, axis=-1, keepdims=True)), 0.0)
    p = e / jnp.maximum(jnp.sum(e, axis=-1, keepdims=True), TINY)
    o_c = _dot(p.astype(BF16), vc_ref[...])
    psum = p[0:qblk] + p[qblk:2 * qblk] + p[2 * qblk:3 * qblk] + p[3 * qblk:4 * qblk]
    imp_t = _dot_nt_hi(ovt_ref[...], psum)
    nblk = imp_t.shape[0]
    tq = q0 + lax.broadcasted_iota(jnp.int32, (nblk, qblk), 1)
    blk = lax.broadcasted_iota(jnp.int32, (nblk, qblk), 0)
    cur = lax.shift_right_logical(tq, 6)
    valid = blk * SEL_LEN <= tq
    forced = (blk == 0) | (blk == cur) | (blk == cur - 1)
    imp = jnp.where(valid, jnp.where(forced, BIG, imp_t), -BIG)
    row8 = lax.broadcasted_iota(jnp.int32, (8, qblk), 0)
    groups = [imp[8 * v:8 * v + 8] for v in range(nblk // 8)]
    ranks = [jnp.zeros((8, qblk), F32) for _ in groups]
    for j in range(nblk):
        rj = imp[j:j + 1, :]
        for v in range(len(groups)):
            if 8 * v > j:
                before = rj >= groups[v]
            elif 8 * v + 7 <= j:
                before = rj > groups[v]
            else:
                before = (rj > groups[v]) | ((rj == groups[v]) & (row8 > j - 8 * v))
            ranks[v] = ranks[v] + jnp.where(before, 1.0, 0.0)
    rank = jnp.concatenate(ranks, axis=0)
    nsel_t = jnp.where(rank < float(SEL_TOPK), 0.0, 1.0)
    nsel_pad = jnp.concatenate([nsel_t, jnp.zeros((128 - nblk, qblk), F32)], axis=0) if nblk < 128 else nsel_t
    nsel = jnp.concatenate([nsel_pad[:, i * 128:(i + 1) * 128].T for i in range(qblk // 128)],
                           axis=0)[:, 0:64].astype(BF16)
    qaug = jnp.concatenate([qs, jnp.concatenate([nsel] * Q_PER_KV, axis=0)], axis=1)

    wlen = WINDOW + qblk
    wstart = pl.multiple_of(jnp.maximum(q0 - WINDOW, 0), 128)
    kw = kw_ref[pl.ds(wstart, wlen), :]
    vw = vw_ref[pl.ds(wstart, wlen), :]
    d = trow[0:qblk] - (wstart + lax.broadcasted_iota(jnp.int32, (qblk, wlen), 1))
    ok = (d >= 0) & (d < WINDOW)
    o_w = []
    for r in range(Q_PER_KV):
        sw = jnp.where(ok, _dot_nt(qs[r * qblk:(r + 1) * qblk], kw), NEG)
        ew = jnp.exp2(sw - jnp.max(sw, axis=-1, keepdims=True))
        o_w.append(_dot(ew.astype(BF16), vw) / jnp.maximum(jnp.sum(ew, axis=-1, keepdims=True), TINY))

    cw = 256
    nck = KEY_BLOCK // cw

    col_minus_row = (lax.broadcasted_iota(jnp.int32, (nq, cw), 1)
                     - (lax.broadcasted_iota(jnp.int32, (nq, cw), 0) & (qblk - 1)))

    def block_scores(jb, chunks, causal):
        base = pl.multiple_of(jb * KEY_BLOCK, KEY_BLOCK)
        mx = None
        for c in chunks:
            sc = _dot_nt(qaug, ksa_ref[pl.ds(base + c * cw, cw), :])
            if causal:
                sc = jnp.where(col_minus_row <= q0 - base - c * cw, sc, NEG)
            s_scr[jb, :, c * cw:(c + 1) * cw] = sc
            for i in range(cw // 128):
                part = sc[:, i * 128:(i + 1) * 128]
                mx = part if mx is None else jnp.maximum(mx, part)
        m_scr[...] = jnp.maximum(m_scr[...], mx)

    def block_values(jb, chunks):
        base = pl.multiple_of(jb * KEY_BLOCK, KEY_BLOCK)
        lo, hi = chunks[0] * cw, (chunks[-1] + 1) * cw
        ps = [jnp.exp2(s_scr[jb, :, c * 128:(c + 1) * 128] - mb).astype(BF16) for c in range(lo // 128, hi // 128)]
        acc_scr[...] = acc_scr[...] + _dot(jnp.concatenate(ps, axis=1), vs_ref[pl.ds(base + lo, hi - lo), :])

    every = tuple(range(nck))
    nfull = lax.shift_right_logical(qb, (KEY_BLOCK // qblk).bit_length() - 1)
    m_scr[...] = jnp.full((nq, 128), -jnp.inf, F32)

    def pass_a(jb, carry):
        block_scores(jb, every, False)
        return carry

    lax.fori_loop(0, nfull, pass_a, 0)
    block_scores(nfull, every, True)
    mb =jnp.broadcast_to(jnp.max(m_scr[...], axis=-1, keepdims=True), (nq, 128))

    acc_scr[...] = jnp.zeros((nq, 128), F32)

    def pass_b(jb, carry):
        block_values(jb, every)
        return carry

    lax.fori_loop(0, nfull, pass_b, 0)
    block_values(nfull, every)
    acc = acc_scr[...]
    o_s = acc[:, 0:64] / jnp.maximum(acc[:, 64:65], TINY)

    g = g_ref[...]
    outs = []
    for r in range(Q_PER_KV):
        rows = slice(r * qblk, (r + 1) * qblk)
        outs.append(g[:, 3 * r:3 * r + 1] * o_c[rows] + g[:, 3 * r + 1:3 * r + 2] * o_s[rows]
                    + g[:, 3 * r + 2:3 * r + 3] * o_w[r])
    o_ref[...] = jnp.concatenate(outs, axis=1).astype(BF16)


def _nsa_prompt(q_r, gates, kc, vc, ksa, khm, ovt):
    bx, tx, _ = q_r.shape
    qblk = NSA_QUERY_BLOCK
    ntiles = tx // qblk
    ncmp = kc.shape[2]
    nq = Q_PER_KV * qblk
    assert tx % KEY_BLOCK == 0 and tx >= WINDOW + qblk and tx <= 64 * SEL_LEN
    kv_spec = lambda s: pl.BlockSpec((None, None, tx, 64), lambda b, h, i: (b, s * 4 + h, 0, 0))
    return pl.pallas_call(
        functools.partial(_nsa_prompt_kernel, qblk=qblk),
        grid=(bx, N_KV_HEADS, ntiles),
        in_specs=[pl.BlockSpec((None, qblk, 256), lambda b, h, i: (b, i, h)),
                  pl.BlockSpec((None, None, qblk, 12), lambda b, h, i: (b, h, i, 0)),
                  pl.BlockSpec((None, None, ncmp, 64), lambda b, h, i: (b, h, 0, 0)),
                  pl.BlockSpec((None, None, ncmp, 64), lambda b, h, i: (b, h, 0, 0)),
                  pl.BlockSpec((None, None, tx, 128), lambda b, h, i: (b, h, 0, 0)),
                  pl.BlockSpec((None, None, tx, 128), lambda b, h, i: (b, N_KV_HEADS + h, 0, 0)),
                  kv_spec(1), kv_spec(2),
                  pl.BlockSpec(ovt.shape, lambda b, h, i: (0, 0))],
        out_specs=pl.BlockSpec((None, qblk, 256), lambda b, h, i: (b, i, h)),
        out_shape=jax.ShapeDtypeStruct((bx, tx, ATT_WIDTH), BF16),
        scratch_shapes=[pltpu.VMEM((tx // KEY_BLOCK, nq, KEY_BLOCK), F32), pltpu.VMEM((nq, 128), F32),
                        pltpu.VMEM((nq, 128), F32)],
        compiler_params=_params(("arbitrary", "arbitrary", "arbitrary")),
        name="nsa_prompt",
    )(q_r, gates, kc, vc, ksa, ksa, khm, khm, ovt)


NSA_SAMPLE_ROWS = 4


def _fold_heads(o256, hmask):
    o = o256 * hmask
    return o[:, 0:64] + o[:, 64:128] + o[:, 128:192] + o[:, 192:256]


def _nsa_sample_kernel(*refs, npages, nbb):
    refs = refs[1:]
    pages = refs[:npages * nbb]
    per_row = refs[npages * nbb:npages * nbb + 7]
    ov_ref, e_ref, o_ref, wout_ref = refs[npages * nbb + 7:]
    for bb in range(nbb):
        _nsa_sample_row(pages[bb * npages:(bb + 1) * npages], *[r.at[bb] for r in per_row], ov_ref, e_ref,
                        o_ref.at[bb], wout_ref.at[bb])


def _nsa_sample_row(pages, q_ref, g_ref, knew_ref, wnew_ref, kc_ref, vc_ref, win_ref, ov_ref, e_ref, o_ref, wout_ref):
    npages = len(pages)
    past = npages * PAGE_SIZE
    ncmp = kc_ref.shape[0]
    ntok = (past - CMP_LEN) // CMP_STRIDE + 1
    nsel = past // SEL_LEN + 1

    q16 = q_ref[...].astype(F32)
    rowh = lax.broadcasted_iota(jnp.int32, (16, 256), 0)
    laneh = lax.broadcasted_iota(jnp.int32, (16, 256), 1)
    hmask = (lax.shift_right_logical(rowh, 2) == lax.shift_right_logical(laneh, 6)).astype(F32)
    qbd_f = jnp.concatenate([q16] * 4, axis=1) * hmask
    qbd = qbd_f.astype(BF16)

    jl = lax.broadcasted_iota(jnp.int32, (16, ncmp), 1)
    mask_c = jl < ntok
    s = jnp.where(mask_c, _dot_nt(qbd, kc_ref[...]), NEG)
    e = jnp.where(mask_c, jnp.exp2(s - jnp.max(s, axis=-1, keepdims=True)), 0.0)
    p_c = e / jnp.maximum(jnp.sum(e, axis=-1, keepdims=True), TINY)
    o_c = _fold_heads(_dot(p_c.astype(BF16), vc_ref[...]), hmask)

    imp = _dot(p_c.astype(BF16), ov_ref[...])
    imp = imp + pltpu.roll(imp, 1, axis=0)
    imp = imp + pltpu.roll(imp, 2, axis=0)
    blk = lax.broadcasted_iota(jnp.int32, (16, 128), 1)
    exists = blk < nsel
    forced = (blk == 0) | (blk == nsel - 1) | (blk == nsel - 2)
    imp = jnp.where(exists, jnp.where(forced, BIG, imp), -BIG)
    rank = jnp.zeros((16, 128), F32)
    for j in range(nsel):
        cj = imp[:, j:j + 1]
        beats = (cj > imp) | ((cj == imp) & (blk > j))
        rank = rank + jnp.where(beats, 1.0, 0.0)
    group_row = (lax.broadcasted_iota(jnp.int32, (16, 128), 0) & 3) == 3
    sel16 = jnp.where((rank < float(min(SEL_TOPK, nsel))) & exists & group_row, 1.0, 0.0)
    sel16 = sel16 + pltpu.roll(sel16, 15, axis=0)
    sel16 = sel16 + pltpu.roll(sel16, 14, axis=0)
    bias_past = (_dot(sel16.astype(BF16), e_ref[...]) - 1.0) * BIG

    knew = knew_ref[...]
    s_new = jnp.sum(qbd_f * knew[:, 512:768], axis=-1, keepdims=True)
    hd = N_KV_HEADS * ATT_HEAD_DIM
    s_past = jnp.concatenate(
        [_dot(qbd, pg[0].reshape(hd, PAGE_SIZE).astype(BF16)) for pg in pages], axis=1) + bias_past
    m = jnp.maximum(jnp.max(s_past, axis=-1, keepdims=True), s_new)
    e_past = jnp.exp2(s_past - m)
    e_new = jnp.exp2(s_new - m)
    acc = e_new * knew[:, 768:1024]
    for p, pg in enumerate(pages):
        acc = acc + _dot_nt(e_past[:, p * 128:(p + 1) * 128].astype(BF16), pg[1].reshape(hd, PAGE_SIZE).astype(BF16))
    lsum = jnp.sum(e_past, axis=-1, keepdims=True) + e_new
    o_s = _fold_heads(acc / jnp.maximum(lsum, TINY), hmask)

    kw_t = win_ref[0].reshape(hd, WINDOW)
    vw_t = win_ref[1].reshape(hd, WINDOW)
    wnew = wnew_ref[...]
    wl = lax.broadcasted_iota(jnp.int32, (16, WINDOW), 1)
    s_w = jnp.where(wl >= 1, _dot(qbd, kw_t.astype(BF16)), NEG)
    s_wn = jnp.sum(qbd_f * wnew[:, 0:256], axis=-1, keepdims=True)
    mw = jnp.maximum(jnp.max(s_w, axis=-1, keepdims=True), s_wn)
    e_w = jnp.where(wl >= 1, jnp.exp2(s_w - mw), 0.0)
    e_wn = jnp.exp2(s_wn - mw)
    acc_w = _dot_nt(e_w.astype(BF16), vw_t.astype(BF16)) + e_wn * wnew[:, 256:512]
    o_w = _fold_heads(acc_w / jnp.maximum(jnp.sum(e_w, axis=-1, keepdims=True) + e_wn, TINY), hmask)

    g = _sig(g_ref[...])
    o_ref[...] = (g[:, 0:1] * o_c + g[:, 1:2] * o_s + g[:, 2:3] * o_w).astype(BF16)

    last = lax.broadcasted_iota(jnp.int32, (hd, 128), 1) == 127
    for i, src in enumerate((kw_t, vw_t)):
        col = jnp.broadcast_to(wnew[:, i * hd:(i + 1) * hd], (128, hd)).T
        rolled = pltpu.roll(src, WINDOW - 1, axis=1)
        out = jnp.concatenate([rolled[:, 0:WINDOW - 128], jnp.where(last, col, rolled[:, WINDOW - 128:])], axis=1)
        wout_ref[i] = out.reshape(N_KV_HEADS, ATT_HEAD_DIM, WINDOW)


def _nsa_sample(page_table, cache3, q16, g16, knew, wnew, kc, vc, win, ov, emat):
    nb, npages = page_table.shape
    ncmp = kc.shape[1]
    c2 = lambda shape: pl.BlockSpec(shape, lambda b, pt: (0,) * len(shape))
    nbb = NSA_SAMPLE_ROWS
    assert nb % nbb == 0
    page_specs = [pl.BlockSpec((None, 2, N_KV_HEADS, ATT_HEAD_DIM, PAGE_SIZE),
                               functools.partial(lambda b, pt, bb, p: (pt[b * nbb + bb, p], 1, 0, 0, 0), bb=bb, p=p))
                  for bb in range(nbb) for p in range(npages)]
    win_spec = pl.BlockSpec((nbb, 2, N_KV_HEADS, ATT_HEAD_DIM, WINDOW), lambda b, pt: (b, 0, 0, 0, 0))
    rows = lambda r, c: pl.BlockSpec((nbb, r, c), lambda b, pt: (b, 0, 0))
    grid_spec = pltpu.PrefetchScalarGridSpec(
        num_scalar_prefetch=1,
        grid=(nb // nbb,),
        in_specs=page_specs + [rows(16, 64), rows(16, 3), rows(1, 1024), rows(1, 512), rows(ncmp, 256),
                               rows(ncmp, 256), win_spec, c2(ov.shape), c2(emat.shape)],
        out_specs=[rows(16, 64), win_spec],
    )
    return pl.pallas_call(
        functools.partial(_nsa_sample_kernel, npages=npages, nbb=nbb),
        grid_spec=grid_spec,
        out_shape=[jax.ShapeDtypeStruct((nb, 16, 64), BF16),
                   jax.ShapeDtypeStruct((nb, 2, N_KV_HEADS, ATT_HEAD_DIM, WINDOW), F32)],
        compiler_params=_params(("arbitrary",)),
        name="nsa_sample",
    )(page_table, *([cache3] * (npages * nbb)), q16, g16, knew, wnew, kc, vc, win, ov, emat)


def _merge_kernel(x_ref, yssm_ref, yatt_ref, gs_ref, ga_ref, g1_ref, sh2_ref, sc2_ref, nw_ref,
                  wss_ref, wat_ref, wo_ref, x1_ref, h2_ref):
    ms = _dot(yssm_ref[...], wss_ref[...])
    ma = _dot(yatt_ref[...], wat_ref[...])
    merged = _sig(gs_ref[...]) * ms + _sig(ga_ref[...]) * ma
    x1 = x_ref[...] + g1_ref[...] * _dot(merged.astype(BF16), wo_ref[...])
    x1_ref[...] = x1
    y = x1 * lax.rsqrt(jnp.mean(x1 * x1, axis=-1, keepdims=True) + EPS)
    h2_ref[...] = (y * nw_ref[...] * (1.0 + sc2_ref[...]) + sh2_ref[...]).astype(BF16)


def _merge_out(x3, yssm, yatt, proj3, mod3, nw_row, wss, wat, wo, tm):
    bx, tx, _ = x3.shape
    r = mod3.shape[1]
    mod_b = (lambda b: b) if mod3.shape[0] == bx else (lambda b: 0)
    row = lambda w, cb: pl.BlockSpec((None, tm, w), lambda b, i: (b, i, cb))
    modc = lambda cb: pl.BlockSpec((None, r, D_MODEL), lambda b, i: (mod_b(b), 0, cb))
    const = lambda shape: pl.BlockSpec(shape, lambda b, i: (0, 0))
    return pl.pallas_call(
        _merge_kernel,
        grid=(bx, tx // tm),
        in_specs=[row(D_MODEL, 0), row(D_INNER, 0), row(ATT_WIDTH, 0),
                  row(1024, COL_GM // 1024), row(1024, COL_GM // 1024 + 1),
                  modc(2), modc(3), modc(4), const((1, D_MODEL)),
                  const(wss.shape), const(wat.shape), const(wo.shape)],
        out_specs=[row(D_MODEL, 0), row(D_MODEL, 0)],
        out_shape=[jax.ShapeDtypeStruct((bx, tx, D_MODEL), F32), jax.ShapeDtypeStruct((bx, tx, D_MODEL), BF16)],
        compiler_params=_params(("arbitrary", "arbitrary")),
        name="merge_out",
    )(x3, yssm, yatt, proj3, proj3, mod3, mod3, mod3, nw_row, wss, wat, wo)


FFN_TN = 1408
FFN_NT = D_FF // FFN_TN


def _ffn_up_prompt_kernel(h2_ref, wa_ref, wb_ref, cwa_ref, cwb_ref, cba_ref, cbb_ref,
                          act_ref, fa_ref, fb_ref, tails, *, tm, ntiles):
    i = pl.program_id(1)
    j = pl.program_id(2)
    h2 = h2_ref[...]
    row8 = lax.broadcasted_iota(jnp.int32, (8, FFN_TN), 0)

    @pl.when(i == 0)
    def _():
        tails[j] = jnp.zeros((2, 8, FFN_TN), F32)

    halves = []
    for half, (w_ref, cw_ref, cb_ref, f_ref) in enumerate(((wa_ref, cwa_ref, cba_ref, fa_ref),
                                                            (wb_ref, cwb_ref, cbb_ref, fb_ref))):
        u = _dot(h2, w_ref[...])
        cw = cw_ref[...]
        tail = tails[j, half]
        conv = u * cw[FFN_CONV - 1:FFN_CONV, :] + cb_ref[...]
        for k in range(1, FFN_CONV):
            rolled = pltpu.roll(u, k, axis=0)
            head = jnp.where(row8 < k, pltpu.roll(tail, k, axis=0), rolled[0:8])
            conv = conv + jnp.concatenate([head, rolled[8:]], axis=0) * cw[FFN_CONV - 1 - k:FFN_CONV - k, :]
        halves.append(conv)
        last = u[tm - 8:tm]
        tails[j, half] = last
        f_ref[...] = last[8 - (FFN_CONV - 1):8]
    act_ref[...] = (_silu(halves[0]) * halves[1]).astype(BF16)


def _ffn_up_prompt(h2, wup, cw, cb_row, tm):
    bx, tx, _ = h2.shape
    ntiles = tx // tm
    return pl.pallas_call(
        functools.partial(_ffn_up_prompt_kernel, tm=tm, ntiles=ntiles),
        grid=(bx, ntiles, FFN_NT),
        in_specs=[pl.BlockSpec((None, tm, D_MODEL), lambda b, i, j: (b, i, 0)),
                  pl.BlockSpec((D_MODEL, FFN_TN), lambda b, i, j: (0, j)),
                  pl.BlockSpec((D_MODEL, FFN_TN), lambda b, i, j: (0, j + FFN_NT)),
                  pl.BlockSpec((FFN_CONV, FFN_TN), lambda b, i, j: (0, j)),
                  pl.BlockSpec((FFN_CONV, FFN_TN), lambda b, i, j: (0, j + FFN_NT)),
                  pl.BlockSpec((1, FFN_TN), lambda b, i, j: (0, j)),
                  pl.BlockSpec((1, FFN_TN), lambda b, i, j: (0, j + FFN_NT))],
        out_specs=[pl.BlockSpec((None, tm, FFN_TN), lambda b, i, j: (b, i, j)),
                   pl.BlockSpec((None, None, FFN_CONV - 1, FFN_TN), lambda b, i, j: (b, i, 0, j)),
                   pl.BlockSpec((None, None, FFN_CONV - 1, FFN_TN), lambda b, i, j: (b, i, 0, j))],
        out_shape=[jax.ShapeDtypeStruct((bx, tx, D_FF), BF16),
                   jax.ShapeDtypeStruct((bx, ntiles, FFN_CONV - 1, D_FF), F32),
                   jax.ShapeDtypeStruct((bx, ntiles, FFN_CONV - 1, D_FF), F32)],
        scratch_shapes=[pltpu.VMEM((FFN_NT, 2, 8, FFN_TN), F32)],
        compiler_params=_params(("arbitrary", "arbitrary", "arbitrary")),
        name="ffn_up_prompt",
    )(h2, wup, wup, cw, cw, cb_row, cb_row)


def _ffn_up_sample_kernel(h2_ref, wa_ref, wb_ref, h0a_ref, h0b_ref, h1a_ref, h1b_ref, cwa_ref, cwb_ref,
                          cba_ref, cbb_ref, act_ref, ua_ref, ub_ref):
    h2 = h2_ref[...]
    ua = _dot(h2, wa_ref[...])
    ub = _dot(h2, wb_ref[...])
    cwa = cwa_ref[...]
    cwb = cwb_ref[...]
    ca = h0a_ref[...] * cwa[0:1, :] + h1a_ref[...] * cwa[1:2, :] + ua * cwa[2:3, :] + cba_ref[...]
    cb = h0b_ref[...] * cwb[0:1, :] + h1b_ref[...] * cwb[1:2, :] + ub * cwb[2:3, :] + cbb_ref[...]
    act_ref[...] = (_silu(ca) * cb).astype(BF16)
    ua_ref[...] = ua
    ub_ref[...] = ub


def _ffn_up_sample(h2, wup, hist0, hist1, cw, cb_row):
    nb = h2.shape[0]
    col = lambda rows, off: pl.BlockSpec((rows, FFN_TN), lambda j: (0, j + off))
    return pl.pallas_call(
        _ffn_up_sample_kernel,
        grid=(FFN_NT,),
        in_specs=[pl.BlockSpec((nb, D_MODEL), lambda j: (0, 0)),
                  col(D_MODEL, 0), col(D_MODEL, FFN_NT), col(nb, 0), col(nb, FFN_NT), col(nb, 0), col(nb, FFN_NT),
                  col(FFN_CONV, 0), col(FFN_CONV, FFN_NT), col(1, 0), col(1, FFN_NT)],
        out_specs=[col(nb, 0), col(nb, 0), col(nb, 0)],
        out_shape=[jax.ShapeDtypeStruct((nb, D_FF), BF16), jax.ShapeDtypeStruct((nb, D_FF), F32),
                   jax.ShapeDtypeStruct((nb, D_FF), F32)],
        compiler_params=_params(("arbitrary",)),
        name="ffn_up_sample",
    )(h2, wup, wup, hist0, hist0, hist1, hist1, cw, cw, cb_row, cb_row)


def _ffn_down_kernel(act_ref, x1_ref, g2_ref, nw_ref, w_ref, y_ref):
    x2 = x1_ref[...] + g2_ref[...] * _dot(act_ref[...], w_ref[...])
    y_ref[...] = x2 * lax.rsqrt(jnp.mean(x2 * x2, axis=-1, keepdims=True) + EPS) * nw_ref[...]


def _ffn_down(act, x1, mod3, nw_row, wdown, tm):
    bx, tx, _ = x1.shape
    r = mod3.shape[1]
    mod_b = (lambda b: b) if mod3.shape[0] == bx else (lambda b: 0)
    return pl.pallas_call(
        _ffn_down_kernel,
        grid=(bx, tx // tm),
        in_specs=[pl.BlockSpec((None, tm, D_FF), lambda b, i: (b, i, 0)),
                  pl.BlockSpec((None, tm, D_MODEL), lambda b, i: (b, i, 0)),
                  pl.BlockSpec((None, r, D_MODEL), lambda b, i: (mod_b(b), 0, 5)),
                  pl.BlockSpec((1, D_MODEL), lambda b, i: (0, 0)),
                  pl.BlockSpec((D_FF, D_MODEL), lambda b, i: (0, 0))],
        out_specs=pl.BlockSpec((None, tm, D_MODEL), lambda b, i: (b, i, 0)),
        out_shape=jax.ShapeDtypeStruct((bx, tx, D_MODEL), F32),
        compiler_params=_params(("arbitrary", "arbitrary")),
        name="ffn_down",
    )(act, x1, mod3, nw_row, wdown)


def _rope_tables(pos):
    half = ROPE_DIM // 2
    inv_freq = ROPE_THETA ** (-np.arange(half, dtype=np.float64) / half)
    ang = np.asarray(pos, np.float64)[:, None] * inv_freq[None, :]
    cos, sin = np.cos(ang).astype(np.float32), np.sin(ang).astype(np.float32)
    n = ang.shape[0]
    ones = np.ones((n, ATT_HEAD_DIM - ROPE_DIM), np.float32)
    zeros8 = np.zeros((n, half), np.float32)
    zeros48 = np.zeros((n, ATT_HEAD_DIM - ROPE_DIM), np.float32)
    c = np.concatenate([cos, cos, ones], axis=1)
    s1 = np.concatenate([-sin, zeros8, zeros48], axis=1)
    s2 = np.concatenate([zeros8, sin, zeros48], axis=1)
    return tuple(jnp.asarray(np.concatenate([t, t], axis=1)) for t in (c, s1, s2))


def _overlap(nc, ns):
    cst = np.arange(nc)[:, None] * CMP_STRIDE
    sst = np.arange(ns)[None, :] * SEL_LEN
    ov = np.clip(np.minimum(cst + CMP_LEN, sst + SEL_LEN) - np.maximum(cst, sst), 0, None)
    return (ov / CMP_STRIDE).astype(np.float32)


def _cmp_weights(pe, w1, w2):
    k16 = CMP_STRIDE * ATT_HEAD_DIM
    w1ab = jnp.concatenate([w1[:CMP_STRIDE].reshape(k16, CMP_HIDDEN), w1[CMP_STRIDE:].reshape(k16, CMP_HIDDEN)], axis=1)
    return (w1ab.astype(BF16), w1.reshape(CMP_LEN * ATT_HEAD_DIM, CMP_HIDDEN).astype(BF16), pe.reshape(1, -1),
            w2.astype(BF16))


def kernel(x_prompt, x_sample, c_prompt, c_sample, cache_nsa_kv, page_table, cache_win_kv, state_ssm, state_ssm_conv, state_ffn_conv, ada_w, ada_b, norm1_w, norm2_w, final_norm_w, w_in, ssm_conv_w, ssm_conv_b, ssm_dt_bias, ssm_A_log, ssm_D, ssm_norm_w, cmp_pe_k, cmp_w1_k, cmp_w2_k, cmp_pe_v, cmp_w1_v, cmp_w2_v, w_ssm_out, w_att_out, w_out, ffn_w_up, ffn_conv_w, ffn_conv_b, ffn_w_down):
    bp, tp, _ = x_prompt.shape
    nb = x_sample.shape[0]
    npages = page_table.shape[1]
    past = npages * PAGE_SIZE

    o_z, o_xbc, o_dt = 0, D_INNER, D_INNER + CONV_DIM
    o_q = o_dt + N_SSM_HEADS
    o_kv = o_q + ATT_WIDTH
    o_ag = o_kv + 6 * 256
    o_mg = o_ag + 3 * N_ATT_HEADS
    w_r = jnp.concatenate([
        w_in[:, o_z:o_z + 2048], w_in[:, o_xbc:o_xbc + 2048], w_in[:, o_q:o_q + 1024], w_in[:, o_mg:o_mg + 2048],
        w_in[:, o_xbc + 2048:o_xbc + 3072], w_in[:, o_kv:o_kv + 1536], w_in[:, o_dt:o_dt + 32],
        w_in[:, o_ag:o_ag + 48], jnp.zeros((D_MODEL, 48), F32)], axis=1).astype(BF16)
    ada_w_bf = ada_w.astype(BF16)
    wss = w_ssm_out.astype(BF16)
    wat = w_att_out.astype(BF16)
    wo = w_out.astype(BF16)
    wup = ffn_w_up.astype(BF16)
    wdown = ffn_w_down.astype(BF16)
    row = lambda v: v.reshape(1, -1)
    pad128 = lambda v: jnp.pad(v, (0, 128 - v.shape[0])).reshape(1, 128)
    dvec = jnp.repeat(ssm_D, SSM_HEAD_DIM).reshape(1, D_INNER)
    cmpw_k = _cmp_weights(cmp_pe_k, cmp_w1_k, cmp_w2_k)
    cmpw_v = _cmp_weights(cmp_pe_v, cmp_w1_v, cmp_w2_v)
    wbd, w1f, pe2, w2bd = (jnp.stack([a, b]) for a, b in zip(cmpw_k, cmpw_v))

    npad = (-(bp + nb)) % 8
    c_all = jnp.concatenate([c_prompt, c_sample, jnp.zeros((npad, D_MODEL), F32)], axis=0)
    mod = _ada_mod(c_all, ada_w_bf, row(ada_b))
    mod_p = mod[:bp].reshape(bp, 1, 6 * D_MODEL)
    mod_s = mod[bp:bp + nb].reshape(1, nb, 6 * D_MODEL)

    proj_p = _inproj(x_prompt, mod_p, row(norm1_w), w_r, 1024)
    yssm_p, conv_p, hlast_p = _ssd_prompt(proj_p, ssm_conv_w, row(ssm_conv_b), pad128(ssm_dt_bias),
                                          pad128(ssm_A_log), dvec, row(ssm_norm_w))
    tabs_p = _rope_tables(np.arange(tp))
    q_p, kvrows_p, ksa_p, khm_p, gates_p, win_p = _rope_prep(proj_p, *tabs_p, 512)
    kc_p, vc_p = _compress_prompt(kvrows_p, wbd, w1f, pe2, w2bd)
    nblk = -(-tp // SEL_LEN)
    ncmp = tp // CMP_STRIDE
    ovt = np.zeros((nblk, ncmp), np.float32)
    ovt[:, :ncmp - 1] = _overlap(ncmp - 1, nblk).T
    yatt_p = _nsa_prompt(q_p, gates_p, kc_p, vc_p, ksa_p, khm_p, jnp.asarray(ovt))
    x1_p, h2_p = _merge_out(x_prompt, yssm_p, yatt_p, proj_p, mod_p, row(norm2_w), wss, wat, wo, 512)
    act_p, fa_p, fb_p = _ffn_up_prompt(h2_p, wup, ffn_conv_w, row(ffn_conv_b), 512)
    y_p = _ffn_down(act_p, x1_p, mod_p, row(final_norm_w), wdown, 512)

    x_s3 = x_sample.reshape(1, nb, D_MODEL)
    proj_s = _inproj(x_s3, mod_s, row(norm1_w), w_r, nb)
    eh = (np.arange(D_INNER)[None, :] // SSM_HEAD_DIM == np.arange(128)[:, None]).astype(np.float32)
    xdt, dec_e, yd, bm_c, cm_c, xdt_t, dec_t = _ssm_pre(
        proj_s, state_ssm_conv[:, 0], state_ssm_conv[:, 1], state_ssm_conv[:, 2], ssm_conv_w, row(ssm_conv_b),
        pad128(ssm_dt_bias), pad128(ssm_A_log), dvec, jnp.asarray(eh))
    h_new, yoff = _ssm_state(state_ssm.reshape(nb, D_INNER, SSM_STATE), xdt_t, dec_t, bm_c, cm_c)
    yssm_s = _ssm_post(yd, yoff[:, 0, :], dec_e, proj_s, row(ssm_norm_w))
    tabs_s = _rope_tables(np.full((nb,), past))
    q_s, kvrows_s, _, _, _, wnew_s = _rope_prep(proj_s, *tabs_s, nb)
    cache3 = jnp.transpose(cache_nsa_kv, (0, 2, 3, 4, 1))
    win_t = jnp.transpose(cache_win_kv, (0, 2, 3, 4, 1))
    kc_s, vc_s = _compress_sample(page_table, cache3, wbd, w1f, pe2, w2bd)
    ncmp_s = past // CMP_STRIDE
    nsel_s = past // SEL_LEN + 1
    ov_s = np.zeros((ncmp_s, 128), np.float32)
    ov_s[:ncmp_s - 1, :nsel_s] = _overlap(ncmp_s - 1, nsel_s)
    e_s = (np.arange(past)[None, :] // SEL_LEN == np.arange(128)[:, None]).astype(np.float32)
    att_g_s = proj_s[0, :, COL_TAIL + 32:COL_TAIL + 80].reshape(nb, N_ATT_HEADS, 3)
    yatt_s16, win_s = _nsa_sample(
        page_table, cache3, q_s.reshape(nb, N_ATT_HEADS, ATT_HEAD_DIM), att_g_s, kvrows_s.reshape(nb, 1, 1024),
        wnew_s.reshape(nb, 1, 512), kc_s, vc_s, win_t, jnp.asarray(ov_s, dtype=BF16), jnp.asarray(e_s, dtype=BF16))
    x1_s, h2_s = _merge_out(x_s3, yssm_s.reshape(1, nb, D_INNER), yatt_s16.reshape(1, nb, ATT_WIDTH), proj_s, mod_s,
                            row(norm2_w), wss, wat, wo, nb)
    act_s, ua_s, ub_s = _ffn_up_sample(h2_s.reshape(nb, D_MODEL), wup, state_ffn_conv[:, 0], state_ffn_conv[:, 1],
                                       ffn_conv_w, row(ffn_conv_b))
    y_s = _ffn_down(act_s.reshape(1, nb, D_FF), x1_s, mod_s, row(final_norm_w), wdown, nb)

    xbc_s = jnp.concatenate([proj_s[0, :, COL_XS:COL_XS + 2048], proj_s[0, :, COL_BM:COL_BM + 1024]], axis=1)
    conv_s = jnp.stack([state_ssm_conv[:, 1], state_ssm_conv[:, 2], xbc_s], axis=1)
    ffn_s = jnp.stack([state_ffn_conv[:, 1], jnp.concatenate([ua_s, ub_s], axis=1)], axis=1)
    return (y_p,
            y_s.reshape(nb, 1, D_MODEL),
            kvrows_p.reshape(bp, tp, 4, N_KV_HEADS, ATT_HEAD_DIM),
            kvrows_s.reshape(nb, 1, 4, N_KV_HEADS, ATT_HEAD_DIM),
            win_p.reshape(bp, WINDOW, 2, N_KV_HEADS, ATT_HEAD_DIM),
            jnp.transpose(win_s, (0, 4, 1, 2, 3)),
            hlast_p.reshape(bp, N_SSM_HEADS, SSM_HEAD_DIM, SSM_STATE),
            h_new.reshape(nb, N_SSM_HEADS, SSM_HEAD_DIM, SSM_STATE),
            conv_p,
            conv_s,
            jnp.concatenate([fa_p[:, -1], fb_p[:, -1]], axis=2),
            ffn_s)
```

```python
import functools

import numpy as np
import jax
import jax.numpy as jnp
from jax import lax
from jax.experimental import pallas as pl
from jax.experimental.pallas import tpu as pltpu

F32 = jnp.float32
BF16 = jnp.bfloat16
HIGHEST = lax.Precision.HIGHEST

D_MODEL = 1024
D_INNER = 2048
N_SSM_HEADS = 32
SSM_HEAD_DIM = 64
SSM_STATE = 128
SSM_GROUPS = 4
SSM_CONV = 4
CONV_DIM = 3072
SSD_CHUNK = 128
N_ATT_HEADS = 16
ATT_HEAD_DIM = 64
N_KV_HEADS = 4
Q_PER_KV = 4
ATT_WIDTH = 1024
ATT_SCALE = ATT_HEAD_DIM ** -0.5
LOG2E = 1.4426950408889634
ROPE_DIM = 16
ROPE_THETA = 500000.0
CMP_LEN = 32
CMP_STRIDE = 16
CMP_HIDDEN = 128
SEL_LEN = 64
SEL_TOPK = 16
WINDOW = 512
Q_BLOCK = 128
PAGE_SIZE = 128
D_FF = 2816
FFN_CONV = 3
EPS = 1e-6
NEG = -1e30
BIG = 1e30
TINY = 1e-30

COL_Z = 0
COL_XS = 2048
COL_Q = 4096
COL_GM = 5120
COL_BM = 7168
COL_CM = 7680
COL_KV = 8192
COL_TAIL = 9728
N_PROJ = 9856
PROJ_TN = 1408
VMEM_LIMIT = 56 * 1024 * 1024


def _sig(x):
    return 1.0 / (1.0 + jnp.exp(-x))


def _silu(x):
    hx = 0.5 * x
    return hx + hx * jnp.tanh(hx)


def _softplus(x):
    return jnp.maximum(x, 0.0) + jnp.log(1.0 + jnp.exp(-jnp.abs(x)))


def _dot(a, b):
    return jnp.dot(a, b, preferred_element_type=F32)


def _dot_hi(a, b):
    return jnp.dot(a, b, preferred_element_type=F32, precision=HIGHEST)


def _dot_nt(a, b):
    return lax.dot_general(a, b, (((1,), (1,)), ((), ())), preferred_element_type=F32)


def _dot_nt_hi(a, b):
    return lax.dot_general(a, b, (((1,), (1,)), ((), ())), preferred_element_type=F32, precision=HIGHEST)


def _params(sem):
    return pltpu.CompilerParams(dimension_semantics=sem, vmem_limit_bytes=VMEM_LIMIT)


def _ada_kernel(c_ref, w_ref, b_ref, o_ref):
    c = c_ref[...]
    o_ref[...] = _dot(_silu(c).astype(BF16), w_ref[...]) + b_ref[...]


def _ada_mod(c_all, w_bf, b_row):
    m = c_all.shape[0]
    tn = 512
    return pl.pallas_call(
        _ada_kernel,
        grid=(w_bf.shape[1] // tn,),
        in_specs=[pl.BlockSpec((m, D_MODEL), lambda j: (0, 0)),
                  pl.BlockSpec((D_MODEL, tn), lambda j: (0, j)),
                  pl.BlockSpec((1, tn), lambda j: (0, j))],
        out_specs=pl.BlockSpec((m, tn), lambda j: (0, j)),
        out_shape=jax.ShapeDtypeStruct((m, w_bf.shape[1]), F32),
        compiler_params=_params(("arbitrary",)),
        name="ada_mod",
    )(c_all, w_bf, b_row)


def _inproj_kernel(x_ref, sh_ref, sc_ref, nw_ref, w_ref, o_ref, h_scr):
    @pl.when(pl.program_id(2) == 0)
    def _():
        x = x_ref[...]
        y = x * lax.rsqrt(jnp.mean(x * x, axis=-1, keepdims=True) + EPS)
        h = y * nw_ref[...] * (1.0 + sc_ref[...]) + sh_ref[...]
        h_scr[...] = h.astype(BF16)

    o_ref[...] = _dot(h_scr[...], w_ref[...])


def _inproj(x3, mod3, nw_row, w_bf, tm):
    bx, tx, _ = x3.shape
    r = mod3.shape[1]
    mod_b = (lambda b: b) if mod3.shape[0] == bx else (lambda b: 0)
    return pl.pallas_call(
        _inproj_kernel,
        grid=(bx, tx // tm, N_PROJ // PROJ_TN),
        in_specs=[pl.BlockSpec((None, tm, D_MODEL), lambda b, i, j: (b, i, 0)),
                  pl.BlockSpec((None, r, D_MODEL), lambda b, i, j: (mod_b(b), 0, 0)),
                  pl.BlockSpec((None, r, D_MODEL), lambda b, i, j: (mod_b(b), 0, 1)),
                  pl.BlockSpec((1, D_MODEL), lambda b, i, j: (0, 0)),
                  pl.BlockSpec((D_MODEL, PROJ_TN), lambda b, i, j: (0, j))],
        out_specs=pl.BlockSpec((None, tm, PROJ_TN), lambda b, i, j: (b, i, j)),
        out_shape=jax.ShapeDtypeStruct((bx, tx, N_PROJ), F32),
        scratch_shapes=[pltpu.VMEM((tm, D_MODEL), BF16)],
        compiler_params=_params(("arbitrary", "arbitrary", "arbitrary")),
        name="inproj",
    )(x3, mod3, mod3, nw_row, w_bf)


def _ssd_kernel(z_ref, xs_ref, bm_ref, cm_ref, dt_ref, cw_ref, cb_ref, dtb_ref, alog_ref, dvec_ref, nw_ref,
                y_ref, conv_ref, hout_ref, xpad, h_t, *, nchunks):
    c = pl.program_id(1)
    ln = SSD_CHUNK

    @pl.when(c == 0)
    def _():
        xpad[...] = jnp.zeros((8, CONV_DIM), F32)
        h_t[...] = jnp.zeros_like(h_t)

    cw = cw_ref[...]
    x_cur = jnp.concatenate([xs_ref[...], bm_ref[...], cm_ref[...]], axis=1)
    tail = xpad[...]
    row8 = lax.broadcasted_iota(jnp.int32, (8, CONV_DIM), 0)
    conv = x_cur * cw[3:4, :] + cb_ref[...]
    for k in range(1, SSM_CONV):
        rolled = pltpu.roll(x_cur, k, axis=0)
        head = jnp.where(row8 < k, pltpu.roll(tail, k, axis=0), rolled[0:8])
        conv = conv + jnp.concatenate([head, rolled[8:]], axis=0) * cw[3 - k:4 - k, :]
    xc = _silu(conv)
    xs_c = xc[:, 0:2048]
    bm_c = xc[:, 2048:2560]
    cm_c = xc[:, 2560:3072]

    dt = _softplus(dt_ref[...] + dtb_ref[...])
    a = -jnp.exp(alog_ref[...])
    row = lax.broadcasted_iota(jnp.int32, (ln, ln), 0)
    col = lax.broadcasted_iota(jnp.int32, (ln, ln), 1)
    causal = row >= col
    cs = _dot_hi(causal.astype(F32), dt * a)
    cs_t = cs.T
    dt_t = dt.T

    x_bf = xs_c.astype(BF16)
    bm_bf = bm_c.astype(BF16)
    cm_bf = cm_c.astype(BF16)
    lo = lax.broadcasted_iota(jnp.int32, (1, 128), 1) < 64
    ys = []
    for g in range(SSM_GROUPS):
        bg = bm_bf[:, g * 128:(g + 1) * 128]
        cg = cm_bf[:, g * 128:(g + 1) * 128]
        cb = _dot_nt(cg, bg)
        b_t = bm_c[:, g * 128:(g + 1) * 128].T
        for r2 in range(4):
            pair = g * 4 + r2
            xp = x_bf[:, pair * 128:(pair + 1) * 128]
            yd, st = [], []
            for h in (2 * pair, 2 * pair + 1):
                cs_col = cs[:, h:h + 1]
                cs_row = cs_t[h:h + 1, :]
                dt_row = dt_t[h:h + 1, :]
                lmat = jnp.where(causal, jnp.exp(cs_col - cs_row), 0.0)
                yd.append(_dot((cb * lmat * dt_row).astype(BF16), xp))
                w_row = dt_row * jnp.exp(cs_t[h:h + 1, ln - 1:ln] - cs_row)
                st.append(_dot((b_t * w_row).astype(BF16), xp))
            ha, hb = 2 * pair, 2 * pair + 1
            ecol = jnp.where(lo, jnp.exp(cs[:, ha:ha + 1]), jnp.exp(cs[:, hb:hb + 1]))
            hprev = h_t[pair]
            yoff = _dot(cg, hprev.astype(BF16)) * ecol
            ys.append(jnp.where(lo, yd[0], yd[1]) + yoff)
            edec = jnp.where(lo, jnp.exp(cs_t[ha:ha + 1, ln - 1:ln]), jnp.exp(cs_t[hb:hb + 1, ln - 1:ln]))
            h_t[pair] = hprev * edec + jnp.where(lo, st[0], st[1])
    y = jnp.concatenate(ys, axis=1) + xs_c * dvec_ref[...]
    yz = y * _silu(z_ref[...])
    ms = jnp.mean(yz * yz, axis=-1, keepdims=True)
    y_ref[...] = (yz * lax.rsqrt(ms + EPS) * nw_ref[...]).astype(BF16)

    last = x_cur[ln - 8:ln]
    xpad[...] = last

    @pl.when(c == nchunks - 1)
    def _():
        conv_ref[...] = last[8 - (SSM_CONV - 1):8]
        for pair in range(16):
            hout_ref[pair * 128:(pair + 1) * 128, :] = h_t[pair].T


def _ssd_prompt(proj3, cw, cb_row, dtb_row, alog_row, dvec_row, nw_row):
    bx, tx, _ = proj3.shape
    nchunks = tx // SSD_CHUNK
    ln = SSD_CHUNK
    const = lambda shape: pl.BlockSpec(shape, lambda b, c: (0, 0))
    return pl.pallas_call(
        functools.partial(_ssd_kernel, nchunks=nchunks),
        grid=(bx, nchunks),
        in_specs=[pl.BlockSpec((None, ln, 2048), lambda b, c: (b, c, COL_Z // 2048)),
                  pl.BlockSpec((None, ln, 2048), lambda b, c: (b, c, COL_XS // 2048)),
                  pl.BlockSpec((None, ln, 512), lambda b, c: (b, c, COL_BM // 512)),
                  pl.BlockSpec((None, ln, 512), lambda b, c: (b, c, COL_CM // 512)),
                  pl.BlockSpec((None, ln, 128), lambda b, c: (b, c, COL_TAIL // 128)),
                  const((SSM_CONV, CONV_DIM)), const((1, CONV_DIM)), const((1, 128)), const((1, 128)),
                  const((1, D_INNER)), const((1, D_INNER))],
        out_specs=[pl.BlockSpec((None, ln, D_INNER), lambda b, c: (b, c, 0)),
                   pl.BlockSpec((None, SSM_CONV - 1, CONV_DIM), lambda b, c: (b, 0, 0)),
                   pl.BlockSpec((None, N_SSM_HEADS * SSM_HEAD_DIM, SSM_STATE), lambda b, c: (b, 0, 0))],
        out_shape=[jax.ShapeDtypeStruct((bx, tx, D_INNER), BF16),
                   jax.ShapeDtypeStruct((bx, SSM_CONV - 1, CONV_DIM), F32),
                   jax.ShapeDtypeStruct((bx, N_SSM_HEADS * SSM_HEAD_DIM, SSM_STATE), F32)],
        scratch_shapes=[pltpu.VMEM((8, CONV_DIM), F32), pltpu.VMEM((16, 128, 128), F32)],
        compiler_params=_params(("arbitrary", "arbitrary")),
        name="ssd_prompt",
    )(proj3, proj3, proj3, proj3, proj3, cw, cb_row, dtb_row, alog_row, dvec_row, nw_row)


def _ssm_pre_kernel(xs_ref, bm_ref, cm_ref, dt_ref, s0_ref, s1_ref, s2_ref, cw_ref, cb_ref, dtb_ref, alog_ref,
                    dvec_ref, eh_ref, xdt_ref, dec_ref, yd_ref, bmc_ref, cmc_ref, xdt_t_ref, dec_t_ref):
    cw = cw_ref[...]
    xbc = jnp.concatenate([xs_ref[...], bm_ref[...], cm_ref[...]], axis=1)
    conv = s0_ref[...] * cw[0:1, :] + s1_ref[...] * cw[1:2, :] + s2_ref[...] * cw[2:3, :] + xbc * cw[3:4, :] + cb_ref[...]
    xc = _silu(conv)
    xs_c = xc[:, 0:2048]
    bm_c = xc[:, 2048:2560]
    cm_c = xc[:, 2560:3072]
    dt = _softplus(dt_ref[...] + dtb_ref[...])
    a = -jnp.exp(alog_ref[...])
    dec = jnp.exp(dt * a)
    eh = eh_ref[...]
    dt_e = _dot_hi(dt, eh)
    dec_e = _dot_hi(dec, eh)
    xdt = xs_c * dt_e
    cbs = []
    for g in range(SSM_GROUPS):
        cbg = jnp.sum(cm_c[:, g * 128:(g + 1) * 128] * bm_c[:, g * 128:(g + 1) * 128], axis=-1, keepdims=True)
        cbs.append(jnp.broadcast_to(cbg, (cbg.shape[0], 512)))
    cb_e = jnp.concatenate(cbs, axis=1)
    xdt_ref[...] = xdt
    dec_ref[...] = dec_e
    yd_ref[...] = cb_e * xdt + xs_c * dvec_ref[...]
    bmc_ref[...] = bm_c
    cmc_ref[...] = cm_c
    for k in range(16):
        xdt_t_ref[k * 128:(k + 1) * 128, :] = xdt[:, k * 128:(k + 1) * 128].T
        dec_t_ref[k * 128:(k + 1) * 128, :] = dec_e[:, k * 128:(k + 1) * 128].T


def _ssm_pre(proj3, s0, s1, s2, cw, cb_row, dtb_row, alog_row, dvec_row, eh):
    nb = proj3.shape[1]
    const = lambda shape: pl.BlockSpec(shape, lambda i: (0,) * len(shape))
    return pl.pallas_call(
        _ssm_pre_kernel,
        grid=(1,),
        in_specs=[pl.BlockSpec((None, nb, 2048), lambda i: (0, 0, COL_XS // 2048)),
                  pl.BlockSpec((None, nb, 512), lambda i: (0, 0, COL_BM // 512)),
                  pl.BlockSpec((None, nb, 512), lambda i: (0, 0, COL_CM // 512)),
                  pl.BlockSpec((None, nb, 128), lambda i: (0, 0, COL_TAIL // 128)),
                  const((nb, CONV_DIM)), const((nb, CONV_DIM)), const((nb, CONV_DIM)),
                  const((SSM_CONV, CONV_DIM)), const((1, CONV_DIM)), const((1, 128)), const((1, 128)),
                  const((1, D_INNER)), const((128, D_INNER))],
        out_specs=[const((nb, D_INNER)), const((nb, D_INNER)), const((nb, D_INNER)), const((nb, 512)),
                   const((nb, 512)), const((D_INNER, nb)), const((D_INNER, nb))],
        out_shape=[jax.ShapeDtypeStruct((nb, D_INNER), F32)] * 3 + [jax.ShapeDtypeStruct((nb, 512), F32)] * 2
        + [jax.ShapeDtypeStruct((D_INNER, nb), F32)] * 2,
        compiler_params=_params(("arbitrary",)),
        name="ssm_pre",
    )(proj3, proj3, proj3, proj3, s0, s1, s2, cw, cb_row, dtb_row, alog_row, dvec_row, eh)


SSM_STATE_ROWS = 4


def _ssm_state_kernel(h0_ref, xdt_t_ref, dec_t_ref, bm_ref, cm_ref, hn_ref, yoff_ref):
    shift = (128 - pl.program_id(0) * SSM_STATE_ROWS) & 127
    xdt_r = pltpu.roll(xdt_t_ref[...], shift, axis=1)
    dec_r = pltpu.roll(dec_t_ref[...], shift, axis=1)
    for bb in range(SSM_STATE_ROWS):
        h0 = h0_ref[bb]
        xcol = xdt_r[:, bb:bb + 1]
        outs, yoffs = [], []
        for g in range(SSM_GROUPS):
            rows = slice(g * 512, (g + 1) * 512)
            outs.append(xcol[rows] * bm_ref[bb, :, g * 128:(g + 1) * 128])
            cm8 = jnp.broadcast_to(cm_ref[bb, :, g * 128:(g + 1) * 128], (8, 128)).astype(BF16)
            yoffs.append(_dot_nt(cm8, h0[rows].astype(BF16)))
        hn_ref[bb] = h0 * dec_r[:, bb:bb + 1] + jnp.concatenate(outs, axis=0)
        yoff_ref[bb] = jnp.concatenate(yoffs, axis=1)


def _ssm_state(h0, xdt_t, dec_t, bm_c, cm_c):
    nb = h0.shape[0]
    nr = SSM_STATE_ROWS
    assert nb == 128 and nb % nr == 0
    const = lambda shape: pl.BlockSpec(shape, lambda b: (0, 0))
    return pl.pallas_call(
        _ssm_state_kernel,
        grid=(nb // nr,),
        in_specs=[pl.BlockSpec((nr, D_INNER, SSM_STATE), lambda b: (b, 0, 0)),
                  const((D_INNER, nb)), const((D_INNER, nb)),
                  pl.BlockSpec((nr, 1, 512), lambda b: (b, 0, 0)), pl.BlockSpec((nr, 1, 512), lambda b: (b, 0, 0))],
        out_specs=[pl.BlockSpec((nr, D_INNER, SSM_STATE), lambda b: (b, 0, 0)),
                   pl.BlockSpec((nr, 8, D_INNER), lambda b: (b, 0, 0))],
        out_shape=[jax.ShapeDtypeStruct((nb, D_INNER, SSM_STATE), F32),
                   jax.ShapeDtypeStruct((nb, 8, D_INNER), F32)],
        compiler_params=_params(("arbitrary",)),
        name="ssm_state",
    )(h0, xdt_t, dec_t, bm_c.reshape(nb, 1, 512), cm_c.reshape(nb, 1, 512))


def _ssm_post_kernel(yd_ref, yoff_ref, dec_ref, z_ref, nw_ref, y_ref):
    y = yd_ref[...] + yoff_ref[...] * dec_ref[...]
    yz = y * _silu(z_ref[...])
    ms = jnp.mean(yz * yz, axis=-1, keepdims=True)
    y_ref[...] = (yz * lax.rsqrt(ms + EPS) * nw_ref[...]).astype(BF16)


def _ssm_post(yd, yoff, dec_e, proj3, nw_row):
    nb = yd.shape[0]
    const = lambda shape: pl.BlockSpec(shape, lambda i: (0, 0))
    return pl.pallas_call(
        _ssm_post_kernel,
        grid=(1,),
        in_specs=[const((nb, D_INNER)), const((nb, D_INNER)), const((nb, D_INNER)),
                  pl.BlockSpec((None, nb, 2048), lambda i: (0, 0, COL_Z // 2048)), const((1, D_INNER))],
        out_specs=const((nb, D_INNER)),
        out_shape=jax.ShapeDtypeStruct((nb, D_INNER), BF16),
        compiler_params=_params(("arbitrary",)),
        name="ssm_post",
    )(yd, yoff, dec_e, proj3, nw_row)


def _rope128(x, c, s1, s2):
    return x * c + pltpu.roll(x, 120, axis=1) * s1 + pltpu.roll(x, 8, axis=1) * s2


def _rope_kernel(q_ref, k01_ref, k23_ref, k45_ref, tail_ref, c_ref, s1_ref, s2_ref,
                 qo_ref, kvrows_ref, ksa_ref, khm_ref, gates_ref, win_ref, *, ntiles, tr):
    c = c_ref[...]
    s1 = s1_ref[...]
    s2 = s2_ref[...]
    q = q_ref[...]
    qo_ref[...] = (jnp.concatenate(
        [_rope128(q[:, k * 128:(k + 1) * 128], c, s1, s2) for k in range(8)], axis=1
    ) * (ATT_SCALE * LOG2E)).astype(BF16)
    streams = []
    for pref in (k01_ref, k23_ref, k45_ref):
        blk = pref[...]
        kk = jnp.concatenate([_rope128(blk[:, k * 128:(k + 1) * 128], c, s1, s2) for k in range(2)], axis=1)
        streams.append(kk)
        streams.append(blk[:, 256:512])
    kvrows_ref[...] = jnp.concatenate(streams[0:4], axis=1)
    pos = pl.program_id(1) * tr + lax.broadcasted_iota(jnp.int32, (tr, 64), 0)
    own = lax.broadcasted_iota(jnp.int32, (tr, 64), 1) == lax.shift_right_logical(pos, 6)
    extra = jnp.where(own, NEG, 0.0).astype(BF16)
    ksel = streams[2].astype(BF16)
    vsel = streams[3].astype(BF16)
    ones_col = (lax.broadcasted_iota(jnp.int32, (tr, 64), 1) == 0).astype(BF16)
    for h in range(N_KV_HEADS):
        ksa_ref[h] = jnp.concatenate([ksel[:, h * 64:(h + 1) * 64], extra], axis=1)
        ksa_ref[N_KV_HEADS + h] = jnp.concatenate([vsel[:, h * 64:(h + 1) * 64], ones_col], axis=1)
    for i, s in enumerate((3, 4, 5)):
        sb = streams[s].astype(BF16)
        for h in range(N_KV_HEADS):
            khm_ref[i * 4 + h] = sb[:, h * 64:(h + 1) * 64]
    g = _sig(tail_ref[...])
    for hk in range(N_KV_HEADS):
        gates_ref[hk] = g[:, 32 + hk * 12:32 + (hk + 1) * 12]

    @pl.when(pl.program_id(1) == ntiles - 1)
    def _():
        win_ref[...] = jnp.concatenate(streams[4:6], axis=1)


def _rope_prep(proj3, ctab, s1tab, s2tab, tr):
    bx, tx, _ = proj3.shape
    ntiles = tx // tr
    tab = pl.BlockSpec((tr, 128), lambda b, i: (i, 0))
    return pl.pallas_call(
        functools.partial(_rope_kernel, ntiles=ntiles, tr=tr),
        grid=(bx, ntiles),
        in_specs=[pl.BlockSpec((None, tr, 1024), lambda b, i: (b, i, COL_Q // 1024)),
                  pl.BlockSpec((None, tr, 512), lambda b, i: (b, i, COL_KV // 512)),
                  pl.BlockSpec((None, tr, 512), lambda b, i: (b, i, COL_KV // 512 + 1)),
                  pl.BlockSpec((None, tr, 512), lambda b, i: (b, i, COL_KV // 512 + 2)),
                  pl.BlockSpec((None, tr, 128), lambda b, i: (b, i, COL_TAIL // 128)),
                  tab, tab, tab],
        out_specs=[pl.BlockSpec((None, tr, 1024), lambda b, i: (b, i, 0)),
                   pl.BlockSpec((None, tr, 1024), lambda b, i: (b, i, 0)),
                   pl.BlockSpec((None, 8, tr, 128), lambda b, i: (b, 0, i, 0)),
                   pl.BlockSpec((None, 12, tr, 64), lambda b, i: (b, 0, i, 0)),
                   pl.BlockSpec((None, 4, tr, 12), lambda b, i: (b, 0, i, 0)),
                   pl.BlockSpec((None, tr, 512), lambda b, i: (b, 0, 0))],
        out_shape=[jax.ShapeDtypeStruct((bx, tx, 1024), BF16),
                   jax.ShapeDtypeStruct((bx, tx, 1024), F32),
                   jax.ShapeDtypeStruct((bx, 8, tx, 128), BF16),
                   jax.ShapeDtypeStruct((bx, 12, tx, 64), BF16),
                   jax.ShapeDtypeStruct((bx, 4, tx, 12), F32),
                   jax.ShapeDtypeStruct((bx, tr, 512), F32)],
        compiler_params=_params(("arbitrary", "arbitrary")),
        name="rope_prep",
    )(proj3, proj3, proj3, proj3, proj3, ctab, s1tab, s2tab)


def _compress_kernel(*refs, nrefs, npages, head_major):
    if nrefs > 1:
        refs = refs[1:]
    page_refs = refs[:nrefs]
    w1ab_ref, w1f_ref, pe_ref, w2_ref, kc_ref, vc_ref = refs[nrefs:]
    nsub = PAGE_SIZE // CMP_STRIDE
    nj = npages * nsub
    nr = N_KV_HEADS * nj
    ri = lax.broadcasted_iota(jnp.int32, (PAGE_SIZE, PAGE_SIZE), 0)
    ci = lax.broadcasted_iota(jnp.int32, (PAGE_SIZE, PAGE_SIZE), 1)
    perm = (ci == (ri & (nsub - 1)) * CMP_STRIDE + lax.shift_right_logical(ri, 3)).astype(BF16)
    if nrefs == 1:
        pages = [page_refs[0][p * PAGE_SIZE:(p + 1) * PAGE_SIZE, :] for p in range(npages)]
        grouped = [_dot(perm, pg.astype(BF16)) for pg in pages]
    else:
        grouped = [_dot_nt(perm, pr[...].reshape(2 * N_KV_HEADS * ATT_HEAD_DIM, PAGE_SIZE).astype(BF16))
                   for pr in page_refs]
    for st, o_ref in ((0, kc_ref), (1, vc_ref)):
        heads = []
        for h in range(N_KV_HEADS):
            c0 = st * 256 + h * ATT_HEAD_DIM
            rows = [jnp.concatenate([y[s * nsub:(s + 1) * nsub, c0:c0 + ATT_HEAD_DIM] for s in range(CMP_STRIDE)],
                                    axis=1) for y in grouped]
            heads.append(rows[0] if npages == 1 else jnp.concatenate(rows, axis=0))
        a = jnp.concatenate(heads, axis=0).astype(BF16)
        p2 = _dot(a, w1ab_ref[st])
        pe8 = jnp.broadcast_to(pe_ref[st], (8, CMP_LEN * ATT_HEAD_DIM)).astype(BF16)
        pe_t = _dot(pe8, w1f_ref[st])[0:1, :]
        hid = p2[:, 0:CMP_HIDDEN] + pltpu.roll(p2[:, CMP_HIDDEN:2 * CMP_HIDDEN], nr - 1, axis=0) + pe_t
        out = _dot(_silu(hid).astype(BF16), w2_ref[st]).astype(BF16)
        if head_major:
            for h in range(N_KV_HEADS):
                o_ref[h] = out[h * nj:(h + 1) * nj, :]
        else:
            o_ref[...] = jnp.concatenate([out[h * nj:(h + 1) * nj, :] for h in range(N_KV_HEADS)], axis=1)


def _compress_prompt(kv_rows, wbd, w1f, pe, w2bd):
    bx, tx, _ = kv_rows.shape
    nj = tx // CMP_STRIDE
    c4 = lambda shape: pl.BlockSpec(shape, lambda b: (0,) * len(shape))
    return pl.pallas_call(
        functools.partial(_compress_kernel, nrefs=1, npages=tx // PAGE_SIZE, head_major=True),
        grid=(bx,),
        in_specs=[pl.BlockSpec((None, tx, 512), lambda b: (b, 0, 0)),
                  c4(wbd.shape), c4(w1f.shape), c4(pe.shape), c4(w2bd.shape)],
        out_specs=[pl.BlockSpec((None, 4, nj, 64), lambda b: (b, 0, 0, 0))] * 2,
        out_shape=[jax.ShapeDtypeStruct((bx, 4, nj, 64), BF16)] * 2,
        compiler_params=_params(("arbitrary",)),
        name="compress_prompt",
    )(kv_rows, wbd, w1f, pe, w2bd)


def _compress_sample(page_table, cache3, wbd, w1f, pe, w2bd):
    nb, npages = page_table.shape
    nsub = PAGE_SIZE // CMP_STRIDE
    nj = npages * nsub
    c4 = lambda shape: pl.BlockSpec(shape, lambda b, pt: (0,) * len(shape))
    page_specs = [pl.BlockSpec((None, 2, N_KV_HEADS, ATT_HEAD_DIM, PAGE_SIZE),
                               functools.partial(lambda b, pt, p: (pt[b, p], 0, 0, 0, 0), p=p))
                  for p in range(npages)]
    grid_spec = pltpu.PrefetchScalarGridSpec(
        num_scalar_prefetch=1,
        grid=(nb,),
        in_specs=page_specs + [c4(wbd.shape), c4(w1f.shape), c4(pe.shape), c4(w2bd.shape)],
        out_specs=[pl.BlockSpec((None, nj, 256), lambda b, pt: (b, 0, 0))] * 2,
    )
    return pl.pallas_call(
        functools.partial(_compress_kernel, nrefs=npages, npages=npages, head_major=False),
        grid_spec=grid_spec,
        out_shape=[jax.ShapeDtypeStruct((nb, nj, 256), BF16)] * 2,
        compiler_params=_params(("arbitrary",)),
        name="compress_sample",
    )(page_table, *([cache3] * npages), wbd, w1f, pe, w2bd)


KEY_BLOCK = 1024
NSA_QUERY_BLOCK = 256


def _nsa_prompt_kernel(q_ref, g_ref, kc_ref, vc_ref, ksa_ref, vs_ref, kw_ref, vw_ref, ovt_ref,
                       o_ref, s_scr, m_scr, acc_scr, *, qblk):
    qb = pl.program_id(2)
    q0 = qb * qblk
    q4 = q_ref[...]
    nq = Q_PER_KV * qblk
    qs = jnp.concatenate([q4[:, r * 64:(r + 1) * 64] for r in range(Q_PER_KV)], axis=0)
    ncmp = kc_ref.shape[0]
    trow = q0 + (lax.broadcasted_iota(jnp.int32, (nq, 1), 0) & (qblk - 1))

    jl = lax.broadcasted_iota(jnp.int32, (nq, ncmp), 1)
    mask_c = jl <= lax.shift_right_arithmetic(trow - (CMP_LEN - 1), CMP_STRIDE.bit_length() - 1)
    s = jnp.where(mask_c, _dot_nt(qs, kc_ref[...]), NEG)
    e = jnp.where(mask_c, jnp.exp2(s - jnp.max(s, axis=-1, keepdims=True)), 0.0)
    p = e / jnp.maximum(jnp.sum(e, axis=-1, keepdims=True), TINY)
    o_c = _dot(p.astype(BF16), vc_ref[...])
    psum = p[0:qblk] + p[qblk:2 * qblk] + p[2 * qblk:3 * qblk] + p[3 * qblk:4 * qblk]
    imp_t = _dot_nt_hi(ovt_ref[...], psum)
    nblk = imp_t.shape[0]
    tq = q0 + lax.broadcasted_iota(jnp.int32, (nblk, qblk), 1)
    blk = lax.broadcasted_iota(jnp.int32, (nblk, qblk), 0)
    cur = lax.shift_right_logical(tq, 6)
    valid = blk * SEL_LEN <= tq
    forced = (blk == 0) | (blk == cur) | (blk == cur - 1)
    imp = jnp.where(valid, jnp.where(forced, BIG, imp_t), -BIG)
    row8 = lax.broadcasted_iota(jnp.int32, (8, qblk), 0)
    groups = [imp[8 * v:8 * v + 8] for v in range(nblk // 8)]
    ranks = [jnp.zeros((8, qblk), F32) for _ in groups]
    for j in range(nblk):
        rj = imp[j:j + 1, :]
        for v in range(len(groups)):
            if 8 * v > j:
                before = rj >= groups[v]
            elif 8 * v + 7 <= j:
                before = rj > groups[v]
            else:
                before = (rj > groups[v]) | ((rj == groups[v]) & (row8 > j - 8 * v))
            ranks[v] = ranks[v] + jnp.where(before, 1.0, 0.0)
    rank = jnp.concatenate(ranks, axis=0)
    nsel_t = jnp.where(rank < float(SEL_TOPK), 0.0, 1.0)
    nsel_pad = jnp.concatenate([nsel_t, jnp.zeros((128 - nblk, qblk), F32)], axis=0) if nblk < 128 else nsel_t
    nsel = jnp.concatenate([nsel_pad[:, i * 128:(i + 1) * 128].T for i in range(qblk // 128)],
                           axis=0)[:, 0:64].astype(BF16)
    qaug = jnp.concatenate([qs, jnp.concatenate([nsel] * Q_PER_KV, axis=0)], axis=1)

    wlen = WINDOW + qblk
    wstart = pl.multiple_of(jnp.maximum(q0 - WINDOW, 0), 128)
    kw = kw_ref[pl.ds(wstart, wlen), :]
    vw = vw_ref[pl.ds(wstart, wlen), :]
    d = trow[0:qblk] - (wstart + lax.broadcasted_iota(jnp.int32, (qblk, wlen), 1))
    ok = (d >= 0) & (d < WINDOW)
    o_w = []
    for r in range(Q_PER_KV):
        sw = jnp.where(ok, _dot_nt(qs[r * qblk:(r + 1) * qblk], kw), NEG)
        ew = jnp.exp2(sw - jnp.max(sw, axis=-1, keepdims=True))
        o_w.append(_dot(ew.astype(BF16), vw) / jnp.maximum(jnp.sum(ew, axis=-1, keepdims=True), TINY))

    cw = 256
    nck = KEY_BLOCK // cw

    col_minus_row = (lax.broadcasted_iota(jnp.int32, (nq, cw), 1)
                     - (lax.broadcasted_iota(jnp.int32, (nq, cw), 0) & (qblk - 1)))

    def block_scores(jb, chunks, causal):
        base = pl.multiple_of(jb * KEY_BLOCK, KEY_BLOCK)
        mx = None
        for c in chunks:
            sc = _dot_nt(qaug, ksa_ref[pl.ds(base + c * cw, cw), :])
            if causal:
                sc = jnp.where(col_minus_row <= q0 - base - c * cw, sc, NEG)
            s_scr[jb, :, c * cw:(c + 1) * cw] = sc
            for i in range(cw // 128):
                part = sc[:, i * 128:(i + 1) * 128]
                mx = part if mx is None else jnp.maximum(mx, part)
        m_scr[...] = jnp.maximum(m_scr[...], mx)

    def block_values(jb, chunks):
        base = pl.multiple_of(jb * KEY_BLOCK, KEY_BLOCK)
        lo, hi = chunks[0] * cw, (chunks[-1] + 1) * cw
        ps = [jnp.exp2(s_scr[jb, :, c * 128:(c + 1) * 128] - mb).astype(BF16) for c in range(lo // 128, hi // 128)]
        acc_scr[...] = acc_scr[...] + _dot(jnp.concatenate(ps, axis=1), vs_ref[pl.ds(base + lo, hi - lo), :])

    every = tuple(range(nck))
    nfull = lax.shift_right_logical(qb, (KEY_BLOCK // qblk).bit_length() - 1)
    m_scr[...] = jnp.full((nq, 128), -jnp.inf, F32)

    def pass_a(jb, carry):
        block_scores(jb, every, False)
        return carry

    lax.fori_loop(0, nfull, pass_a, 0)
    block_scores(nfull, every, True)
    mb =jnp.broadcast_to(jnp.max(m_scr[...], axis=-1, keepdims=True), (nq, 128))

    acc_scr[...] = jnp.zeros((nq, 128), F32)

    def pass_b(jb, carry):
        block_values(jb, every)
        return carry

    lax.fori_loop(0, nfull, pass_b, 0)
    block_values(nfull, every)
    acc = acc_scr[...]
    o_s = acc[:, 0:64] / jnp.maximum(acc[:, 64:65], TINY)

    g = g_ref[...]
    outs = []
    for r in range(Q_PER_KV):
        rows = slice(r * qblk, (r + 1) * qblk)
        outs.append(g[:, 3 * r:3 * r + 1] * o_c[rows] + g[:, 3 * r + 1:3 * r + 2] * o_s[rows]
                    + g[:, 3 * r + 2:3 * r + 3] * o_w[r])
    o_ref[...] = jnp.concatenate(outs, axis=1).astype(BF16)


def _nsa_prompt(q_r, gates, kc, vc, ksa, khm, ovt):
    bx, tx, _ = q_r.shape
    qblk = NSA_QUERY_BLOCK
    ntiles = tx // qblk
    ncmp = kc.shape[2]
    nq = Q_PER_KV * qblk
    assert tx % KEY_BLOCK == 0 and tx >= WINDOW + qblk and tx <= 64 * SEL_LEN
    kv_spec = lambda s: pl.BlockSpec((None, None, tx, 64), lambda b, h, i: (b, s * 4 + h, 0, 0))
    return pl.pallas_call(
        functools.partial(_nsa_prompt_kernel, qblk=qblk),
        grid=(bx, N_KV_HEADS, ntiles),
        in_specs=[pl.BlockSpec((None, qblk, 256), lambda b, h, i: (b, i, h)),
                  pl.BlockSpec((None, None, qblk, 12), lambda b, h, i: (b, h, i, 0)),
                  pl.BlockSpec((None, None, ncmp, 64), lambda b, h, i: (b, h, 0, 0)),
                  pl.BlockSpec((None, None, ncmp, 64), lambda b, h, i: (b, h, 0, 0)),
                  pl.BlockSpec((None, None, tx, 128), lambda b, h, i: (b, h, 0, 0)),
                  pl.BlockSpec((None, None, tx, 128), lambda b, h, i: (b, N_KV_HEADS + h, 0, 0)),
                  kv_spec(1), kv_spec(2),
                  pl.BlockSpec(ovt.shape, lambda b, h, i: (0, 0))],
        out_specs=pl.BlockSpec((None, qblk, 256), lambda b, h, i: (b, i, h)),
        out_shape=jax.ShapeDtypeStruct((bx, tx, ATT_WIDTH), BF16),
        scratch_shapes=[pltpu.VMEM((tx // KEY_BLOCK, nq, KEY_BLOCK), F32), pltpu.VMEM((nq, 128), F32),
                        pltpu.VMEM((nq, 128), F32)],
        compiler_params=_params(("arbitrary", "arbitrary", "arbitrary")),
        name="nsa_prompt",
    )(q_r, gates, kc, vc, ksa, ksa, khm, khm, ovt)


NSA_SAMPLE_ROWS = 4


def _fold_heads(o256, hmask):
    o = o256 * hmask
    return o[:, 0:64] + o[:, 64:128] + o[:, 128:192] + o[:, 192:256]


def _nsa_sample_kernel(*refs, npages, nbb):
    refs = refs[1:]
    pages = refs[:npages * nbb]
    per_row = refs[npages * nbb:npages * nbb + 7]
    ov_ref, e_ref, o_ref, wout_ref = refs[npages * nbb + 7:]
    for bb in range(nbb):
        _nsa_sample_row(pages[bb * npages:(bb + 1) * npages], *[r.at[bb] for r in per_row], ov_ref, e_ref,
                        o_ref.at[bb], wout_ref.at[bb])


def _nsa_sample_row(pages, q_ref, g_ref, knew_ref, wnew_ref, kc_ref, vc_ref, win_ref, ov_ref, e_ref, o_ref, wout_ref):
    npages = len(pages)
    past = npages * PAGE_SIZE
    ncmp = kc_ref.shape[0]
    ntok = (past - CMP_LEN) // CMP_STRIDE + 1
    nsel = past // SEL_LEN + 1

    q16 = q_ref[...].astype(F32)
    rowh = lax.broadcasted_iota(jnp.int32, (16, 256), 0)
    laneh = lax.broadcasted_iota(jnp.int32, (16, 256), 1)
    hmask = (lax.shift_right_logical(rowh, 2) == lax.shift_right_logical(laneh, 6)).astype(F32)
    qbd_f = jnp.concatenate([q16] * 4, axis=1) * hmask
    qbd = qbd_f.astype(BF16)

    jl = lax.broadcasted_iota(jnp.int32, (16, ncmp), 1)
    mask_c = jl < ntok
    s = jnp.where(mask_c, _dot_nt(qbd, kc_ref[...]), NEG)
    e = jnp.where(mask_c, jnp.exp2(s - jnp.max(s, axis=-1, keepdims=True)), 0.0)
    p_c = e / jnp.maximum(jnp.sum(e, axis=-1, keepdims=True), TINY)
    o_c = _fold_heads(_dot(p_c.astype(BF16), vc_ref[...]), hmask)

    imp = _dot(p_c.astype(BF16), ov_ref[...])
    imp = imp + pltpu.roll(imp, 1, axis=0)
    imp = imp + pltpu.roll(imp, 2, axis=0)
    blk = lax.broadcasted_iota(jnp.int32, (16, 128), 1)
    exists = blk < nsel
    forced = (blk == 0) | (blk == nsel - 1) | (blk == nsel - 2)
    imp = jnp.where(exists, jnp.where(forced, BIG, imp), -BIG)
    rank = jnp.zeros((16, 128), F32)
    for j in range(nsel):
        cj = imp[:, j:j + 1]
        beats = (cj > imp) | ((cj == imp) & (blk > j))
        rank = rank + jnp.where(beats, 1.0, 0.0)
    group_row = (lax.broadcasted_iota(jnp.int32, (16, 128), 0) & 3) == 3
    sel16 = jnp.where((rank < float(min(SEL_TOPK, nsel))) & exists & group_row, 1.0, 0.0)
    sel16 = sel16 + pltpu.roll(sel16, 15, axis=0)
    sel16 = sel16 + pltpu.roll(sel16, 14, axis=0)
    bias_past = (_dot(sel16.astype(BF16), e_ref[...]) - 1.0) * BIG

    knew = knew_ref[...]
    s_new = jnp.sum(qbd_f * knew[:, 512:768], axis=-1, keepdims=True)
    hd = N_KV_HEADS * ATT_HEAD_DIM
    s_past = jnp.concatenate(
        [_dot(qbd, pg[0].reshape(hd, PAGE_SIZE).astype(BF16)) for pg in pages], axis=1) + bias_past
    m = jnp.maximum(jnp.max(s_past, axis=-1, keepdims=True), s_new)
    e_past = jnp.exp2(s_past - m)
    e_new = jnp.exp2(s_new - m)
    acc = e_new * knew[:, 768:1024]
    for p, pg in enumerate(pages):
        acc = acc + _dot_nt(e_past[:, p * 128:(p + 1) * 128].astype(BF16), pg[1].reshape(hd, PAGE_SIZE).astype(BF16))
    lsum = jnp.sum(e_past, axis=-1, keepdims=True) + e_new
    o_s = _fold_heads(acc / jnp.maximum(lsum, TINY), hmask)

    kw_t = win_ref[0].reshape(hd, WINDOW)
    vw_t = win_ref[1].reshape(hd, WINDOW)
    wnew = wnew_ref[...]
    wl = lax.broadcasted_iota(jnp.int32, (16, WINDOW), 1)
    s_w = jnp.where(wl >= 1, _dot(qbd, kw_t.astype(BF16)), NEG)
    s_wn = jnp.sum(qbd_f * wnew[:, 0:256], axis=-1, keepdims=True)
    mw = jnp.maximum(jnp.max(s_w, axis=-1, keepdims=True), s_wn)
    e_w = jnp.where(wl >= 1, jnp.exp2(s_w - mw), 0.0)
    e_wn = jnp.exp2(s_wn - mw)
    acc_w = _dot_nt(e_w.astype(BF16), vw_t.astype(BF16)) + e_wn * wnew[:, 256:512]
    o_w = _fold_heads(acc_w / jnp.maximum(jnp.sum(e_w, axis=-1, keepdims=True) + e_wn, TINY), hmask)

    g = _sig(g_ref[...])
    o_ref[...] = (g[:, 0:1] * o_c + g[:, 1:2] * o_s + g[:, 2:3] * o_w).astype(BF16)

    last = lax.broadcasted_iota(jnp.int32, (hd, 128), 1) == 127
    for i, src in enumerate((kw_t, vw_t)):
        col = jnp.broadcast_to(wnew[:, i * hd:(i + 1) * hd], (128, hd)).T
        rolled = pltpu.roll(src, WINDOW - 1, axis=1)
        out = jnp.concatenate([rolled[:, 0:WINDOW - 128], jnp.where(last, col, rolled[:, WINDOW - 128:])], axis=1)
        wout_ref[i] = out.reshape(N_KV_HEADS, ATT_HEAD_DIM, WINDOW)


def _nsa_sample(page_table, cache3, q16, g16, knew, wnew, kc, vc, win, ov, emat):
    nb, npages = page_table.shape
    ncmp = kc.shape[1]
    c2 = lambda shape: pl.BlockSpec(shape, lambda b, pt: (0,) * len(shape))
    nbb = NSA_SAMPLE_ROWS
    assert nb % nbb == 0
    page_specs = [pl.BlockSpec((None, 2, N_KV_HEADS, ATT_HEAD_DIM, PAGE_SIZE),
                               functools.partial(lambda b, pt, bb, p: (pt[b * nbb + bb, p], 1, 0, 0, 0), bb=bb, p=p))
                  for bb in range(nbb) for p in range(npages)]
    win_spec = pl.BlockSpec((nbb, 2, N_KV_HEADS, ATT_HEAD_DIM, WINDOW), lambda b, pt: (b, 0, 0, 0, 0))
    rows = lambda r, c: pl.BlockSpec((nbb, r, c), lambda b, pt: (b, 0, 0))
    grid_spec = pltpu.PrefetchScalarGridSpec(
        num_scalar_prefetch=1,
        grid=(nb // nbb,),
        in_specs=page_specs + [rows(16, 64), rows(16, 3), rows(1, 1024), rows(1, 512), rows(ncmp, 256),
                               rows(ncmp, 256), win_spec, c2(ov.shape), c2(emat.shape)],
        out_specs=[rows(16, 64), win_spec],
    )
    return pl.pallas_call(
        functools.partial(_nsa_sample_kernel, npages=npages, nbb=nbb),
        grid_spec=grid_spec,
        out_shape=[jax.ShapeDtypeStruct((nb, 16, 64), BF16),
                   jax.ShapeDtypeStruct((nb, 2, N_KV_HEADS, ATT_HEAD_DIM, WINDOW), F32)],
        compiler_params=_params(("arbitrary",)),
        name="nsa_sample",
    )(page_table, *([cache3] * (npages * nbb)), q16, g16, knew, wnew, kc, vc, win, ov, emat)


def _merge_kernel(x_ref, yssm_ref, yatt_ref, gs_ref, ga_ref, g1_ref, sh2_ref, sc2_ref, nw_ref,
                  wss_ref, wat_ref, wo_ref, x1_ref, h2_ref):
    ms = _dot(yssm_ref[...], wss_ref[...])
    ma = _dot(yatt_ref[...], wat_ref[...])
    merged = _sig(gs_ref[...]) * ms + _sig(ga_ref[...]) * ma
    x1 = x_ref[...] + g1_ref[...] * _dot(merged.astype(BF16), wo_ref[...])
    x1_ref[...] = x1
    y = x1 * lax.rsqrt(jnp.mean(x1 * x1, axis=-1, keepdims=True) + EPS)
    h2_ref[...] = (y * nw_ref[...] * (1.0 + sc2_ref[...]) + sh2_ref[...]).astype(BF16)


def _merge_out(x3, yssm, yatt, proj3, mod3, nw_row, wss, wat, wo, tm):
    bx, tx, _ = x3.shape
    r = mod3.shape[1]
    mod_b = (lambda b: b) if mod3.shape[0] == bx else (lambda b: 0)
    row = lambda w, cb: pl.BlockSpec((None, tm, w), lambda b, i: (b, i, cb))
    modc = lambda cb: pl.BlockSpec((None, r, D_MODEL), lambda b, i: (mod_b(b), 0, cb))
    const = lambda shape: pl.BlockSpec(shape, lambda b, i: (0, 0))
    return pl.pallas_call(
        _merge_kernel,
        grid=(bx, tx // tm),
        in_specs=[row(D_MODEL, 0), row(D_INNER, 0), row(ATT_WIDTH, 0),
                  row(1024, COL_GM // 1024), row(1024, COL_GM // 1024 + 1),
                  modc(2), modc(3), modc(4), const((1, D_MODEL)),
                  const(wss.shape), const(wat.shape), const(wo.shape)],
        out_specs=[row(D_MODEL, 0), row(D_MODEL, 0)],
        out_shape=[jax.ShapeDtypeStruct((bx, tx, D_MODEL), F32), jax.ShapeDtypeStruct((bx, tx, D_MODEL), BF16)],
        compiler_params=_params(("arbitrary", "arbitrary")),
        name="merge_out",
    )(x3, yssm, yatt, proj3, proj3, mod3, mod3, mod3, nw_row, wss, wat, wo)


FFN_TN = 1408
FFN_NT = D_FF // FFN_TN


def _ffn_up_prompt_kernel(h2_ref, wa_ref, wb_ref, cwa_ref, cwb_ref, cba_ref, cbb_ref,
                          act_ref, fa_ref, fb_ref, tails, *, tm, ntiles):
    i = pl.program_id(1)
    j = pl.program_id(2)
    h2 = h2_ref[...]
    row8 = lax.broadcasted_iota(jnp.int32, (8, FFN_TN), 0)

    @pl.when(i == 0)
    def _():
        tails[j] = jnp.zeros((2, 8, FFN_TN), F32)

    halves = []
    for half, (w_ref, cw_ref, cb_ref, f_ref) in enumerate(((wa_ref, cwa_ref, cba_ref, fa_ref),
                                                            (wb_ref, cwb_ref, cbb_ref, fb_ref))):
        u = _dot(h2, w_ref[...])
        cw = cw_ref[...]
        tail = tails[j, half]
        conv = u * cw[FFN_CONV - 1:FFN_CONV, :] + cb_ref[...]
        for k in range(1, FFN_CONV):
            rolled = pltpu.roll(u, k, axis=0)
            head = jnp.where(row8 < k, pltpu.roll(tail, k, axis=0), rolled[0:8])
            conv = conv + jnp.concatenate([head, rolled[8:]], axis=0) * cw[FFN_CONV - 1 - k:FFN_CONV - k, :]
        halves.append(conv)
        last = u[tm - 8:tm]
        tails[j, half] = last
        f_ref[...] = last[8 - (FFN_CONV - 1):8]
    act_ref[...] = (_silu(halves[0]) * halves[1]).astype(BF16)


def _ffn_up_prompt(h2, wup, cw, cb_row, tm):
    bx, tx, _ = h2.shape
    ntiles = tx // tm
    return pl.pallas_call(
        functools.partial(_ffn_up_prompt_kernel, tm=tm, ntiles=ntiles),
        grid=(bx, ntiles, FFN_NT),
        in_specs=[pl.BlockSpec((None, tm, D_MODEL), lambda b, i, j: (b, i, 0)),
                  pl.BlockSpec((D_MODEL, FFN_TN), lambda b, i, j: (0, j)),
                  pl.BlockSpec((D_MODEL, FFN_TN), lambda b, i, j: (0, j + FFN_NT)),
                  pl.BlockSpec((FFN_CONV, FFN_TN), lambda b, i, j: (0, j)),
                  pl.BlockSpec((FFN_CONV, FFN_TN), lambda b, i, j: (0, j + FFN_NT)),
                  pl.BlockSpec((1, FFN_TN), lambda b, i, j: (0, j)),
                  pl.BlockSpec((1, FFN_TN), lambda b, i, j: (0, j + FFN_NT))],
        out_specs=[pl.BlockSpec((None, tm, FFN_TN), lambda b, i, j: (b, i, j)),
                   pl.BlockSpec((None, None, FFN_CONV - 1, FFN_TN), lambda b, i, j: (b, i, 0, j)),
                   pl.BlockSpec((None, None, FFN_CONV - 1, FFN_TN), lambda b, i, j: (b, i, 0, j))],
        out_shape=[jax.ShapeDtypeStruct((bx, tx, D_FF), BF16),
                   jax.ShapeDtypeStruct((bx, ntiles, FFN_CONV - 1, D_FF), F32),
                   jax.ShapeDtypeStruct((bx, ntiles, FFN_CONV - 1, D_FF), F32)],
        scratch_shapes=[pltpu.VMEM((FFN_NT, 2, 8, FFN_TN), F32)],
        compiler_params=_params(("arbitrary", "arbitrary", "arbitrary")),
        name="ffn_up_prompt",
    )(h2, wup, wup, cw, cw, cb_row, cb_row)


def _ffn_up_sample_kernel(h2_ref, wa_ref, wb_ref, h0a_ref, h0b_ref, h1a_ref, h1b_ref, cwa_ref, cwb_ref,
                          cba_ref, cbb_ref, act_ref, ua_ref, ub_ref):
    h2 = h2_ref[...]
    ua = _dot(h2, wa_ref[...])
    ub = _dot(h2, wb_ref[...])
    cwa = cwa_ref[...]
    cwb = cwb_ref[...]
    ca = h0a_ref[...] * cwa[0:1, :] + h1a_ref[...] * cwa[1:2, :] + ua * cwa[2:3, :] + cba_ref[...]
    cb = h0b_ref[...] * cwb[0:1, :] + h1b_ref[...] * cwb[1:2, :] + ub * cwb[2:3, :] + cbb_ref[...]
    act_ref[...] = (_silu(ca) * cb).astype(BF16)
    ua_ref[...] = ua
    ub_ref[...] = ub


def _ffn_up_sample(h2, wup, hist0, hist1, cw, cb_row):
    nb = h2.shape[0]
    col = lambda rows, off: pl.BlockSpec((rows, FFN_TN), lambda j: (0, j + off))
    return pl.pallas_call(
        _ffn_up_sample_kernel,
        grid=(FFN_NT,),
        in_specs=[pl.BlockSpec((nb, D_MODEL), lambda j: (0, 0)),
                  col(D_MODEL, 0), col(D_MODEL, FFN_NT), col(nb, 0), col(nb, FFN_NT), col(nb, 0), col(nb, FFN_NT),
                  col(FFN_CONV, 0), col(FFN_CONV, FFN_NT), col(1, 0), col(1, FFN_NT)],
        out_specs=[col(nb, 0), col(nb, 0), col(nb, 0)],
        out_shape=[jax.ShapeDtypeStruct((nb, D_FF), BF16), jax.ShapeDtypeStruct((nb, D_FF), F32),
                   jax.ShapeDtypeStruct((nb, D_FF), F32)],
        compiler_params=_params(("arbitrary",)),
        name="ffn_up_sample",
    )(h2, wup, wup, hist0, hist0, hist1, hist1, cw, cw, cb_row, cb_row)


def _ffn_down_kernel(act_ref, x1_ref, g2_ref, nw_ref, w_ref, y_ref):
    x2 = x1_ref[...] + g2_ref[...] * _dot(act_ref[...], w_ref[...])
    y_ref[...] = x2 * lax.rsqrt(jnp.mean(x2 * x2, axis=-1, keepdims=True) + EPS) * nw_ref[...]


def _ffn_down(act, x1, mod3, nw_row, wdown, tm):
    bx, tx, _ = x1.shape
    r = mod3.shape[1]
    mod_b = (lambda b: b) if mod3.shape[0] == bx else (lambda b: 0)
    return pl.pallas_call(
        _ffn_down_kernel,
        grid=(bx, tx // tm),
        in_specs=[pl.BlockSpec((None, tm, D_FF), lambda b, i: (b, i, 0)),
                  pl.BlockSpec((None, tm, D_MODEL), lambda b, i: (b, i, 0)),
                  pl.BlockSpec((None, r, D_MODEL), lambda b, i: (mod_b(b), 0, 5)),
                  pl.BlockSpec((1, D_MODEL), lambda b, i: (0, 0)),
                  pl.BlockSpec((D_FF, D_MODEL), lambda b, i: (0, 0))],
        out_specs=pl.BlockSpec((None, tm, D_MODEL), lambda b, i: (b, i, 0)),
        out_shape=jax.ShapeDtypeStruct((bx, tx, D_MODEL), F32),
        compiler_params=_params(("arbitrary", "arbitrary")),
        name="ffn_down",
    )(act, x1, mod3, nw_row, wdown)


def _rope_tables(pos):
    half = ROPE_DIM // 2
    inv_freq = ROPE_THETA ** (-np.arange(half, dtype=np.float64) / half)
    ang = np.asarray(pos, np.float64)[:, None] * inv_freq[None, :]
    cos, sin = np.cos(ang).astype(np.float32), np.sin(ang).astype(np.float32)
    n = ang.shape[0]
    ones = np.ones((n, ATT_HEAD_DIM - ROPE_DIM), np.float32)
    zeros8 = np.zeros((n, half), np.float32)
    zeros48 = np.zeros((n, ATT_HEAD_DIM - ROPE_DIM), np.float32)
    c = np.concatenate([cos, cos, ones], axis=1)
    s1 = np.concatenate([-sin, zeros8, zeros48], axis=1)
    s2 = np.concatenate([zeros8, sin, zeros48], axis=1)
    return tuple(jnp.asarray(np.concatenate([t, t], axis=1)) for t in (c, s1, s2))


def _overlap(nc, ns):
    cst = np.arange(nc)[:, None] * CMP_STRIDE
    sst = np.arange(ns)[None, :] * SEL_LEN
    ov = np.clip(np.minimum(cst + CMP_LEN, sst + SEL_LEN) - np.maximum(cst, sst), 0, None)
    return (ov / CMP_STRIDE).astype(np.float32)


def _cmp_weights(pe, w1, w2):
    k16 = CMP_STRIDE * ATT_HEAD_DIM
    w1ab = jnp.concatenate([w1[:CMP_STRIDE].reshape(k16, CMP_HIDDEN), w1[CMP_STRIDE:].reshape(k16, CMP_HIDDEN)], axis=1)
    return (w1ab.astype(BF16), w1.reshape(CMP_LEN * ATT_HEAD_DIM, CMP_HIDDEN).astype(BF16), pe.reshape(1, -1),
            w2.astype(BF16))


def kernel(x_prompt, x_sample, c_prompt, c_sample, cache_nsa_kv, page_table, cache_win_kv, state_ssm, state_ssm_conv, state_ffn_conv, ada_w, ada_b, norm1_w, norm2_w, final_norm_w, w_in, ssm_conv_w, ssm_conv_b, ssm_dt_bias, ssm_A_log, ssm_D, ssm_norm_w, cmp_pe_k, cmp_w1_k, cmp_w2_k, cmp_pe_v, cmp_w1_v, cmp_w2_v, w_ssm_out, w_att_out, w_out, ffn_w_up, ffn_conv_w, ffn_conv_b, ffn_w_down):
    bp, tp, _ = x_prompt.shape
    nb = x_sample.shape[0]
    npages = page_table.shape[1]
    past = npages * PAGE_SIZE

    o_z, o_xbc, o_dt = 0, D_INNER, D_INNER + CONV_DIM
    o_q = o_dt + N_SSM_HEADS
    o_kv = o_q + ATT_WIDTH
    o_ag = o_kv + 6 * 256
    o_mg = o_ag + 3 * N_ATT_HEADS
    w_r = jnp.concatenate([
        w_in[:, o_z:o_z + 2048], w_in[:, o_xbc:o_xbc + 2048], w_in[:, o_q:o_q + 1024], w_in[:, o_mg:o_mg + 2048],
        w_in[:, o_xbc + 2048:o_xbc + 3072], w_in[:, o_kv:o_kv + 1536], w_in[:, o_dt:o_dt + 32],
        w_in[:, o_ag:o_ag + 48], jnp.zeros((D_MODEL, 48), F32)], axis=1).astype(BF16)
    ada_w_bf = ada_w.astype(BF16)
    wss = w_ssm_out.astype(BF16)
    wat = w_att_out.astype(BF16)
    wo = w_out.astype(BF16)
    wup = ffn_w_up.astype(BF16)
    wdown = ffn_w_down.astype(BF16)
    row = lambda v: v.reshape(1, -1)
    pad128 = lambda v: jnp.pad(v, (0, 128 - v.shape[0])).reshape(1, 128)
    dvec = jnp.repeat(ssm_D, SSM_HEAD_DIM).reshape(1, D_INNER)
    cmpw_k = _cmp_weights(cmp_pe_k, cmp_w1_k, cmp_w2_k)
    cmpw_v = _cmp_weights(cmp_pe_v, cmp_w1_v, cmp_w2_v)
    wbd, w1f, pe2, w2bd = (jnp.stack([a, b]) for a, b in zip(cmpw_k, cmpw_v))

    npad = (-(bp + nb)) % 8
    c_all = jnp.concatenate([c_prompt, c_sample, jnp.zeros((npad, D_MODEL), F32)], axis=0)
    mod = _ada_mod(c_all, ada_w_bf, row(ada_b))
    mod_p = mod[:bp].reshape(bp, 1, 6 * D_MODEL)
    mod_s = mod[bp:bp + nb].reshape(1, nb, 6 * D_MODEL)

    proj_p = _inproj(x_prompt, mod_p, row(norm1_w), w_r, 1024)
    yssm_p, conv_p, hlast_p = _ssd_prompt(proj_p, ssm_conv_w, row(ssm_conv_b), pad128(ssm_dt_bias),
                                          pad128(ssm_A_log), dvec, row(ssm_norm_w))
    tabs_p = _rope_tables(np.arange(tp))
    q_p, kvrows_p, ksa_p, khm_p, gates_p, win_p = _rope_prep(proj_p, *tabs_p, 512)
    kc_p, vc_p = _compress_prompt(kvrows_p, wbd, w1f, pe2, w2bd)
    nblk = -(-tp // SEL_LEN)
    ncmp = tp // CMP_STRIDE
    ovt = np.zeros((nblk, ncmp), np.float32)
    ovt[:, :ncmp - 1] = _overlap(ncmp - 1, nblk).T
    yatt_p = _nsa_prompt(q_p, gates_p, kc_p, vc_p, ksa_p, khm_p, jnp.asarray(ovt))
    x1_p, h2_p = _merge_out(x_prompt, yssm_p, yatt_p, proj_p, mod_p, row(norm2_w), wss, wat, wo, 512)
    act_p, fa_p, fb_p = _ffn_up_prompt(h2_p, wup, ffn_conv_w, row(ffn_conv_b), 512)
    y_p = _ffn_down(act_p, x1_p, mod_p, row(final_norm_w), wdown, 512)

    x_s3 = x_sample.reshape(1, nb, D_MODEL)
    proj_s = _inproj(x_s3, mod_s, row(norm1_w), w_r, nb)
    eh = (np.arange(D_INNER)[None, :] // SSM_HEAD_DIM == np.arange(128)[:, None]).astype(np.float32)
    xdt, dec_e, yd, bm_c, cm_c, xdt_t, dec_t = _ssm_pre(
        proj_s, state_ssm_conv[:, 0], state_ssm_conv[:, 1], state_ssm_conv[:, 2], ssm_conv_w, row(ssm_conv_b),
        pad128(ssm_dt_bias), pad128(ssm_A_log), dvec, jnp.asarray(eh))
    h_new, yoff = _ssm_state(state_ssm.reshape(nb, D_INNER, SSM_STATE), xdt_t, dec_t, bm_c, cm_c)
    yssm_s = _ssm_post(yd, yoff[:, 0, :], dec_e, proj_s, row(ssm_norm_w))
    tabs_s = _rope_tables(np.full((nb,), past))
    q_s, kvrows_s, _, _, _, wnew_s = _rope_prep(proj_s, *tabs_s, nb)
    cache3 = jnp.transpose(cache_nsa_kv, (0, 2, 3, 4, 1))
    win_t = jnp.transpose(cache_win_kv, (0, 2, 3, 4, 1))
    kc_s, vc_s = _compress_sample(page_table, cache3, wbd, w1f, pe2, w2bd)
    ncmp_s = past // CMP_STRIDE
    nsel_s = past // SEL_LEN + 1
    ov_s = np.zeros((ncmp_s, 128), np.float32)
    ov_s[:ncmp_s - 1, :nsel_s] = _overlap(ncmp_s - 1, nsel_s)
    e_s = (np.arange(past)[None, :] // SEL_LEN == np.arange(128)[:, None]).astype(np.float32)
    att_g_s = proj_s[0, :, COL_TAIL + 32:COL_TAIL + 80].reshape(nb, N_ATT_HEADS, 3)
    yatt_s16, win_s = _nsa_sample(
        page_table, cache3, q_s.reshape(nb, N_ATT_HEADS, ATT_HEAD_DIM), att_g_s, kvrows_s.reshape(nb, 1, 1024),
        wnew_s.reshape(nb, 1, 512), kc_s, vc_s, win_t, jnp.asarray(ov_s, dtype=BF16), jnp.asarray(e_s, dtype=BF16))
    x1_s, h2_s = _merge_out(x_s3, yssm_s.reshape(1, nb, D_INNER), yatt_s16.reshape(1, nb, ATT_WIDTH), proj_s, mod_s,
                            row(norm2_w), wss, wat, wo, nb)
    act_s, ua_s, ub_s = _ffn_up_sample(h2_s.reshape(nb, D_MODEL), wup, state_ffn_conv[:, 0], state_ffn_conv[:, 1],
                                       ffn_conv_w, row(ffn_conv_b))
    y_s = _ffn_down(act_s.reshape(1, nb, D_FF), x1_s, mod_s, row(final_norm_w), wdown, nb)

    xbc_s = jnp.concatenate([proj_s[0, :, COL_XS:COL_XS + 2048], proj_s[0, :, COL_BM:COL_BM + 1024]], axis=1)
    conv_s = jnp.stack([state_ssm_conv[:, 1], state_ssm_conv[:, 2], xbc_s], axis=1)
    ffn_s = jnp.stack([state_ffn_conv[:, 1], jnp.concatenate([ua_s, ub_s], axis=1)], axis=1)
    return (y_p,
            y_s.reshape(nb, 1, D_MODEL),
            kvrows_p.reshape(bp, tp, 4, N_KV_HEADS, ATT_HEAD_DIM),
            kvrows_s.reshape(nb, 1, 4, N_KV_HEADS, ATT_HEAD_DIM),
            win_p.reshape(bp, WINDOW, 2, N_KV_HEADS, ATT_HEAD_DIM),
            jnp.transpose(win_s, (0, 4, 1, 2, 3)),
            hlast_p.reshape(bp, N_SSM_HEADS, SSM_HEAD_DIM, SSM_STATE),
            h_new.reshape(nb, N_SSM_HEADS, SSM_HEAD_DIM, SSM_STATE),
            conv_p,
            conv_s,
            jnp.concatenate([fa_p[:, -1], fb_p[:, -1]], axis=2),
            ffn_s)
```
